```python
import jax, jax.numpy as jnp
from jax import lax
import numpy as np

D_MODEL = 2048
BATCH = 16
SEQ = 256
DEPTH = 2
DEC_BATCH = 4
DEC_SEQ = 4096
PAST_LEN = 512

GRID_W = 64
HEAD_DIM_A = 128
D_A = D_MODEL // 2
N_HEADS_A = D_A // HEAD_DIM_A
WIN_H = 8
WIN_W = 16
Q_BLOCK = 128
D_B = D_MODEL // 4
N_GROUPS_B = 4
C_B = D_B // N_GROUPS_B
CHUNK = 128
D_C = D_MODEL // 4
POOL_WINDOWS = (2, 4, 8, 16)
N_POOL = len(POOL_WINDOWS)
C_C = D_C // N_POOL
D_MIX = D_A + D_B + D_C
IN_WIDTH = 3 * D_A + 2 * D_B + D_C
SPLITS = (D_A, 2 * D_A, 3 * D_A, 3 * D_A + D_B, 3 * D_A + 2 * D_B)
N_EXPERT_GROUPS = 4
EXPERTS_PER_GROUP = 8
N_EXPERTS = N_EXPERT_GROUPS * EXPERTS_PER_GROUP
TOP_K = 2
D_EXPERT = 512
MOE_BLOCK = 128
N_MOD = 6
EPS = 1e-6

kernel_name = 'hybrid_natten_sgu_pool_hmoe_diffusion_step'


def rms_norm(x, g):
    xf = x.astype(jnp.float32)
    xf = xf * lax.rsqrt(jnp.mean(xf * xf, axis=-1, keepdims=True) + EPS)
    return (xf * g.astype(jnp.float32)).astype(x.dtype)


def context_attention(q, k, v):
    B, H, L, hd = q.shape
    scale = hd ** -0.5
    qb = q.reshape(B, H, L // Q_BLOCK, Q_BLOCK, hd).transpose(2, 0, 1, 3, 4)

    def block(qi):
        s = jnp.einsum('bhqd,bhkd->bhqk', qi, k).astype(jnp.float32) * scale
        pr = jax.nn.softmax(s, axis=-1).astype(v.dtype)
        return jnp.einsum('bhqk,bhkd->bhqd', pr, v)

    o = lax.map(block, qb)
    return o.transpose(1, 2, 0, 3, 4).reshape(B, H, L, hd)


def neighbourhood_attention(q, k, v, ctx_k, ctx_v, rpb):
    B, H, T, hd = q.shape
    rows = T // GRID_W
    kh = min(WIN_H, rows)
    n_loc = kh * WIN_W
    scale = hd ** -0.5
    q5 = q.reshape(B, H, rows, GRID_W, hd)
    k5 = k.reshape(B, H, rows, GRID_W, hd)
    v5 = v.reshape(B, H, rows, GRID_W, hd)
    col = jnp.arange(GRID_W)
    key_cols = jnp.clip(col - WIN_W // 2, 0, GRID_W - WIN_W)[:, None] + jnp.arange(WIN_W)[None, :]
    dc = key_cols - col[:, None]

    def row_block(r):
        rs = jnp.clip(r - kh // 2, 0, rows - kh)
        qr = lax.dynamic_index_in_dim(q5, r, axis=2, keepdims=False)
        kg = lax.dynamic_slice_in_dim(k5, rs, kh, axis=2)[:, :, :, key_cols]
        vg = lax.dynamic_slice_in_dim(v5, rs, kh, axis=2)[:, :, :, key_cols]
        dr = rs + jnp.arange(kh) - r
        bias = rpb[:, dr[None, :, None] + (WIN_H - 1), dc[:, None, :] + (WIN_W - 1)]
        s_loc = jnp.einsum('bhqd,bhiqjd->bhqij', qr, kg).astype(jnp.float32) * scale + bias.astype(jnp.float32)[None]
        s_ctx = jnp.einsum('bhqd,bhkd->bhqk', qr, ctx_k).astype(jnp.float32) * scale
        s = jnp.concatenate([s_loc.reshape(B, H, GRID_W, n_loc), s_ctx], axis=-1)
        pr = jax.nn.softmax(s, axis=-1).astype(v.dtype)
        p_loc = pr[..., :n_loc].reshape(B, H, GRID_W, kh, WIN_W)
        return (jnp.einsum('bhqij,bhiqjd->bhqd', p_loc, vg)
                + jnp.einsum('bhqk,bhkd->bhqd', pr[..., n_loc:], ctx_v))

    out = lax.map(row_block, jnp.arange(rows))
    return out.transpose(1, 2, 0, 3, 4).reshape(B, H, T, hd)


def spatial_gating(u, gv, g_sgu, w_sgu, b_sgu):
    B, T, _ = u.shape
    u = jax.nn.gelu(u)
    gv = rms_norm(jax.nn.gelu(gv), g_sgu).reshape(B, T // CHUNK, CHUNK, N_GROUPS_B, C_B)
    mixed = jnp.einsum('gpq,bnqgc->bnpgc', w_sgu, gv) + b_sgu.T[:, :, None]
    return u * mixed.reshape(B, T, D_B)


def multiscale_pool(p, w_pool, s_pool):
    B, T, _ = p.shape
    pf = p.astype(jnp.float32).reshape(B, T, N_POOL, C_C)
    csum = jnp.concatenate([jnp.zeros((B, 1, N_POOL, C_C), jnp.float32), jnp.cumsum(pf, axis=1)], axis=1)
    half = jnp.array(POOL_WINDOWS, jnp.int32) // 2
    t = jnp.arange(T, dtype=jnp.int32)[:, None]
    lo = jnp.clip(t - half, 0, T)
    hi = jnp.clip(t + half, 0, T)
    grp = jnp.arange(N_POOL)
    win_sum = csum[:, hi, grp] - csum[:, lo, grp]
    pooled = win_sum / (hi - lo).astype(jnp.float32)[None, :, :, None] - pf
    mixed = jnp.einsum('btgc,gcd->btgd', pooled.astype(p.dtype), w_pool)
    return mixed.reshape(B, T, D_C) * s_pool


def routed_experts(h, expert_idx, gate, w_e_gate, w_e_up, w_e_down):
    N, D = h.shape
    A = N * TOP_K
    flat_e = expert_idx.reshape(A)
    order = jnp.argsort(flat_e)
    sorted_e = flat_e[order]
    counts = jnp.bincount(flat_e, length=N_EXPERTS)
    padded = (counts + MOE_BLOCK - 1) // MOE_BLOCK * MOE_BLOCK
    pad_end = jnp.cumsum(padded)
    pad_start = pad_end - padded
    start = jnp.cumsum(counts) - counts
    dest = pad_start[sorted_e] + jnp.arange(A) - start[sorted_e]
    n_blocks = A // MOE_BLOCK + N_EXPERTS
    slot_tok = jnp.full((n_blocks * MOE_BLOCK,), N, jnp.int32).at[dest].set((order // TOP_K).astype(jnp.int32))
    block_e = jnp.minimum(jnp.searchsorted(pad_end, jnp.arange(n_blocks) * MOE_BLOCK, side='right'), N_EXPERTS - 1)
    h_pad = jnp.concatenate([h, jnp.zeros((1, D), h.dtype)], axis=0)
    xb = h_pad[slot_tok].reshape(n_blocks, MOE_BLOCK, D)

    def run_block(args):
        xi, e = args
        return (jax.nn.silu(xi @ w_e_gate[e]) * (xi @ w_e_up[e])) @ w_e_down[e]

    yb = lax.map(run_block, (xb, block_e)).reshape(n_blocks * MOE_BLOCK, D)
    y = jnp.zeros((A, D), yb.dtype).at[order].set(yb[dest])
    return jnp.einsum('nkd,nk->nd', y.reshape(N, TOP_K, D), gate.astype(y.dtype))


def hierarchical_moe(h, w_rg, b_rg, w_re, b_re, w_e_gate, w_e_up, w_e_down):
    N = h.shape[0]
    p_grp = jax.nn.softmax((h @ w_rg).astype(jnp.float32) + b_rg.astype(jnp.float32), axis=-1)
    p_g, g_sel = lax.top_k(p_grp, 1)
    le = ((h @ w_re).astype(jnp.float32) + b_re.astype(jnp.float32)).reshape(N, N_EXPERT_GROUPS, EXPERTS_PER_GROUP)
    le = jnp.take_along_axis(le, g_sel[:, :, None], axis=1)[:, 0]
    p_e, e_local = lax.top_k(jax.nn.softmax(le, axis=-1), TOP_K)
    gate = p_g * p_e / jnp.sum(p_e, axis=-1, keepdims=True)
    expert_idx = g_sel * EXPERTS_PER_GROUP + e_local
    return routed_experts(h, expert_idx, gate, w_e_gate, w_e_up, w_e_down)


def token_mixers(h, lw, ctx_k, ctx_v):
    B, T, _ = h.shape
    z = h @ lw['w_in']
    q, k, v, u, gv, p = jnp.split(z, list(SPLITS), axis=-1)
    heads = lambda t: t.reshape(B, T, N_HEADS_A, HEAD_DIM_A).transpose(0, 2, 1, 3)
    q, k, v = heads(q), heads(k), heads(v)
    if ctx_k is None:
        o_a = context_attention(q, k, v)
    else:
        o_a = neighbourhood_attention(q, k, v, ctx_k, ctx_v, lw['rpb'])
    o_a = o_a.transpose(0, 2, 1, 3).reshape(B, T, D_A)
    o_b = spatial_gating(u, gv, lw['g_sgu'], lw['w_sgu'], lw['b_sgu'])
    o_c = multiscale_pool(p, lw['w_pool'], lw['s_pool'])
    gb = lw['g_branch']
    merged = jnp.concatenate([rms_norm(o_a, gb[:D_A]),
                              rms_norm(o_b, gb[D_A:D_A + D_B]),
                              rms_norm(o_c, gb[D_A + D_B:])], axis=-1)
    return merged @ lw['w_out'], k, v


def trunk_layer(x, mod, lw, ctx_k, ctx_v):
    sh_m, sc_m, gt_m, sh_f, sc_f, gt_f = [m[:, None, :] for m in jnp.split(mod, N_MOD, axis=-1)]
    h = rms_norm(x, lw['g_mix']) * (1 + sc_m) + sh_m
    mix_out, k, v = token_mixers(h, lw, ctx_k, ctx_v)
    x = x + gt_m * mix_out
    h = rms_norm(x, lw['g_ffn']) * (1 + sc_f) + sh_f
    B, T, D = h.shape
    f = hierarchical_moe(h.reshape(B * T, D), lw['w_rg'], lw['b_rg'], lw['w_re'], lw['b_re'],
                         lw['w_e_gate'], lw['w_e_up'], lw['w_e_down'])
    x = x + gt_f * f.reshape(B, T, D)
    return x, k, v


def setup_inputs(seed: int = 0) -> dict:
    key = jax.random.key(seed)
    ks = iter(jax.random.split(key, 27))
    D = D_MODEL

    def nrm(shape, s):
        return jax.random.normal(next(ks), shape, jnp.float32) * s

    return {
        'x_prompt': nrm((BATCH, SEQ, D), 1.0),
        'x_sample': nrm((DEC_BATCH, DEC_SEQ, D), 1.0),
        'cache_k': nrm((DEC_BATCH, DEPTH, N_HEADS_A, PAST_LEN, HEAD_DIM_A), 1.0),
        'cache_v': nrm((DEC_BATCH, DEPTH, N_HEADS_A, PAST_LEN, HEAD_DIM_A), 1.0),
        'c': nrm((DEC_BATCH, D), 1.0),
        'c_ctx': nrm((D,), 1.0),
        'w_mod': nrm((DEPTH, D, N_MOD * D), 0.5 * D ** -0.5),
        'b_mod': nrm((DEPTH, N_MOD * D), 0.02),
        'g_mix': 1.0 + nrm((DEPTH, D), 0.05),
        'g_ffn': 1.0 + nrm((DEPTH, D), 0.05),
        'w_in': nrm((DEPTH, D, IN_WIDTH), D ** -0.5),
        'rpb': nrm((DEPTH, N_HEADS_A, 2 * WIN_H - 1, 2 * WIN_W - 1), 0.1),
        'g_sgu': 1.0 + nrm((DEPTH, D_B), 0.05),
        'w_sgu': nrm((DEPTH, N_GROUPS_B, CHUNK, CHUNK), CHUNK ** -0.5),
        'b_sgu': 1.0 + nrm((DEPTH, N_GROUPS_B, CHUNK), 0.1),
        'w_pool': nrm((DEPTH, N_POOL, C_C, C_C), C_C ** -0.5),
        's_pool': 1.0 + nrm((DEPTH, D_C), 0.1),
        'g_branch': 1.0 + nrm((DEPTH, D_MIX), 0.05),
        'w_out': nrm((DEPTH, D_MIX, D), D_MIX ** -0.5),
        'w_rg': nrm((DEPTH, D, N_EXPERT_GROUPS), D ** -0.5),
        'b_rg': nrm((DEPTH, N_EXPERT_GROUPS), 0.01),
        'w_re': nrm((DEPTH, D, N_EXPERTS), D ** -0.5),
        'b_re': nrm((DEPTH, N_EXPERTS), 0.01),
        'w_e_gate': nrm((DEPTH, N_EXPERTS, D, D_EXPERT), D ** -0.5),
        'w_e_up': nrm((DEPTH, N_EXPERTS, D, D_EXPERT), D ** -0.5),
        'w_e_down': nrm((DEPTH, N_EXPERTS, D_EXPERT, D), D_EXPERT ** -0.5),
        'g_final': 1.0 + nrm((D,), 0.05),
    }


def reference(x_prompt, x_sample, cache_k, cache_v, c, c_ctx, w_mod, b_mod, g_mix, g_ffn, w_in, rpb,
              g_sgu, w_sgu, b_sgu, w_pool, s_pool, g_branch, w_out, w_rg, b_rg, w_re, b_re,
              w_e_gate, w_e_up, w_e_down, g_final):
    xp = x_prompt
    xs = x_sample
    new_k = []
    new_v = []
    for l in range(DEPTH):
        lw = {'g_mix': g_mix[l], 'g_ffn': g_ffn[l], 'w_in': w_in[l], 'rpb': rpb[l], 'g_sgu': g_sgu[l],
              'w_sgu': w_sgu[l], 'b_sgu': b_sgu[l], 'w_pool': w_pool[l], 's_pool': s_pool[l],
              'g_branch': g_branch[l], 'w_out': w_out[l], 'w_rg': w_rg[l], 'b_rg': b_rg[l],
              'w_re': w_re[l], 'b_re': b_re[l], 'w_e_gate': w_e_gate[l], 'w_e_up': w_e_up[l],
              'w_e_down': w_e_down[l]}
        mod_ctx = (jax.nn.silu(c_ctx) @ w_mod[l] + b_mod[l])[None]
        mod_lat = jax.nn.silu(c) @ w_mod[l] + b_mod[l]
        xp, k_l, v_l = trunk_layer(xp, mod_ctx, lw, None, None)
        new_k.append(k_l)
        new_v.append(v_l)
        xs, _, _ = trunk_layer(xs, mod_lat, lw, cache_k[:, l], cache_v[:, l])
    y_prompt = rms_norm(xp, g_final)
    y_sample = rms_norm(xs, g_final)
    new_cache_k = jnp.stack(new_k, axis=1)
    new_cache_v = jnp.stack(new_v, axis=1)
    return (y_prompt, y_sample, new_cache_k, new_cache_v)
```

```python
import functools

import jax
import jax.numpy as jnp
from jax import lax
from jax.experimental import pallas as pl
from jax.experimental.pallas import tpu as pltpu

F32 = jnp.float32
BF16 = jnp.bfloat16
I32 = jnp.int32

D_MODEL = 2048
DEPTH = 2
GRID_W = 64
HEAD_DIM = 128
D_A = 1024
N_HEADS = 8
WIN_H = 8
WIN_W = 16
D_B = 512
N_GROUPS_B = 4
CHUNK = 128
D_C = 512
POOL_WINDOWS = (2, 4, 8, 16)
IN_WIDTH = 3 * D_A + 2 * D_B + D_C
N_GROUPS_E = 4
EXPERTS_PER_GROUP = 8
N_EXPERTS = 32
TOP_K = 2
D_EXPERT = 512
N_MOD = 6
EPS = 1e-6
NEG = -1e30

LANES = 128
SUBLANES = 8
VMEM_LIMIT_BYTES = 56 * 1024 * 1024

TM_IN = 512
TN_IN = 512
TM_OUT = 256
TM_MIX = 256
Q_ROWS = 8
K_ROWS = 16
MOE_BM = 256
TM_ROW = 256
POOL_HALO = 8

COL_BLOCKS = ((0, 24, 0), (24, 16, 16), (40, 24, 32))
KEY_COLS = 32


def _params(sem):
    return pltpu.CompilerParams(dimension_semantics=sem, vmem_limit_bytes=VMEM_LIMIT_BYTES)


def _dot(a, b):
    return jnp.dot(a, b, preferred_element_type=F32)


def _dot_nt(a, b):
    return lax.dot_general(a, b, (((1,), (1,)), ((), ())), preferred_element_type=F32)


def _rms(x, g):
    return x * lax.rsqrt(jnp.mean(x * x, axis=-1, keepdims=True) + EPS) * g


def _mod_kernel(c_ref, w_ref, b_ref, o_ref):
    c = c_ref[...]
    s = c * jax.nn.sigmoid(c)
    o_ref[...] = jnp.dot(s, w_ref[...], preferred_element_type=F32,
                         precision=lax.Precision.HIGHEST) + b_ref[...]


def _modulation(c_all, w_mod, b_mod):
    tn = 1024
    nrow = c_all.shape[0]
    width = w_mod.shape[2]
    return pl.pallas_call(
        _mod_kernel,
        grid=(DEPTH, width // tn),
        in_specs=[
            pl.BlockSpec((nrow, D_MODEL), lambda l, j: (0, 0)),
            pl.BlockSpec((None, D_MODEL, tn), lambda l, j: (l, 0, j)),
            pl.BlockSpec((None, 1, tn), lambda l, j: (l, 0, j)),
        ],
        out_specs=pl.BlockSpec((None, nrow, tn), lambda l, j: (l, 0, j)),
        out_shape=jax.ShapeDtypeStruct((DEPTH, nrow, width), F32),
        compiler_params=_params(("arbitrary", "arbitrary")),
        name="modulation",
    )(c_all, w_mod, b_mod.reshape(DEPTH, 1, width))


def _in_proj_kernel(x_ref, mod_ref, g_ref, w_ref, *refs, with_cache, q_f32):
    if with_cache:
        q_ref, k_ref, v_ref, u_ref, gv_ref, p_ref, kc_ref, vc_ref, h_scr = refs
    else:
        q_ref, k_ref, v_ref, u_ref, gv_ref, p_ref, h_scr = refs
        kc_ref = vc_ref = None
    j = pl.program_id(1)

    @pl.when(j == 0)
    def _():
        x = x_ref[...]
        h = _rms(x, g_ref[...]) * (1.0 + mod_ref[1:2, :]) + mod_ref[0:1, :]
        h_scr[...] = h.astype(BF16)

    acc = _dot(h_scr[...], w_ref[...])
    heads = TN_IN // HEAD_DIM

    def store_heads(ref, dtype):
        for hh in range(heads):
            ref[hh] = acc[:, hh * HEAD_DIM:(hh + 1) * HEAD_DIM].astype(dtype)

    def store_cache(ref):
        seq = ref.shape[2]
        for s in range(TM_IN // seq):
            for hh in range(heads):
                ref[s, hh] = acc[s * seq:(s + 1) * seq, hh * HEAD_DIM:(hh + 1) * HEAD_DIM]

    @pl.when(j < 2)
    def _():
        store_heads(q_ref, F32 if q_f32 else BF16)

    @pl.when((j >= 2) & (j < 4))
    def _():
        store_heads(k_ref, BF16)
        if with_cache:
            store_cache(kc_ref)

    @pl.when((j >= 4) & (j < 6))
    def _():
        store_heads(v_ref, BF16)
        if with_cache:
            store_cache(vc_ref)

    @pl.when(j == 6)
    def _():
        u_ref[...] = acc

    @pl.when(j == 7)
    def _():
        gv_ref[...] = acc

    @pl.when(j == 8)
    def _():
        p_ref[...] = acc


def _in_proj(x2d, row_off, ntok, mod_l, mod_row_fn, g_mix, w_in, seq_len, with_cache, q_f32):
    nblk = ntok // TM_IN
    ncol = IN_WIDTH // TN_IN
    heads = TN_IN // HEAD_DIM
    hm = lambda lo: (lambda i, j: (jnp.clip(j - lo, 0, 1), i, 0))
    out_shape = [
        jax.ShapeDtypeStruct((N_HEADS, ntok, HEAD_DIM), F32 if q_f32 else BF16),
        jax.ShapeDtypeStruct((N_HEADS, ntok, HEAD_DIM), BF16),
        jax.ShapeDtypeStruct((N_HEADS, ntok, HEAD_DIM), BF16),
        jax.ShapeDtypeStruct((ntok, D_B), F32),
        jax.ShapeDtypeStruct((ntok, D_B), F32),
        jax.ShapeDtypeStruct((ntok, D_C), F32),
    ]
    out_specs = [
        pl.BlockSpec((heads, TM_IN, HEAD_DIM), hm(0)),
        pl.BlockSpec((heads, TM_IN, HEAD_DIM), hm(2)),
        pl.BlockSpec((heads, TM_IN, HEAD_DIM), hm(4)),
        pl.BlockSpec((TM_IN, D_B), lambda i, j: (i, 0)),
        pl.BlockSpec((TM_IN, D_B), lambda i, j: (i, 0)),
        pl.BlockSpec((TM_IN, D_C), lambda i, j: (i, 0)),
    ]
    if with_cache:
        nseq = ntok // seq_len
        spb = TM_IN // seq_len
        for lo in (2, 4):
            out_shape.append(jax.ShapeDtypeStruct((nseq, N_HEADS, seq_len, HEAD_DIM), F32))
            out_specs.append(pl.BlockSpec(
                (spb, heads, seq_len, HEAD_DIM),
                functools.partial(lambda i, j, lo: (i, jnp.clip(j - lo, 0, 1), 0, 0), lo=lo)))
    return pl.pallas_call(
        functools.partial(_in_proj_kernel, with_cache=with_cache, q_f32=q_f32),
        grid=(nblk, ncol),
        in_specs=[
            pl.BlockSpec((TM_IN, D_MODEL), lambda i, j: (i + row_off, 0)),
            pl.BlockSpec((None, N_MOD, D_MODEL), lambda i, j: (mod_row_fn(i), 0, 0)),
            pl.BlockSpec((1, D_MODEL), lambda i, j: (0, 0)),
            pl.BlockSpec((D_MODEL, TN_IN), lambda i, j: (0, j)),
        ],
        out_specs=out_specs,
        out_shape=out_shape,
        scratch_shapes=[pltpu.VMEM((TM_IN, D_MODEL), BF16)],
        compiler_params=_params(("arbitrary", "arbitrary")),
        name="in_proj_ctx" if with_cache else "in_proj_lat",
    )(x2d, mod_l, g_mix, w_in)


def _write_normed_heads(o_scr, gb_ref, o_ref):
    ss = None
    for h in range(N_HEADS):
        oh = o_scr[h]
        t = jnp.sum(oh * oh, axis=-1, keepdims=True)
        ss = t if ss is None else ss + t
    r = lax.rsqrt(ss * (1.0 / D_A) + EPS)
    for h in range(N_HEADS):
        sl = slice(h * HEAD_DIM, (h + 1) * HEAD_DIM)
        o_ref[:, sl] = (o_scr[h] * r * gb_ref[:, sl]).astype(o_ref.dtype)


def _ctx_attn_kernel(q_ref, k_ref, v_ref, gb_ref, o_ref, o_scr):
    scale = HEAD_DIM ** -0.5

    def head(h, carry):
        s = _dot_nt(q_ref[h], k_ref[h]) * scale
        m = jnp.max(s, axis=-1, keepdims=True)
        p = jnp.exp(s - m)
        l = jnp.sum(p, axis=-1, keepdims=True)
        o_scr[h] = _dot(p.astype(BF16), v_ref[h]) / l
        return carry

    lax.fori_loop(0, N_HEADS, head, 0)
    _write_normed_heads(o_scr, gb_ref, o_ref)


def _ctx_attention(q, k, v, gb_a, seq_len):
    ntok = q.shape[1]
    spec = pl.BlockSpec((N_HEADS, seq_len, HEAD_DIM), lambda b: (0, b, 0))
    return pl.pallas_call(
        _ctx_attn_kernel,
        grid=(ntok // seq_len,),
        in_specs=[spec, spec, spec, pl.BlockSpec((1, D_A), lambda b: (0, 0))],
        out_specs=pl.BlockSpec((seq_len, D_A), lambda b: (b, 0)),
        out_shape=jax.ShapeDtypeStruct((ntok, D_A), BF16),
        scratch_shapes=[pltpu.VMEM((N_HEADS, seq_len, HEAD_DIM), F32)],
        compiler_params=_params(("arbitrary",)),
        name="ctx_attention",
    )(q, k, v, gb_a)


def _local_bias_table(rpb, rows):
    n_rb = rows // Q_ROWS
    tables = []
    for jb in (0, 1, n_rb - 1):
        r0 = Q_ROWS * jb
        ks = min(max(r0 - WIN_H // 2, 0), rows - K_ROWS)
        blocks = []
        for (c0, nc, kc0) in COL_BLOCKS:
            r = r0 + jnp.arange(Q_ROWS)[:, None, None, None]
            c = c0 + jnp.arange(nc)[None, :, None, None]
            kr = ks + jnp.arange(K_ROWS)[None, None, :, None]
            kc = kc0 + jnp.arange(KEY_COLS)[None, None, None, :]
            rs = jnp.clip(r - WIN_H // 2, 0, rows - WIN_H)
            cs = jnp.clip(c - WIN_W // 2, 0, GRID_W - WIN_W)
            valid = (kr >= rs) & (kr < rs + WIN_H) & (kc >= cs) & (kc < cs + WIN_W)
            ri = jnp.clip(kr - r + WIN_H - 1, 0, 2 * WIN_H - 2)
            ci = jnp.clip(kc - c + WIN_W - 1, 0, 2 * WIN_W - 2)
            shape = (Q_ROWS, nc, K_ROWS, KEY_COLS)
            ri = jnp.broadcast_to(ri, shape)
            ci = jnp.broadcast_to(ci, shape)
            b = rpb[:, ri, ci]
            b = jnp.where(valid[None], b, NEG)
            blocks.append(b.reshape(N_HEADS, Q_ROWS * nc, K_ROWS * KEY_COLS))
        tables.append(jnp.concatenate(blocks, axis=1))
    return jnp.stack(tables, axis=1).astype(F32)


def _nbr_attn_kernel(q_ref, k0, k1, k2, k3, v0, v1, v2, v3, ck_ref, cv_ref, bias_ref, gb_ref,
                     o_ref, o_scr):
    scale = HEAD_DIM ** -0.5
    k_refs = (k0, k1, k2, k3)
    v_refs = (v0, v1, v2, v3)
    rows_per_ref = K_ROWS // len(k_refs)

    def window(refs, h, kc0):
        parts = []
        for kr in range(K_ROWS):
            ref = refs[kr // rows_per_ref]
            parts.append(ref[h, pl.ds((kr % rows_per_ref) * GRID_W + kc0, KEY_COLS), :])
        return jnp.concatenate(parts, axis=0)

    def head(h, carry):
        kc = ck_ref[h].astype(BF16)
        vc = cv_ref[h].astype(BF16)
        row = 0
        for (c0, nc, kc0) in COL_BLOCKS:
            nq = Q_ROWS * nc
            qi = jnp.concatenate(
                [q_ref[h, pl.ds(qr * GRID_W + c0, nc), :] for qr in range(Q_ROWS)], axis=0
            ).astype(BF16)
            ki = window(k_refs, h, kc0)
            vi = window(v_refs, h, kc0)
            s_loc = _dot_nt(qi, ki) * scale + bias_ref[h, pl.ds(row, nq), :]
            s_ctx = _dot_nt(qi, kc) * scale
            m = jnp.maximum(jnp.max(s_loc, axis=-1, keepdims=True),
                            jnp.max(s_ctx, axis=-1, keepdims=True))
            p_loc = jnp.exp(s_loc - m)
            p_ctx = jnp.exp(s_ctx - m)
            l = jnp.sum(p_loc, axis=-1, keepdims=True) + jnp.sum(p_ctx, axis=-1, keepdims=True)
            o = (_dot(p_loc.astype(BF16), vi) + _dot(p_ctx.astype(BF16), vc)) / l
            for qr in range(Q_ROWS):
                o_scr[h, pl.ds(qr * GRID_W + c0, nc), :] = o[qr * nc:(qr + 1) * nc]
            row += nq
        return carry

    lax.fori_loop(0, N_HEADS, head, 0)
    _write_normed_heads(o_scr, gb_ref, o_ref)


def _nbr_attention(q, k, v, cache_k, cache_v, layer, bias_tab, gb_a, n_req, seq_len):
    rows = seq_len // GRID_W
    n_rb = rows // Q_ROWS
    tq = Q_ROWS * GRID_W
    n_kref = 4
    tk = (K_ROWS // n_kref) * GRID_W
    kblk_per_req = seq_len // tk
    ctx_len = cache_k.shape[3]

    def kmap(m):
        def f(b, jb):
            start = jnp.clip(2 * jb - 1, 0, kblk_per_req - n_kref)
            return (0, b * kblk_per_req + start + m, 0)
        return f

    kspecs = [pl.BlockSpec((N_HEADS, tk, HEAD_DIM), kmap(m)) for m in range(n_kref)]
    cspec = pl.BlockSpec((None, None, N_HEADS, ctx_len, HEAD_DIM), lambda b, jb: (b, layer, 0, 0, 0))
    kind = lambda b, jb: (0, (jb > 0).astype(I32) + (jb == n_rb - 1).astype(I32), 0, 0)
    return pl.pallas_call(
        _nbr_attn_kernel,
        grid=(n_req, n_rb),
        in_specs=[pl.BlockSpec((N_HEADS, tq, HEAD_DIM), lambda b, jb: (0, b * n_rb + jb, 0))]
        + kspecs + kspecs + [
            cspec, cspec,
            pl.BlockSpec((N_HEADS, None, tq, K_ROWS * KEY_COLS), kind),
            pl.BlockSpec((1, D_A), lambda b, jb: (0, 0)),
        ],
        out_specs=pl.BlockSpec((tq, D_A), lambda b, jb: (b * n_rb + jb, 0)),
        out_shape=jax.ShapeDtypeStruct((n_req * seq_len, D_A), BF16),
        scratch_shapes=[pltpu.VMEM((N_HEADS, tq, HEAD_DIM), F32)],
        compiler_params=_params(("arbitrary", "arbitrary")),
        name="nbr_attention",
    )(q, k, k, k, k, v, v, v, v, cache_k, cache_v, bias_tab, gb_a)


def _mixers_kernel(u_ref, gv_ref, p_ref, pprev_ref, pnext_ref, gsgu_ref, wsgu_ref, bsgu_ref,
                   wpool_ref, spool_ref, gb_ref, o_ref, ext_scr, *, seq_len):
    i = pl.program_id(0)
    blocks_per_seq = seq_len // TM_MIX
    bi = i % blocks_per_seq
    t0 = bi * TM_MIX

    gu = jax.nn.gelu(u_ref[...])
    gg = _rms(jax.nn.gelu(gv_ref[...]), gsgu_ref[...]).astype(BF16)
    cols = []
    for g in range(N_GROUPS_B):
        sl = slice(g * CHUNK, (g + 1) * CHUNK)
        chunks = []
        for n in range(TM_MIX // CHUNK):
            rows = slice(n * CHUNK, (n + 1) * CHUNK)
            chunks.append(_dot(wsgu_ref[g], gg[rows, sl]) + bsgu_ref[:, g:g + 1])
        cols.append(jnp.concatenate(chunks, axis=0))
    o_b = gu * jnp.concatenate(cols, axis=1)
    o_ref[:, 0:D_B] = _rms(o_b, gb_ref[:, 0:D_B]).astype(o_ref.dtype)

    p = p_ref[...]
    ext_scr[pl.ds(0, POOL_HALO), :] = jnp.where(bi > 0, pprev_ref[...], 0.0)
    ext_scr[pl.ds(POOL_HALO, TM_MIX), :] = p
    ext_scr[pl.ds(POOL_HALO + TM_MIX, POOL_HALO), :] = jnp.where(
        bi < blocks_per_seq - 1, pnext_ref[...], 0.0)
    t = t0 + lax.broadcasted_iota(I32, (TM_MIX, 1), 0)
    outs = []
    for g, w in enumerate(POOL_WINDOWS):
        half = w // 2
        sl = slice(g * CHUNK, (g + 1) * CHUNK)
        acc = None
        for d in range(-half, half):
            part = ext_scr[pl.ds(POOL_HALO + d, TM_MIX), sl]
            acc = part if acc is None else acc + part
        cnt = (jnp.minimum(t + half, seq_len) - jnp.maximum(t - half, 0)).astype(F32)
        pooled = acc / cnt - p[:, sl]
        outs.append(_dot(pooled.astype(BF16), wpool_ref[g]))
    o_c = jnp.concatenate(outs, axis=1) * spool_ref[...]
    o_ref[:, D_B:D_B + D_C] = _rms(o_c, gb_ref[:, D_B:D_B + D_C]).astype(o_ref.dtype)


def _mixers(u, gv, p, g_sgu, w_sgu, b_sgu_t, w_pool, s_pool, gb_bc, seq_len):
    ntok = u.shape[0]
    nblk = ntok // TM_MIX
    hb = TM_MIX // POOL_HALO
    n_halo = ntok // POOL_HALO
    blk = pl.BlockSpec((TM_MIX, D_B), lambda i: (i, 0))
    full = lambda shape: pl.BlockSpec(shape, lambda i: (0,) * len(shape))
    return pl.pallas_call(
        functools.partial(_mixers_kernel, seq_len=seq_len),
        grid=(nblk,),
        in_specs=[
            blk, blk, blk,
            pl.BlockSpec((POOL_HALO, D_C), lambda i: (jnp.maximum(i * hb - 1, 0), 0)),
            pl.BlockSpec((POOL_HALO, D_C), lambda i: (jnp.minimum((i + 1) * hb, n_halo - 1), 0)),
            full((1, D_B)), full((N_GROUPS_B, CHUNK, CHUNK)), full((CHUNK, N_GROUPS_B)),
            full((len(POOL_WINDOWS), CHUNK, CHUNK)), full((1, D_C)), full((1, D_B + D_C)),
        ],
        out_specs=pl.BlockSpec((TM_MIX, D_B + D_C), lambda i: (i, 0)),
        out_shape=jax.ShapeDtypeStruct((ntok, D_B + D_C), BF16),
        scratch_shapes=[pltpu.VMEM((TM_MIX + 2 * POOL_HALO, D_C), F32)],
        compiler_params=_params(("arbitrary",)),
        name="mixers",
    )(u, gv, p, p, p, g_sgu, w_sgu, b_sgu_t, w_pool, s_pool, gb_bc)


def _out_proj_kernel(x_ref, oa_ref, obc_ref, mod_ref, g_ref, w_ref, wr_ref, br_ref,
                     x1_ref, h_ref, e_ref, gate_ref):
    merged = jnp.concatenate([oa_ref[...], obc_ref[...]], axis=1)
    mix = _dot(merged, w_ref[...])
    x1 = x_ref[...] + mod_ref[2:3, :] * mix
    x1_ref[...] = x1
    h = _rms(x1, g_ref[...]) * (1.0 + mod_ref[4:5, :]) + mod_ref[3:4, :]
    h_ref[...] = h

    logits = jnp.dot(h, wr_ref[...], preferred_element_type=F32,
                     precision=lax.Precision.HIGHEST) + br_ref[...]
    lane = lax.broadcasted_iota(I32, logits.shape, 1)
    big = jnp.int32(LANES)

    def softmax_masked(mask):
        z = jnp.where(mask, logits, NEG)
        m = jnp.max(z, axis=-1, keepdims=True)
        e = jnp.where(mask, jnp.exp(z - m), 0.0)
        return e / jnp.sum(e, axis=-1, keepdims=True)

    def top1(vals, mask):
        v = jnp.where(mask, vals, -1.0)
        best = jnp.max(v, axis=-1, keepdims=True)
        idx = jnp.min(jnp.where(mask & (v == best), lane, big), axis=-1, keepdims=True)
        return best, idx

    gmask = lane < N_GROUPS_E
    p_grp = softmax_masked(gmask)
    p_g, g_sel = top1(p_grp, gmask)
    lo = N_GROUPS_E + EXPERTS_PER_GROUP * g_sel
    emask = (lane >= lo) & (lane < lo + EXPERTS_PER_GROUP)
    p_exp = softmax_masked(emask)
    p1, i1 = top1(p_exp, emask)
    p2, i2 = top1(p_exp, emask & (lane != i1))
    denom = p1 + p2
    g1 = p_g * p1 / denom
    g2 = p_g * p2 / denom
    e_ref[...] = jnp.where(lane == 0, i1 - N_GROUPS_E, jnp.where(lane == 1, i2 - N_GROUPS_E, 0))
    gate_ref[...] = jnp.where(lane == 0, g1, jnp.where(lane == 1, g2, 0.0))


def _out_proj(x2d, row_off, ntok, o_a, o_bc, mod_l, mod_row_fn, g_ffn, w_out, w_r, b_r):
    nblk = ntok // TM_OUT
    full = lambda shape: pl.BlockSpec(shape, lambda i: (0,) * len(shape))
    return pl.pallas_call(
        _out_proj_kernel,
        grid=(nblk,),
        in_specs=[
            pl.BlockSpec((TM_OUT, D_MODEL), lambda i: (i + row_off, 0)),
            pl.BlockSpec((TM_OUT, D_A), lambda i: (i, 0)),
            pl.BlockSpec((TM_OUT, D_B + D_C), lambda i: (i, 0)),
            pl.BlockSpec((None, N_MOD, D_MODEL), lambda i: (mod_row_fn(i), 0, 0)),
            full((1, D_MODEL)), full((D_MODEL, D_MODEL)), full((D_MODEL, LANES)), full((1, LANES)),
        ],
        out_specs=[
            pl.BlockSpec((TM_OUT, D_MODEL), lambda i: (i, 0)),
            pl.BlockSpec((TM_OUT, D_MODEL), lambda i: (i, 0)),
            pl.BlockSpec((TM_OUT, LANES), lambda i: (i, 0)),
            pl.BlockSpec((TM_OUT, LANES), lambda i: (i, 0)),
        ],
        out_shape=[
            jax.ShapeDtypeStruct((ntok, D_MODEL), F32),
            jax.ShapeDtypeStruct((ntok, D_MODEL), F32),
            jax.ShapeDtypeStruct((ntok, LANES), I32),
            jax.ShapeDtypeStruct((ntok, LANES), F32),
        ],
        compiler_params=_params(("arbitrary",)),
        name="out_proj_router",
    )(x2d, o_a, o_bc, mod_l, g_ffn, w_out, w_r, b_r)


def _route_plan(e_flat):
    n_assign = e_flat.shape[0]
    onehot = (e_flat[:, None] == jnp.arange(N_EXPERTS, dtype=I32)[None, :]).astype(I32)
    csum = jnp.cumsum(onehot, axis=0)
    counts = csum[-1]
    rank = jnp.sum(onehot * (csum - 1), axis=1)
    padded = (counts + MOE_BM - 1) // MOE_BM * MOE_BM
    pad_end = jnp.cumsum(padded)
    pad_start = pad_end - padded
    pos = jnp.sum(onehot * pad_start[None, :], axis=1) + rank
    n_blocks = n_assign // MOE_BM + N_EXPERTS
    starts = jnp.arange(n_blocks, dtype=I32) * MOE_BM
    block_e = jnp.minimum(jnp.sum((starts[:, None] >= pad_end[None, :]).astype(I32), axis=1),
                          N_EXPERTS - 1).astype(I32)
    n_used = (pad_end[-1] // MOE_BM).astype(I32).reshape(1)
    return pos.astype(I32), block_e, n_used, n_blocks


def _scatter_kernel(pos_ref, h_ref, xs_in_ref, xs_ref, sem):
    del xs_in_ref

    def issue(r, carry):
        for k in range(TOP_K):
            slot = pos_ref[0, 0, TOP_K * r + k]
            pltpu.make_async_copy(h_ref.at[pl.ds(r, 1)], xs_ref.at[pl.ds(slot, 1)], sem).start()
        return carry

    lax.fori_loop(0, TM_ROW, issue, 0, unroll=8)
    for _ in range(TOP_K):
        pltpu.make_async_copy(h_ref, h_ref, sem).wait()


def _scatter_rows(pos3, h, xs):
    ntok = h.shape[0]
    return pl.pallas_call(
        _scatter_kernel,
        grid=(ntok // TM_ROW,),
        in_specs=[
            pl.BlockSpec((1, 1, TOP_K * TM_ROW), lambda i: (i, 0, 0), memory_space=pltpu.SMEM),
            pl.BlockSpec((TM_ROW, D_MODEL), lambda i: (i, 0)),
            pl.BlockSpec(memory_space=pl.ANY),
        ],
        out_specs=pl.BlockSpec(memory_space=pl.ANY),
        out_shape=jax.ShapeDtypeStruct(xs.shape, xs.dtype),
        scratch_shapes=[pltpu.SemaphoreType.DMA(())],
        input_output_aliases={2: 0},
        compiler_params=_params(("arbitrary",)),
        name="moe_scatter",
    )(pos3, h, xs)


def _experts_kernel(be_ref, nu_ref, xs_ref, wg_ref, wu_ref, wd_ref, ys_ref):
    del be_ref
    i = pl.program_id(0)

    @pl.when(i < nu_ref[0])
    def _():
        x = xs_ref[...].astype(BF16)
        g = _dot(x, wg_ref[...])
        u = _dot(x, wu_ref[...])
        a = (g * jax.nn.sigmoid(g) * u).astype(BF16)
        ys_ref[...] = _dot(a, wd_ref[...])

    @pl.when(i >= nu_ref[0])
    def _():
        ys_ref[...] = jnp.zeros_like(ys_ref)


def _experts(block_e, n_used, xs, wg, wu, wd, n_blocks):
    grid_spec = pltpu.PrefetchScalarGridSpec(
        num_scalar_prefetch=2,
        grid=(n_blocks,),
        in_specs=[
            pl.BlockSpec((MOE_BM, D_MODEL), lambda i, be, nu: (jnp.minimum(i, nu[0] - 1), 0)),
            pl.BlockSpec((None, D_MODEL, D_EXPERT), lambda i, be, nu: (be[i], 0, 0)),
            pl.BlockSpec((None, D_MODEL, D_EXPERT), lambda i, be, nu: (be[i], 0, 0)),
            pl.BlockSpec((None, D_EXPERT, D_MODEL), lambda i, be, nu: (be[i], 0, 0)),
        ],
        out_specs=pl.BlockSpec((MOE_BM, D_MODEL), lambda i, be, nu: (i, 0)),
    )
    return pl.pallas_call(
        _experts_kernel,
        grid_spec=grid_spec,
        out_shape=jax.ShapeDtypeStruct(xs.shape, F32),
        compiler_params=_params(("arbitrary",)),
        name="moe_experts",
    )(block_e, n_used, xs, wg, wu, wd)


def _combine_kernel(pos_ref, x1_ref, gate_ref, mod_ref, gfin_ref, ys_ref, o_ref, ybuf, sem,
                    *, final_norm):
    def issue(r, carry):
        for k in range(TOP_K):
            slot = pos_ref[0, 0, TOP_K * r + k]
            pltpu.make_async_copy(ys_ref.at[pl.ds(slot, 1)], ybuf.at[k, pl.ds(r, 1)], sem).start()
        return carry

    lax.fori_loop(0, TM_ROW, issue, 0, unroll=8)
    for k in range(TOP_K):
        pltpu.make_async_copy(ybuf.at[k], ybuf.at[k], sem).wait()
    f = gate_ref[:, 0:1] * ybuf[0] + gate_ref[:, 1:2] * ybuf[1]
    x2 = x1_ref[...] + mod_ref[5:6, :] * f
    if final_norm:
        x2 = _rms(x2, gfin_ref[...])
    o_ref[...] = x2


def _combine(pos3, x1, gate, mod_l, mod_row_fn, g_final, ys, final_norm):
    ntok = x1.shape[0]
    return pl.pallas_call(
        functools.partial(_combine_kernel, final_norm=final_norm),
        grid=(ntok // TM_ROW,),
        in_specs=[
            pl.BlockSpec((1, 1, TOP_K * TM_ROW), lambda i: (i, 0, 0), memory_space=pltpu.SMEM),
            pl.BlockSpec((TM_ROW, D_MODEL), lambda i: (i, 0)),
            pl.BlockSpec((TM_ROW, LANES), lambda i: (i, 0)),
            pl.BlockSpec((None, N_MOD, D_MODEL), lambda i: (mod_row_fn(i), 0, 0)),
            pl.BlockSpec((1, D_MODEL), lambda i: (0, 0)),
            pl.BlockSpec(memory_space=pl.ANY),
        ],
        out_specs=pl.BlockSpec((TM_ROW, D_MODEL), lambda i: (i, 0)),
        out_shape=jax.ShapeDtypeStruct((ntok, D_MODEL), F32),
        scratch_shapes=[pltpu.VMEM((TOP_K, TM_ROW, D_MODEL), F32), pltpu.SemaphoreType.DMA(())],
        compiler_params=_params(("arbitrary",)),
        name="moe_combine",
    )(pos3, x1, gate, mod_l, g_final, ys)


def kernel(x_prompt, x_sample, cache_k, cache_v, c, c_ctx, w_mod, b_mod, g_mix, g_ffn, w_in, rpb,
           g_sgu, w_sgu, b_sgu, w_pool, s_pool, g_branch, w_out, w_rg, b_rg, w_re, b_re,
           w_e_gate, w_e_up, w_e_down, g_final):
    n_ctx, seq, _ = x_prompt.shape
    n_lat, lat_seq, _ = x_sample.shape
    ntok_p = n_ctx * seq
    ntok_s = n_lat * lat_seq
    rows = lat_seq // GRID_W

    n_mod_rows = -(-(1 + n_lat) // SUBLANES) * SUBLANES
    c_all = jnp.concatenate(
        [c_ctx[None], c, jnp.zeros((n_mod_rows - 1 - n_lat, D_MODEL), F32)], axis=0)
    mod = _modulation(c_all, w_mod, b_mod).reshape(DEPTH, n_mod_rows, N_MOD, D_MODEL)

    xp = x_prompt.reshape(ntok_p, D_MODEL)
    xs_lat = x_sample.reshape(ntok_s, D_MODEL)

    def lat_mod_row(tm):
        return lambda i: 1 + (i * tm) // lat_seq

    ctx_mod_row = lambda i: 0

    w_in_b = w_in.astype(BF16)
    w_out_b = w_out.astype(BF16)
    w_sgu_b = w_sgu.astype(BF16)
    w_pool_b = w_pool.astype(BF16)
    wg_b = w_e_gate.astype(BF16)
    wu_b = w_e_up.astype(BF16)
    wd_b = w_e_down.astype(BF16)

    new_k, new_v = [], []
    for l in range(DEPTH):
        mod_l = mod[l]
        gmix = g_mix[l][None]
        gffn = g_ffn[l][None]
        gb = g_branch[l][None]
        gb_a = gb[:, :D_A]
        gb_bc = gb[:, D_A:]
        gsgu = g_sgu[l][None]
        bsgu_t = b_sgu[l].T
        spool = s_pool[l][None]
        w_r = jnp.zeros((D_MODEL, LANES), F32)
        w_r = w_r.at[:, :N_GROUPS_E].set(w_rg[l]).at[:, N_GROUPS_E:N_GROUPS_E + N_EXPERTS].set(w_re[l])
        b_r = jnp.zeros((1, LANES), F32)
        b_r = b_r.at[0, :N_GROUPS_E].set(b_rg[l]).at[0, N_GROUPS_E:N_GROUPS_E + N_EXPERTS].set(b_re[l])
        bias_tab = _local_bias_table(rpb[l], rows)

        qp, kp, vp, up, gvp, pp, kc, vc = _in_proj(
            xp, 0, ntok_p, mod_l, ctx_mod_row, gmix, w_in_b[l], seq, True, False)
        new_k.append(kc)
        new_v.append(vc)
        oa_p = _ctx_attention(qp, kp, vp, gb_a, seq)
        obc_p = _mixers(up, gvp, pp, gsgu, w_sgu_b[l], bsgu_t, w_pool_b[l], spool, gb_bc, seq)
        x1p, hp, ep, gatep = _out_proj(
            xp, 0, ntok_p, oa_p, obc_p, mod_l, ctx_mod_row, gffn, w_out_b[l], w_r, b_r)

        qs, ks, vs, us, gvs, ps = _in_proj(
            xs_lat, 0, ntok_s, mod_l, lat_mod_row(TM_IN), gmix, w_in_b[l], lat_seq, False, True)
        oa_s = _nbr_attention(qs, ks, vs, cache_k, cache_v, l, bias_tab, gb_a, n_lat, lat_seq)
        obc_s = _mixers(us, gvs, ps, gsgu, w_sgu_b[l], bsgu_t, w_pool_b[l], spool, gb_bc, lat_seq)
        x1s, hs, es, gates = _out_proj(
            xs_lat, 0, ntok_s, oa_s, obc_s, mod_l, lat_mod_row(TM_OUT), gffn, w_out_b[l], w_r, b_r)

        e_flat = jnp.concatenate([ep[:, :TOP_K].reshape(-1), es[:, :TOP_K].reshape(-1)])
        pos, block_e, n_used, n_blocks = _route_plan(e_flat)
        pos_p = pos[:TOP_K * ntok_p].reshape(ntok_p // TM_ROW, 1, TOP_K * TM_ROW)
        pos_s = pos[TOP_K * ntok_p:].reshape(ntok_s // TM_ROW, 1, TOP_K * TM_ROW)
        slots = jnp.zeros((n_blocks * MOE_BM, D_MODEL), F32)
        slots = _scatter_rows(pos_p, hp, slots)
        slots = _scatter_rows(pos_s, hs, slots)
        ys = _experts(block_e, n_used, slots, wg_b[l], wu_b[l], wd_b[l], n_blocks)
        last = l == DEPTH - 1
        gfin = g_final[None]
        xp = _combine(pos_p, x1p, gatep, mod_l, ctx_mod_row, gfin, ys, last)
        xs_lat = _combine(pos_s, x1s, gates, mod_l, lat_mod_row(TM_ROW), gfin, ys, last)

    y_prompt = xp.reshape(n_ctx, seq, D_MODEL)
    y_sample = xs_lat.reshape(n_lat, lat_seq, D_MODEL)
    return (y_prompt, y_sample, jnp.stack(new_k, axis=1), jnp.stack(new_v, axis=1))
```

```python
import functools

import jax
import numpy as np
import jax.numpy as jnp
from jax import lax
from jax.experimental import pallas as pl
from jax.experimental.pallas import tpu as pltpu

F32 = jnp.float32
BF16 = jnp.bfloat16
I32 = jnp.int32
U32 = jnp.uint32

D_MODEL = 2048
DEPTH = 2
GRID_W = 64
HEAD_DIM = 128
D_A = 1024
N_HEADS = 8
WIN_H = 8
WIN_W = 16
D_B = 512
N_GROUPS_B = 4
CHUNK = 128
D_C = 512
POOL_WINDOWS = (2, 4, 8, 16)
IN_WIDTH = 3 * D_A + 2 * D_B + D_C
N_GROUPS_E = 4
EXPERTS_PER_GROUP = 8
N_EXPERTS = 32
TOP_K = 2
D_EXPERT = 512
N_MOD = 6
EPS = 1e-6
NEG = -1e30

LANES = 128
SUBLANES = 8
VMEM_LIMIT_BYTES = 56 * 1024 * 1024

TM_IN = 1024
TN_IN = 512
TM_OUT = 512
TM_MIX = 256
Q_ROWS = 8
K_ROWS = 16
MOE_BM = 512
D_PACK = D_MODEL // 2
TM_ROW = 256
POOL_HALO = 8

COL_BLOCKS = ((0, 24, 0), (24, 16, 16), (40, 24, 32))
KEY_COLS = 32


def _params(sem):
    return pltpu.CompilerParams(dimension_semantics=sem, vmem_limit_bytes=VMEM_LIMIT_BYTES)


def _dot(a, b):
    return jnp.dot(a, b, preferred_element_type=F32)


def _dot_nt(a, b):
    return lax.dot_general(a, b, (((1,), (1,)), ((), ())), preferred_element_type=F32)


def _rms(x, g):
    return x * lax.rsqrt(jnp.mean(x * x, axis=-1, keepdims=True) + EPS) * g


def _pack_bf16_pairs(x):
    n = x.shape[1] // 2
    lo = lax.bitcast_convert_type(x[:, :n].astype(F32), U32) >> 16
    hi = lax.bitcast_convert_type(x[:, n:].astype(F32), U32) & jnp.uint32(0xFFFF0000)
    return hi | lo


def _unpack_bf16_pairs(w):
    lo = lax.bitcast_convert_type(w << 16, F32).astype(BF16)
    hi = lax.bitcast_convert_type(w & jnp.uint32(0xFFFF0000), F32).astype(BF16)
    return jnp.concatenate([lo, hi], axis=1)


def _mod_kernel(c_ref, w_ref, b_ref, o_ref):
    c = c_ref[...]
    s = c * jax.nn.sigmoid(c)
    o_ref[...] = jnp.dot(s, w_ref[...], preferred_element_type=F32,
                         precision=lax.Precision.HIGHEST) + b_ref[...]


def _modulation(c_all, w_mod, b_mod):
    tn = 1024
    nrow = c_all.shape[0]
    width = w_mod.shape[2]
    return pl.pallas_call(
        _mod_kernel,
        grid=(DEPTH, width // tn),
        in_specs=[
            pl.BlockSpec((nrow, D_MODEL), lambda l, j: (0, 0)),
            pl.BlockSpec((None, D_MODEL, tn), lambda l, j: (l, 0, j)),
            pl.BlockSpec((None, 1, tn), lambda l, j: (l, 0, j)),
        ],
        out_specs=pl.BlockSpec((None, nrow, tn), lambda l, j: (l, 0, j)),
        out_shape=jax.ShapeDtypeStruct((DEPTH, nrow, width), F32),
        compiler_params=_params(("arbitrary", "arbitrary")),
        name="modulation",
    )(c_all, w_mod, b_mod.reshape(DEPTH, 1, width))


def _in_proj_kernel(x_ref, mod_ref, g_ref, w_ref, *refs, ctx):
    if ctx:
        qkv_ref, ugp_ref, kc_ref, vc_ref, h_scr = refs
    else:
        q_ref, kv_ref, ugp_ref, h_scr = refs
    j = pl.program_id(1)

    @pl.when(j == 0)
    def _():
        x = x_ref[...]
        h = _rms(x, g_ref[...]) * (1.0 + mod_ref[1:2, :]) + mod_ref[0:1, :]
        h_scr[...] = h.astype(BF16)

    acc = _dot(h_scr[...], w_ref[...])
    heads = TN_IN // HEAD_DIM

    def store_heads(ref, dtype):
        for hh in range(heads):
            ref[hh] = acc[:, hh * HEAD_DIM:(hh + 1) * HEAD_DIM].astype(dtype)

    def store_cache(ref):
        seq = ref.shape[2]
        for s in range(TM_IN // seq):
            for hh in range(heads):
                ref[s, hh] = acc[s * seq:(s + 1) * seq, hh * HEAD_DIM:(hh + 1) * HEAD_DIM]

    if ctx:
        @pl.when(j < 6)
        def _():
            store_heads(qkv_ref, BF16)

        @pl.when((j >= 2) & (j < 4))
        def _():
            store_cache(kc_ref)

        @pl.when((j >= 4) & (j < 6))
        def _():
            store_cache(vc_ref)
    else:
        @pl.when(j < 2)
        def _():
            store_heads(q_ref, F32)

        @pl.when((j >= 2) & (j < 6))
        def _():
            store_heads(kv_ref, BF16)

    @pl.when(j >= 6)
    def _():
        ugp_ref[...] = acc


def _in_proj(x2d, ntok, mod, mod_row_fn, g_mix, w_in, layer, seq_len, ctx):
    nblk = ntok // TM_IN
    ncol = IN_WIDTH // TN_IN
    heads = TN_IN // HEAD_DIM
    hm = lambda lo, n: (lambda i, j: (jnp.clip(j - lo, 0, n - 1), i, 0))
    hspec = lambda lo, n: pl.BlockSpec((heads, TM_IN, HEAD_DIM), hm(lo, n))
    ugp_shape = jax.ShapeDtypeStruct((ntok, 2 * D_B + D_C), F32)
    ugp_spec = pl.BlockSpec((TM_IN, TN_IN), lambda i, j: (i, jnp.clip(j - 6, 0, 2)))
    if ctx:
        nseq = ntok // seq_len
        spb = TM_IN // seq_len
        cache_shape = jax.ShapeDtypeStruct((nseq, N_HEADS, seq_len, HEAD_DIM), F32)
        cache_spec = lambda lo: pl.BlockSpec(
            (spb, heads, seq_len, HEAD_DIM), lambda i, j: (i, jnp.clip(j - lo, 0, 1), 0, 0))
        out_shape = [jax.ShapeDtypeStruct((3 * N_HEADS, ntok, HEAD_DIM), BF16), ugp_shape,
                     cache_shape, cache_shape]
        out_specs = [hspec(0, 6), ugp_spec, cache_spec(2), cache_spec(4)]
    else:
        out_shape = [jax.ShapeDtypeStruct((N_HEADS, ntok, HEAD_DIM), F32),
                     jax.ShapeDtypeStruct((2 * N_HEADS, ntok, HEAD_DIM), BF16), ugp_shape]
        out_specs = [hspec(0, 2), hspec(2, 4), ugp_spec]
    return pl.pallas_call(
        functools.partial(_in_proj_kernel, ctx=ctx),
        grid=(nblk, ncol),
        in_specs=[
            pl.BlockSpec((TM_IN, D_MODEL), lambda i, j: (i, 0)),
            pl.BlockSpec((None, None, N_MOD, D_MODEL), lambda i, j: (layer, mod_row_fn(i), 0, 0)),
            pl.BlockSpec((None, 1, D_MODEL), lambda i, j: (layer, 0, 0)),
            pl.BlockSpec((None, None, D_MODEL, TN_IN), lambda i, j: (layer, j, 0, 0)),
        ],
        out_specs=out_specs,
        out_shape=out_shape,
        scratch_shapes=[pltpu.VMEM((TM_IN, D_MODEL), BF16)],
        compiler_params=_params(("arbitrary", "arbitrary")),
        name="in_proj_ctx" if ctx else "in_proj_lat",
    )(x2d, mod, g_mix, w_in)


def _write_normed_heads(o_scr, gb_ref, o_ref):
    ss = None
    for h in range(N_HEADS):
        oh = o_scr[h]
        t = jnp.sum(oh * oh, axis=-1, keepdims=True)
        ss = t if ss is None else ss + t
    r = lax.rsqrt(ss * (1.0 / D_A) + EPS)
    for h in range(N_HEADS):
        sl = slice(h * HEAD_DIM, (h + 1) * HEAD_DIM)
        o_ref[:, sl] = (o_scr[h] * r * gb_ref[:, sl]).astype(o_ref.dtype)


def _ctx_attn_kernel(q_ref, k_ref, v_ref, gb_ref, o_ref, o_scr):
    scale = HEAD_DIM ** -0.5

    def head(h, carry):
        s = _dot_nt(q_ref[h], k_ref[h]) * scale
        m = jnp.max(s, axis=-1, keepdims=True)
        p = jnp.exp(s - m)
        l = jnp.sum(p, axis=-1, keepdims=True)
        o_scr[h] = _dot(p.astype(BF16), v_ref[h]) / l
        return carry

    lax.fori_loop(0, N_HEADS, head, 0)
    _write_normed_heads(o_scr, gb_ref, o_ref)


def _ctx_attention(qkv, gb_a, layer, seq_len):
    ntok = qkv.shape[1]
    spec = lambda part: pl.BlockSpec((N_HEADS, seq_len, HEAD_DIM), lambda b: (part, b, 0))
    return pl.pallas_call(
        _ctx_attn_kernel,
        grid=(ntok // seq_len,),
        in_specs=[spec(0), spec(1), spec(2),
                  pl.BlockSpec((None, 1, D_A), lambda b: (layer, 0, 0))],
        out_specs=pl.BlockSpec((seq_len, D_A), lambda b: (b, 0)),
        out_shape=jax.ShapeDtypeStruct((ntok, D_A), BF16),
        scratch_shapes=[pltpu.VMEM((N_HEADS, seq_len, HEAD_DIM), F32)],
        compiler_params=_params(("arbitrary",)),
        name="ctx_attention",
    )(qkv, qkv, qkv, gb_a)


def _local_bias_table(rpb, rows):
    n_rb = rows // Q_ROWS
    exact = lax.Precision.HIGHEST
    n_dr = 2 * WIN_H - 1
    n_dc = 2 * WIN_W - 1
    tables = []
    for jb in (0, 1, n_rb - 1):
        r0 = Q_ROWS * jb
        ks = min(max(r0 - WIN_H // 2, 0), rows - K_ROWS)
        r = r0 + np.arange(Q_ROWS)[:, None]
        kr = ks + np.arange(K_ROWS)[None, :]
        rs = np.clip(r - WIN_H // 2, 0, rows - WIN_H)
        valid_r = (kr >= rs) & (kr < rs + WIN_H)
        ri = np.clip(kr - r + WIN_H - 1, 0, n_dr - 1)
        sel_r = (ri[..., None] == np.arange(n_dr)).astype(np.float32)
        by_row = jnp.einsum("rka,hab->hrkb", sel_r, rpb, precision=exact)
        blocks = []
        for (c0, nc, kc0) in COL_BLOCKS:
            c = c0 + np.arange(nc)[:, None]
            kc = kc0 + np.arange(KEY_COLS)[None, :]
            cs = np.clip(c - WIN_W // 2, 0, GRID_W - WIN_W)
            valid_c = (kc >= cs) & (kc < cs + WIN_W)
            ci = np.clip(kc - c + WIN_W - 1, 0, n_dc - 1)
            sel_c = (ci[..., None] == np.arange(n_dc)).astype(np.float32)
            b = jnp.einsum("hrkb,cjb->hrckj", by_row, sel_c, precision=exact)
            valid = valid_r[:, None, :, None] & valid_c[None, :, None, :]
            b = jnp.where(valid[None], b, NEG)
            blocks.append(b.reshape(rpb.shape[0], Q_ROWS * nc, K_ROWS * KEY_COLS))
        tables.append(jnp.concatenate(blocks, axis=1))
    return jnp.stack(tables, axis=1).astype(F32)


def _nbr_attn_kernel(q_ref, k0, k1, k2, k3, v0, v1, v2, v3, ck_ref, cv_ref, bias_ref, gb_ref,
                     o_ref, o_scr):
    scale = HEAD_DIM ** -0.5
    k_refs = (k0, k1, k2, k3)
    v_refs = (v0, v1, v2, v3)
    rows_per_ref = K_ROWS // len(k_refs)

    def window(refs, h, kc0):
        parts = []
        for kr in range(K_ROWS):
            ref = refs[kr // rows_per_ref]
            parts.append(ref[h, pl.ds((kr % rows_per_ref) * GRID_W + kc0, KEY_COLS), :])
        return jnp.concatenate(parts, axis=0)

    def head(h, carry):
        kc = ck_ref[h].astype(BF16)
        vc = cv_ref[h].astype(BF16)
        row = 0
        for (c0, nc, kc0) in COL_BLOCKS:
            nq = Q_ROWS * nc
            qi = jnp.concatenate(
                [q_ref[h, pl.ds(qr * GRID_W + c0, nc), :] for qr in range(Q_ROWS)], axis=0
            ).astype(BF16)
            ki = window(k_refs, h, kc0)
            vi = window(v_refs, h, kc0)
            s_loc = _dot_nt(qi, ki) * scale + bias_ref[h, pl.ds(row, nq), :]
            s_ctx = _dot_nt(qi, kc) * scale
            m = jnp.maximum(jnp.max(s_loc, axis=-1, keepdims=True),
                            jnp.max(s_ctx, axis=-1, keepdims=True))
            p_loc = jnp.exp(s_loc - m)
            p_ctx = jnp.exp(s_ctx - m)
            l = jnp.sum(p_loc, axis=-1, keepdims=True) + jnp.sum(p_ctx, axis=-1, keepdims=True)
            o = (_dot(p_loc.astype(BF16), vi) + _dot(p_ctx.astype(BF16), vc)) / l
            for qr in range(Q_ROWS):
                o_scr[h, pl.ds(qr * GRID_W + c0, nc), :] = o[qr * nc:(qr + 1) * nc]
            row += nq
        return carry

    lax.fori_loop(0, N_HEADS, head, 0)
    _write_normed_heads(o_scr, gb_ref, o_ref)


def _nbr_attention(q, kv, cache_k, cache_v, layer, bias_tab, gb_a, n_req, seq_len):
    rows = seq_len // GRID_W
    n_rb = rows // Q_ROWS
    tq = Q_ROWS * GRID_W
    n_kref = 4
    tk = (K_ROWS // n_kref) * GRID_W
    kblk_per_req = seq_len // tk
    ctx_len = cache_k.shape[3]

    def kmap(part, m):
        def f(b, jb):
            start = jnp.clip(2 * jb - 1, 0, kblk_per_req - n_kref)
            return (part, b * kblk_per_req + start + m, 0)
        return f

    kspecs = [pl.BlockSpec((N_HEADS, tk, HEAD_DIM), kmap(0, m)) for m in range(n_kref)]
    vspecs = [pl.BlockSpec((N_HEADS, tk, HEAD_DIM), kmap(1, m)) for m in range(n_kref)]
    cspec = pl.BlockSpec((None, None, N_HEADS, ctx_len, HEAD_DIM), lambda b, jb: (b, layer, 0, 0, 0))
    kind = lambda b, jb: (layer, (jb > 0).astype(I32) + (jb == n_rb - 1).astype(I32), 0, 0)
    return pl.pallas_call(
        _nbr_attn_kernel,
        grid=(n_req, n_rb),
        in_specs=[pl.BlockSpec((N_HEADS, tq, HEAD_DIM), lambda b, jb: (0, b * n_rb + jb, 0))]
        + kspecs + vspecs + [
            cspec, cspec,
            pl.BlockSpec((N_HEADS, None, tq, K_ROWS * KEY_COLS), kind),
            pl.BlockSpec((None, 1, D_A), lambda b, jb: (layer, 0, 0)),
        ],
        out_specs=pl.BlockSpec((tq, D_A), lambda b, jb: (b * n_rb + jb, 0)),
        out_shape=jax.ShapeDtypeStruct((n_req * seq_len, D_A), BF16),
        scratch_shapes=[pltpu.VMEM((N_HEADS, tq, HEAD_DIM), F32)],
        compiler_params=_params(("arbitrary", "arbitrary")),
        name="nbr_attention",
    )(q, *([kv] * (2 * n_kref)), cache_k, cache_v, bias_tab, gb_a)


def _mixers_kernel(u_ref, gv_ref, p_ref, pprev_ref, pnext_ref, gsgu_ref, wsgu_ref, bsgu_ref,
                   wpool_ref, spool_ref, gb_ref, o_ref, ext_scr, *, seq_len):
    i = pl.program_id(0)
    blocks_per_seq = seq_len // TM_MIX
    bi = i % blocks_per_seq
    t0 = bi * TM_MIX

    gu = jax.nn.gelu(u_ref[...])
    gg = _rms(jax.nn.gelu(gv_ref[...]), gsgu_ref[...]).astype(BF16)
    cols = []
    for g in range(N_GROUPS_B):
        sl = slice(g * CHUNK, (g + 1) * CHUNK)
        chunks = []
        for n in range(TM_MIX // CHUNK):
            rows = slice(n * CHUNK, (n + 1) * CHUNK)
            chunks.append(_dot(wsgu_ref[g], gg[rows, sl]) + bsgu_ref[:, g:g + 1])
        cols.append(jnp.concatenate(chunks, axis=0))
    o_b = gu * jnp.concatenate(cols, axis=1)
    o_ref[:, 0:D_B] = _rms(o_b, gb_ref[:, 0:D_B]).astype(o_ref.dtype)

    p = p_ref[...]
    ext_scr[pl.ds(0, POOL_HALO), :] = jnp.where(bi > 0, pprev_ref[...], 0.0)
    ext_scr[pl.ds(POOL_HALO, TM_MIX), :] = p
    ext_scr[pl.ds(POOL_HALO + TM_MIX, POOL_HALO), :] = jnp.where(
        bi < blocks_per_seq - 1, pnext_ref[...], 0.0)
    t = t0 + lax.broadcasted_iota(I32, (TM_MIX, 1), 0)
    outs = []
    for g, w in enumerate(POOL_WINDOWS):
        half = w // 2
        sl = slice(g * CHUNK, (g + 1) * CHUNK)
        acc = None
        for d in range(-half, half):
            part = ext_scr[pl.ds(POOL_HALO + d, TM_MIX), sl]
            acc = part if acc is None else acc + part
        cnt = (jnp.minimum(t + half, seq_len) - jnp.maximum(t - half, 0)).astype(F32)
        pooled = acc / cnt - p[:, sl]
        outs.append(_dot(pooled.astype(BF16), wpool_ref[g]))
    o_c = jnp.concatenate(outs, axis=1) * spool_ref[...]
    o_ref[:, D_B:D_B + D_C] = _rms(o_c, gb_ref[:, D_B:D_B + D_C]).astype(o_ref.dtype)


def _mixers(ugp, g_sgu, w_sgu, b_sgu_t, w_pool, s_pool, gb_bc, layer, seq_len):
    ntok = ugp.shape[0]
    nblk = ntok // TM_MIX
    hb = TM_MIX // POOL_HALO
    n_halo = ntok // POOL_HALO
    blk = lambda part: pl.BlockSpec((TM_MIX, D_B), lambda i: (i, part))
    full = lambda shape: pl.BlockSpec((None,) + shape, lambda i: (layer,) + (0,) * len(shape))
    return pl.pallas_call(
        functools.partial(_mixers_kernel, seq_len=seq_len),
        grid=(nblk,),
        in_specs=[
            blk(0), blk(1), blk(2),
            pl.BlockSpec((POOL_HALO, D_C), lambda i: (jnp.maximum(i * hb - 1, 0), 2)),
            pl.BlockSpec((POOL_HALO, D_C), lambda i: (jnp.minimum((i + 1) * hb, n_halo - 1), 2)),
            full((1, D_B)), full((N_GROUPS_B, CHUNK, CHUNK)), full((CHUNK, N_GROUPS_B)),
            full((len(POOL_WINDOWS), CHUNK, CHUNK)), full((1, D_C)), full((1, D_B + D_C)),
        ],
        out_specs=pl.BlockSpec((TM_MIX, D_B + D_C), lambda i: (i, 0)),
        out_shape=jax.ShapeDtypeStruct((ntok, D_B + D_C), BF16),
        scratch_shapes=[pltpu.VMEM((TM_MIX + 2 * POOL_HALO, D_C), F32)],
        compiler_params=_params(("arbitrary",)),
        name="mixers",
    )(ugp, ugp, ugp, ugp, ugp, g_sgu, w_sgu, b_sgu_t, w_pool, s_pool, gb_bc)


def _out_proj_kernel(x_ref, oa_ref, obc_ref, mod_ref, g_ref, w_ref, wr_ref, br_ref,
                     x1_ref, h_ref, e_ref, gate_ref):
    merged = jnp.concatenate([oa_ref[...], obc_ref[...]], axis=1)
    mix = _dot(merged, w_ref[...])
    x1 = x_ref[...] + mod_ref[2:3, :] * mix
    x1_ref[...] = x1
    h = _rms(x1, g_ref[...]) * (1.0 + mod_ref[4:5, :]) + mod_ref[3:4, :]
    h_hi = h.astype(BF16)
    h_ref[...] = _pack_bf16_pairs(h_hi)

    h_lo = (h - h_hi.astype(F32)).astype(BF16)
    part = _dot(h_hi, wr_ref[...])
    logits = (part[:, :LANES] + part[:, LANES:] + _dot(h_lo, wr_ref[:, :LANES])) + br_ref[...]
    lane = lax.broadcasted_iota(I32, logits.shape, 1)
    big = jnp.int32(LANES)

    def softmax_masked(mask):
        z = jnp.where(mask, logits, NEG)
        m = jnp.max(z, axis=-1, keepdims=True)
        e = jnp.where(mask, jnp.exp(z - m), 0.0)
        return e / jnp.sum(e, axis=-1, keepdims=True)

    def top1(vals, mask):
        v = jnp.where(mask, vals, -1.0)
        best = jnp.max(v, axis=-1, keepdims=True)
        idx = jnp.min(jnp.where(mask & (v == best), lane, big), axis=-1, keepdims=True)
        return best, idx

    gmask = lane < N_GROUPS_E
    p_grp = softmax_masked(gmask)
    p_g, g_sel = top1(p_grp, gmask)
    lo = N_GROUPS_E + EXPERTS_PER_GROUP * g_sel
    emask = (lane >= lo) & (lane < lo + EXPERTS_PER_GROUP)
    p_exp = softmax_masked(emask)
    p1, i1 = top1(p_exp, emask)
    p2, i2 = top1(p_exp, emask & (lane != i1))
    denom = p1 + p2
    g1 = p_g * p1 / denom
    g2 = p_g * p2 / denom
    e_ref[...] = jnp.where(lane == 0, i1 - N_GROUPS_E, jnp.where(lane == 1, i2 - N_GROUPS_E, 0))
    gate_ref[...] = jnp.where(lane == 0, g1, jnp.where(lane == 1, g2, 0.0))


def _out_proj(x2d, ntok, o_a, o_bc, mod, mod_row_fn, g_ffn, w_out, w_r, b_r, layer):
    nblk = ntok // TM_OUT
    full = lambda shape: pl.BlockSpec((None,) + shape, lambda i: (layer,) + (0,) * len(shape))
    return pl.pallas_call(
        _out_proj_kernel,
        grid=(nblk,),
        in_specs=[
            pl.BlockSpec((TM_OUT, D_MODEL), lambda i: (i, 0)),
            pl.BlockSpec((TM_OUT, D_A), lambda i: (i, 0)),
            pl.BlockSpec((TM_OUT, D_B + D_C), lambda i: (i, 0)),
            pl.BlockSpec((None, None, N_MOD, D_MODEL), lambda i: (layer, mod_row_fn(i), 0, 0)),
            full((1, D_MODEL)), full((D_MODEL, D_MODEL)), full((D_MODEL, 2 * LANES)), full((1, LANES)),
        ],
        out_specs=[
            pl.BlockSpec((TM_OUT, D_MODEL), lambda i: (i, 0)),
            pl.BlockSpec((TM_OUT, D_PACK), lambda i: (i, 0)),
            pl.BlockSpec((TM_OUT, LANES), lambda i: (i, 0)),
            pl.BlockSpec((TM_OUT, LANES), lambda i: (i, 0)),
        ],
        out_shape=[
            jax.ShapeDtypeStruct((ntok, D_MODEL), F32),
            jax.ShapeDtypeStruct((ntok, D_PACK), U32),
            jax.ShapeDtypeStruct((ntok, LANES), I32),
            jax.ShapeDtypeStruct((ntok, LANES), F32),
        ],
        compiler_params=_params(("arbitrary",)),
        name="out_proj_router",
    )(x2d, o_a, o_bc, mod, g_ffn, w_out, w_r, b_r)


def _route_plan(e_flat):
    n_assign = e_flat.shape[0]
    onehot = (e_flat[:, None] == jnp.arange(N_EXPERTS, dtype=I32)[None, :]).astype(I32)
    csum = jnp.cumsum(onehot, axis=0)
    counts = csum[-1]
    rank = jnp.sum(onehot * (csum - 1), axis=1)
    padded = (counts + MOE_BM - 1) // MOE_BM * MOE_BM
    pad_end = jnp.cumsum(padded)
    pad_start = pad_end - padded
    pos = jnp.sum(onehot * pad_start[None, :], axis=1) + rank
    n_blocks = n_assign // MOE_BM + N_EXPERTS
    starts = jnp.arange(n_blocks, dtype=I32) * MOE_BM
    block_e = jnp.minimum(jnp.sum((starts[:, None] >= pad_end[None, :]).astype(I32), axis=1),
                          N_EXPERTS - 1).astype(I32)
    n_used = (pad_end[-1] // MOE_BM).astype(I32).reshape(1)
    return pos.astype(I32), block_e, n_used, n_blocks


def _scatter_kernel(pos_ref, h_ref, xs_in_ref, xs_ref, sem):
    del xs_in_ref

    def issue(r, carry):
        for k in range(TOP_K):
            slot = pos_ref[0, 0, TOP_K * r + k]
            pltpu.make_async_copy(h_ref.at[pl.ds(r, 1)], xs_ref.at[pl.ds(slot, 1)], sem).start()
        return carry

    lax.fori_loop(0, TM_ROW, issue, 0, unroll=8)
    for _ in range(TOP_K):
        pltpu.make_async_copy(h_ref, h_ref, sem).wait()


def _scatter_rows(pos3, h, xs):
    ntok = h.shape[0]
    return pl.pallas_call(
        _scatter_kernel,
        grid=(ntok // TM_ROW,),
        in_specs=[
            pl.BlockSpec((1, 1, TOP_K * TM_ROW), lambda i: (i, 0, 0), memory_space=pltpu.SMEM),
            pl.BlockSpec((TM_ROW, D_PACK), lambda i: (i, 0)),
            pl.BlockSpec(memory_space=pl.ANY),
        ],
        out_specs=pl.BlockSpec(memory_space=pl.ANY),
        out_shape=jax.ShapeDtypeStruct(xs.shape, xs.dtype),
        scratch_shapes=[pltpu.SemaphoreType.DMA(())],
        input_output_aliases={2: 0},
        compiler_params=_params(("arbitrary",)),
        name="moe_scatter",
    )(pos3, h, xs)


def _experts_kernel(be_ref, nu_ref, xs_ref, wg_ref, wu_ref, wd_ref, ys_ref):
    del be_ref
    i = pl.program_id(0)

    @pl.when(i < nu_ref[0])
    def _():
        x = _unpack_bf16_pairs(xs_ref[...])
        g = _dot(x, wg_ref[...])
        u = _dot(x, wu_ref[...])
        a = (g * jax.nn.sigmoid(g) * u).astype(BF16)
        ys_ref[...] = _dot(a, wd_ref[...])

    @pl.when(i >= nu_ref[0])
    def _():
        ys_ref[...] = jnp.zeros_like(ys_ref)


def _experts(block_e, n_used, xs, wg, wu, wd, layer, n_blocks):
    grid_spec = pltpu.PrefetchScalarGridSpec(
        num_scalar_prefetch=2,
        grid=(n_blocks,),
        in_specs=[
            pl.BlockSpec((MOE_BM, D_PACK), lambda i, be, nu: (jnp.minimum(i, nu[0] - 1), 0)),
            pl.BlockSpec((None, None, D_MODEL, D_EXPERT), lambda i, be, nu: (layer, be[i], 0, 0)),
            pl.BlockSpec((None, None, D_MODEL, D_EXPERT), lambda i, be, nu: (layer, be[i], 0, 0)),
            pl.BlockSpec((None, None, D_EXPERT, D_MODEL), lambda i, be, nu: (layer, be[i], 0, 0)),
        ],
        out_specs=pl.BlockSpec((MOE_BM, D_MODEL), lambda i, be, nu: (i, 0)),
    )
    return pl.pallas_call(
        _experts_kernel,
        grid_spec=grid_spec,
        out_shape=jax.ShapeDtypeStruct((xs.shape[0], D_MODEL), F32),
        compiler_params=_params(("arbitrary",)),
        name="moe_experts",
    )(block_e, n_used, xs, wg, wu, wd)


def _combine_kernel(pos_ref, x1_ref, gate_ref, mod_ref, gfin_ref, ys_ref, o_ref, ybuf, sem,
                    *, final_norm):
    def issue(r, carry):
        for k in range(TOP_K):
            slot = pos_ref[0, 0, TOP_K * r + k]
            pltpu.make_async_copy(ys_ref.at[pl.ds(slot, 1)], ybuf.at[k, pl.ds(r, 1)], sem).start()
        return carry

    lax.fori_loop(0, TM_ROW, issue, 0, unroll=8)
    for k in range(TOP_K):
        pltpu.make_async_copy(ybuf.at[k], ybuf.at[k], sem).wait()
    f = gate_ref[:, 0:1] * ybuf[0] + gate_ref[:, 1:2] * ybuf[1]
    x2 = x1_ref[...] + mod_ref[5:6, :] * f
    if final_norm:
        x2 = _rms(x2, gfin_ref[...])
    o_ref[...] = x2


def _combine(pos3, x1, gate, mod, mod_row_fn, g_final, ys, layer, final_norm):
    ntok = x1.shape[0]
    return pl.pallas_call(
        functools.partial(_combine_kernel, final_norm=final_norm),
        grid=(ntok // TM_ROW,),
        in_specs=[
            pl.BlockSpec((1, 1, TOP_K * TM_ROW), lambda i: (i, 0, 0), memory_space=pltpu.SMEM),
            pl.BlockSpec((TM_ROW, D_MODEL), lambda i: (i, 0)),
            pl.BlockSpec((TM_ROW, LANES), lambda i: (i, 0)),
            pl.BlockSpec((None, None, N_MOD, D_MODEL), lambda i: (layer, mod_row_fn(i), 0, 0)),
            pl.BlockSpec((1, D_MODEL), lambda i: (0, 0)),
            pl.BlockSpec(memory_space=pl.ANY),
        ],
        out_specs=pl.BlockSpec((TM_ROW, D_MODEL), lambda i: (i, 0)),
        out_shape=jax.ShapeDtypeStruct((ntok, D_MODEL), F32),
        scratch_shapes=[pltpu.VMEM((TOP_K, TM_ROW, D_MODEL), F32), pltpu.SemaphoreType.DMA(())],
        compiler_params=_params(("arbitrary",)),
        name="moe_combine",
    )(pos3, x1, gate, mod, g_final, ys)


def kernel(x_prompt, x_sample, cache_k, cache_v, c, c_ctx, w_mod, b_mod, g_mix, g_ffn, w_in, rpb,
           g_sgu, w_sgu, b_sgu, w_pool, s_pool, g_branch, w_out, w_rg, b_rg, w_re, b_re,
           w_e_gate, w_e_up, w_e_down, g_final):
    n_ctx, seq, _ = x_prompt.shape
    n_lat, lat_seq, _ = x_sample.shape
    ntok_p = n_ctx * seq
    ntok_s = n_lat * lat_seq
    rows = lat_seq // GRID_W

    n_mod_rows = -(-(1 + n_lat) // SUBLANES) * SUBLANES
    c_all = jnp.concatenate(
        [c_ctx[None], c, jnp.zeros((n_mod_rows - 1 - n_lat, D_MODEL), F32)], axis=0)
    mod = _modulation(c_all, w_mod, b_mod).reshape(DEPTH, n_mod_rows, N_MOD, D_MODEL)

    xp = x_prompt.reshape(ntok_p, D_MODEL)
    xs_lat = x_sample.reshape(ntok_s, D_MODEL)

    def lat_mod_row(tm):
        return lambda i: 1 + (i * tm) // lat_seq

    ctx_mod_row = lambda i: 0

    w_in_b = w_in.astype(BF16).reshape(DEPTH, D_MODEL, IN_WIDTH // TN_IN, TN_IN).transpose(0, 2, 1, 3)
    w_out_b = w_out.astype(BF16)
    w_sgu_b = w_sgu.astype(BF16)
    w_pool_b = w_pool.astype(BF16)
    wg_b = w_e_gate.astype(BF16)
    wu_b = w_e_up.astype(BF16)
    wd_b = w_e_down.astype(BF16)

    gmix = g_mix[:, None, :]
    gffn = g_ffn[:, None, :]
    gb_a = g_branch[:, None, :D_A]
    gb_bc = g_branch[:, None, D_A:]
    gsgu = g_sgu[:, None, :]
    bsgu_t = jnp.swapaxes(b_sgu, 1, 2)
    spool = s_pool[:, None, :]
    n_route = N_GROUPS_E + N_EXPERTS
    w_r = jnp.concatenate(
        [w_rg, w_re, jnp.zeros((DEPTH, D_MODEL, LANES - n_route), F32)], axis=2)
    w_r_hi = lax.bitcast_convert_type(
        lax.bitcast_convert_type(w_r, U32) & jnp.uint32(0xFFFF0000), F32)
    w_r = jnp.concatenate([w_r_hi.astype(BF16), (w_r - w_r_hi).astype(BF16)], axis=2)
    b_r = jnp.concatenate(
        [b_rg, b_re, jnp.zeros((DEPTH, LANES - n_route), F32)], axis=1)[:, None, :]
    bias_tab = _local_bias_table(rpb.reshape((DEPTH * N_HEADS,) + rpb.shape[2:]), rows)

    new_k, new_v = [], []
    for l in range(DEPTH):
        qkv_p, ugp_p, kc, vc = _in_proj(
            xp, ntok_p, mod, ctx_mod_row, gmix, w_in_b, l, seq, True)
        new_k.append(kc)
        new_v.append(vc)
        oa_p = _ctx_attention(qkv_p, gb_a, l, seq)
        obc_p = _mixers(ugp_p, gsgu, w_sgu_b, bsgu_t, w_pool_b, spool, gb_bc, l, seq)
        x1p, hp, ep, gatep = _out_proj(
            xp, ntok_p, oa_p, obc_p, mod, ctx_mod_row, gffn, w_out_b, w_r, b_r, l)

        q_s, kv_s, ugp_s = _in_proj(
            xs_lat, ntok_s, mod, lat_mod_row(TM_IN), gmix, w_in_b, l, lat_seq, False)
        oa_s = _nbr_attention(q_s, kv_s, cache_k, cache_v, l, bias_tab, gb_a, n_lat, lat_seq)
        obc_s = _mixers(ugp_s, gsgu, w_sgu_b, bsgu_t, w_pool_b, spool, gb_bc, l, lat_seq)
        x1s, hs, es, gates = _out_proj(
            xs_lat, ntok_s, oa_s, obc_s, mod, lat_mod_row(TM_OUT), gffn, w_out_b, w_r, b_r, l)

        e_flat = jnp.concatenate([ep[:, :TOP_K].reshape(-1), es[:, :TOP_K].reshape(-1)])
        pos, block_e, n_used, n_blocks = _route_plan(e_flat)
        pos_p = pos[:TOP_K * ntok_p].reshape(ntok_p // TM_ROW, 1, TOP_K * TM_ROW)
        pos_s = pos[TOP_K * ntok_p:].reshape(ntok_s // TM_ROW, 1, TOP_K * TM_ROW)
        slots = jnp.zeros((n_blocks * MOE_BM, D_PACK), U32)
        slots = _scatter_rows(pos_p, hp, slots)
        slots = _scatter_rows(pos_s, hs, slots)
        ys = _experts(block_e, n_used, slots, wg_b, wu_b, wd_b, l, n_blocks)
        last = l == DEPTH - 1
        gfin = g_final[None]
        xp = _combine(pos_p, x1p, gatep, mod, ctx_mod_row, gfin, ys, l, last)
        xs_lat = _combine(pos_s, x1s, gates, mod, lat_mod_row(TM_ROW), gfin, ys, l, last)

    y_prompt = xp.reshape(n_ctx, seq, D_MODEL)
    y_sample = xs_lat.reshape(n_lat, lat_seq, D_MODEL)
    return (y_prompt, y_sample, jnp.stack(new_k, axis=1), jnp.stack(new_v, axis=1))
```

```python
import functools

import jax
import numpy as np
import jax.numpy as jnp
from jax import lax
from jax.experimental import pallas as pl
from jax.experimental.pallas import tpu as pltpu

F32 = jnp.float32
BF16 = jnp.bfloat16
I32 = jnp.int32
U32 = jnp.uint32

D_MODEL = 2048
DEPTH = 2
GRID_W = 64
HEAD_DIM = 128
D_A = 1024
N_HEADS = 8
WIN_H = 8
WIN_W = 16
D_B = 512
N_GROUPS_B = 4
CHUNK = 128
D_C = 512
POOL_WINDOWS = (2, 4, 8, 16)
IN_WIDTH = 3 * D_A + 2 * D_B + D_C
N_GROUPS_E = 4
EXPERTS_PER_GROUP = 8
N_EXPERTS = 32
TOP_K = 2
D_EXPERT = 512
N_MOD = 6
EPS = 1e-6
NEG = -1e30
LOG2E = 1.4426950408889634
SCALE_LOG2E = HEAD_DIM ** -0.5 * LOG2E

LANES = 128
SUBLANES = 8
VMEM_LIMIT_BYTES = 56 * 1024 * 1024

TM_IN = 1024
TN_IN = 512
TM_OUT = 512
TM_MIX = 256
Q_ROWS = 8
K_ROWS = 16
MOE_BM = 512
D_PACK = D_MODEL // 2
TM_ROW = 256
POOL_HALO = 8

COL_BLOCKS = ((0, 24, 0), (24, 16, 16), (40, 24, 32))
KEY_COLS = 32


def _params(sem):
    return pltpu.CompilerParams(dimension_semantics=sem, vmem_limit_bytes=VMEM_LIMIT_BYTES)


def _dot(a, b):
    return jnp.dot(a, b, preferred_element_type=F32)


def _dot_nt(a, b):
    return lax.dot_general(a, b, (((1,), (1,)), ((), ())), preferred_element_type=F32)


def _rms(x, g):
    return x * lax.rsqrt(jnp.mean(x * x, axis=-1, keepdims=True) + EPS) * g


def _pack_bf16_pairs(x):
    n = x.shape[1] // 2
    lo = lax.bitcast_convert_type(x[:, :n].astype(F32), U32) >> 16
    hi = lax.bitcast_convert_type(x[:, n:].astype(F32), U32) & jnp.uint32(0xFFFF0000)
    return hi | lo


def _unpack_bf16_pairs(w):
    lo = lax.bitcast_convert_type(w << 16, F32).astype(BF16)
    hi = lax.bitcast_convert_type(w & jnp.uint32(0xFFFF0000), F32).astype(BF16)
    return jnp.concatenate([lo, hi], axis=1)


def _mod_kernel(c_ref, w_ref, b_ref, o_ref):
    c = c_ref[...]
    s = c * jax.nn.sigmoid(c)
    o_ref[...] = jnp.dot(s, w_ref[...], preferred_element_type=F32,
                         precision=lax.Precision.HIGHEST) + b_ref[...]


def _modulation(c_all, w_mod, b_mod):
    tn = 1024
    nrow = c_all.shape[0]
    width = w_mod.shape[2]
    return pl.pallas_call(
        _mod_kernel,
        grid=(DEPTH, width // tn),
        in_specs=[
            pl.BlockSpec((nrow, D_MODEL), lambda l, j: (0, 0)),
            pl.BlockSpec((None, D_MODEL, tn), lambda l, j: (l, 0, j)),
            pl.BlockSpec((None, 1, tn), lambda l, j: (l, 0, j)),
        ],
        out_specs=pl.BlockSpec((None, nrow, tn), lambda l, j: (l, 0, j)),
        out_shape=jax.ShapeDtypeStruct((DEPTH, nrow, width), F32),
        compiler_params=_params(("arbitrary", "arbitrary")),
        name="modulation",
    )(c_all, w_mod, b_mod.reshape(DEPTH, 1, width))


def _in_proj_kernel(x_ref, mod_ref, g_ref, w_ref, *refs, ctx, n_aliased):
    refs = refs[n_aliased:]
    if ctx:
        qkv_ref, ugp_ref, kc_ref, vc_ref, h_scr = refs
    else:
        q_ref, kv_ref, ugp_ref, h_scr = refs
    j = pl.program_id(1)

    @pl.when(j == 0)
    def _():
        x = x_ref[...]
        h = _rms(x, g_ref[...]) * (1.0 + mod_ref[1:2, :]) + mod_ref[0:1, :]
        h_scr[...] = h.astype(BF16)

    acc = _dot(h_scr[...], w_ref[...])
    heads = TN_IN // HEAD_DIM

    def store_heads(ref, dtype):
        for hh in range(heads):
            ref[hh] = acc[:, hh * HEAD_DIM:(hh + 1) * HEAD_DIM].astype(dtype)

    def store_cache(ref):
        seq = ref.shape[2]
        for s in range(TM_IN // seq):
            for hh in range(heads):
                ref[s, hh] = acc[s * seq:(s + 1) * seq, hh * HEAD_DIM:(hh + 1) * HEAD_DIM]

    if ctx:
        @pl.when(j < 6)
        def _():
            store_heads(qkv_ref, BF16)

        @pl.when((j >= 2) & (j < 4))
        def _():
            store_cache(kc_ref)

        @pl.when((j >= 4) & (j < 6))
        def _():
            store_cache(vc_ref)
    else:
        @pl.when(j < 2)
        def _():
            store_heads(q_ref, F32)

        @pl.when((j >= 2) & (j < 6))
        def _():
            store_heads(kv_ref, BF16)

    @pl.when(j >= 6)
    def _():
        ugp_ref[...] = acc


def _in_proj(x2d, ntok, mod, mod_row_fn, g_mix, w_in, layer, seq_len, ctx, caches=()):
    nblk = ntok // TM_IN
    ncol = IN_WIDTH // TN_IN
    heads = TN_IN // HEAD_DIM
    hm = lambda lo, n: (lambda i, j: (jnp.clip(j - lo, 0, n - 1), i, 0))
    hspec = lambda lo, n: pl.BlockSpec((heads, TM_IN, HEAD_DIM), hm(lo, n))
    ugp_shape = jax.ShapeDtypeStruct((ntok, 2 * D_B + D_C), F32)
    ugp_spec = pl.BlockSpec((TM_IN, TN_IN), lambda i, j: (i, jnp.clip(j - 6, 0, 2)))
    if ctx:
        nseq = ntok // seq_len
        spb = TM_IN // seq_len
        cache_shape = jax.ShapeDtypeStruct((nseq, DEPTH, N_HEADS, seq_len, HEAD_DIM), F32)
        cache_spec = lambda lo: pl.BlockSpec(
            (spb, None, heads, seq_len, HEAD_DIM),
            lambda i, j: (i, layer, jnp.clip(j - lo, 0, 1), 0, 0))
        out_shape = [jax.ShapeDtypeStruct((3 * N_HEADS, ntok, HEAD_DIM), BF16), ugp_shape,
                     cache_shape, cache_shape]
        out_specs = [hspec(0, 6), ugp_spec, cache_spec(2), cache_spec(4)]
    else:
        out_shape = [jax.ShapeDtypeStruct((N_HEADS, ntok, HEAD_DIM), F32),
                     jax.ShapeDtypeStruct((2 * N_HEADS, ntok, HEAD_DIM), BF16), ugp_shape]
        out_specs = [hspec(0, 2), hspec(2, 4), ugp_spec]
    return pl.pallas_call(
        functools.partial(_in_proj_kernel, ctx=ctx, n_aliased=len(caches)),
        grid=(nblk, ncol),
        in_specs=[
            pl.BlockSpec((TM_IN, D_MODEL), lambda i, j: (i, 0)),
            pl.BlockSpec((None, None, N_MOD, D_MODEL), lambda i, j: (layer, mod_row_fn(i), 0, 0)),
            pl.BlockSpec((None, 1, D_MODEL), lambda i, j: (layer, 0, 0)),
            pl.BlockSpec((None, None, D_MODEL, TN_IN), lambda i, j: (layer, j, 0, 0)),
        ] + [pl.BlockSpec(memory_space=pl.ANY)] * len(caches),
        out_specs=out_specs,
        out_shape=out_shape,
        scratch_shapes=[pltpu.VMEM((TM_IN, D_MODEL), BF16)],
        input_output_aliases={4 + n: 2 + n for n in range(len(caches))},
        compiler_params=_params(("arbitrary", "arbitrary")),
        name="in_proj_ctx" if ctx else "in_proj_lat",
    )(x2d, mod, g_mix, w_in, *caches)


def _write_normed_heads(o_scr, gb_ref, o_ref):
    ss = None
    for h in range(N_HEADS):
        oh = o_scr[h]
        t = jnp.sum(oh * oh, axis=-1, keepdims=True)
        ss = t if ss is None else ss + t
    r = lax.rsqrt(ss * (1.0 / D_A) + EPS)
    for h in range(N_HEADS):
        sl = slice(h * HEAD_DIM, (h + 1) * HEAD_DIM)
        o_ref[:, sl] = (o_scr[h] * r * gb_ref[:, sl]).astype(o_ref.dtype)


def _ctx_attn_kernel(q_ref, k_ref, v_ref, gb_ref, o_ref, o_scr):
    def head(h, carry):
        s = _dot_nt(q_ref[h], k_ref[h]) * SCALE_LOG2E
        m = jnp.max(s, axis=-1, keepdims=True)
        p = jnp.exp2(s - m)
        l = jnp.sum(p, axis=-1, keepdims=True)
        o_scr[h] = _dot(p.astype(BF16), v_ref[h]) / l
        return carry

    lax.fori_loop(0, N_HEADS, head, 0)
    _write_normed_heads(o_scr, gb_ref, o_ref)


def _ctx_attention(qkv, gb_a, layer, seq_len):
    ntok = qkv.shape[1]
    spec = lambda part: pl.BlockSpec((N_HEADS, seq_len, HEAD_DIM), lambda b: (part, b, 0))
    return pl.pallas_call(
        _ctx_attn_kernel,
        grid=(ntok // seq_len,),
        in_specs=[spec(0), spec(1), spec(2),
                  pl.BlockSpec((None, 1, D_A), lambda b: (layer, 0, 0))],
        out_specs=pl.BlockSpec((seq_len, D_A), lambda b: (b, 0)),
        out_shape=jax.ShapeDtypeStruct((ntok, D_A), BF16),
        scratch_shapes=[pltpu.VMEM((N_HEADS, seq_len, HEAD_DIM), F32)],
        compiler_params=_params(("arbitrary",)),
        name="ctx_attention",
    )(qkv, qkv, qkv, gb_a)


def _local_bias_table(rpb, rows):
    n_rb = rows // Q_ROWS
    exact = lax.Precision.HIGHEST
    n_dr = 2 * WIN_H - 1
    n_dc = 2 * WIN_W - 1
    sel_r, valid_r = [], []
    for jb in (0, 1, n_rb - 1):
        r0 = Q_ROWS * jb
        ks = min(max(r0 - WIN_H // 2, 0), rows - K_ROWS)
        r = r0 + np.arange(Q_ROWS)[:, None]
        kr = ks + np.arange(K_ROWS)[None, :]
        rs = np.clip(r - WIN_H // 2, 0, rows - WIN_H)
        valid_r.append((kr >= rs) & (kr < rs + WIN_H))
        ri = np.clip(kr - r + WIN_H - 1, 0, n_dr - 1)
        sel_r.append(ri[..., None] == np.arange(n_dr))
    sel_r = np.stack(sel_r).astype(np.float32)
    valid_r = np.stack(valid_r)
    kc0 = np.concatenate([np.full(nc, k0) for (_, nc, k0) in COL_BLOCKS])
    c = np.arange(GRID_W)[:, None]
    kc = kc0[:, None] + np.arange(KEY_COLS)[None, :]
    cs = np.clip(c - WIN_W // 2, 0, GRID_W - WIN_W)
    valid_c = (kc >= cs) & (kc < cs + WIN_W)
    ci = np.clip(kc - c + WIN_W - 1, 0, n_dc - 1)
    sel_c = (ci[..., None] == np.arange(n_dc)).astype(np.float32)
    by_row = jnp.einsum("trka,hab->htrkb", sel_r, rpb, precision=exact)
    b = jnp.einsum("htrkb,cjb->htrckj", by_row, sel_c, precision=exact)
    valid = valid_r[:, :, None, :, None] & valid_c[None, None, :, None, :]
    b = jnp.where(valid[None], b * LOG2E, NEG)
    return b.reshape(rpb.shape[0], 3, Q_ROWS * GRID_W, K_ROWS * KEY_COLS)


def _nbr_attn_kernel(q_ref, k0, k1, k2, k3, v0, v1, v2, v3, ck_ref, cv_ref, bias_ref, gb_ref,
                     o_ref, o_scr):
    k_refs = (k0, k1, k2, k3)
    v_refs = (v0, v1, v2, v3)
    rows_per_ref = K_ROWS // len(k_refs)

    def window(refs, h, kc0):
        parts = []
        for kr in range(K_ROWS):
            ref = refs[kr // rows_per_ref]
            parts.append(ref[h, pl.ds((kr % rows_per_ref) * GRID_W + kc0, KEY_COLS), :])
        return jnp.concatenate(parts, axis=0)

    def query_rows(ref, h, c0, nc):
        return jnp.concatenate(
            [ref[h, pl.ds(qr * GRID_W + c0, nc), :] for qr in range(Q_ROWS)], axis=0)

    def head(h, carry):
        kc = ck_ref[h].astype(BF16)
        vc = cv_ref[h].astype(BF16)
        for (c0, nc, kc0) in COL_BLOCKS:
            qi = query_rows(q_ref, h, c0, nc).astype(BF16)
            ki = window(k_refs, h, kc0)
            vi = window(v_refs, h, kc0)
            s_loc = _dot_nt(qi, ki) * SCALE_LOG2E + query_rows(bias_ref, h, c0, nc)
            s_ctx = _dot_nt(qi, kc) * SCALE_LOG2E
            m = jnp.maximum(jnp.max(s_loc, axis=-1, keepdims=True),
                            jnp.max(s_ctx, axis=-1, keepdims=True))
            p_loc = jnp.exp2(s_loc - m)
            p_ctx = jnp.exp2(s_ctx - m)
            l = jnp.sum(p_loc, axis=-1, keepdims=True) + jnp.sum(p_ctx, axis=-1, keepdims=True)
            o = (_dot(p_loc.astype(BF16), vi) + _dot(p_ctx.astype(BF16), vc)) / l
            for qr in range(Q_ROWS):
                o_scr[h, pl.ds(qr * GRID_W + c0, nc), :] = o[qr * nc:(qr + 1) * nc]
        return carry

    lax.fori_loop(0, N_HEADS, head, 0)
    _write_normed_heads(o_scr, gb_ref, o_ref)


def _nbr_attention(q, kv, cache_k, cache_v, layer, bias_tab, gb_a, n_req, seq_len):
    rows = seq_len // GRID_W
    n_rb = rows // Q_ROWS
    tq = Q_ROWS * GRID_W
    n_kref = 4
    tk = (K_ROWS // n_kref) * GRID_W
    kblk_per_req = seq_len // tk
    ctx_len = cache_k.shape[3]

    def kmap(part, m):
        def f(b, jb):
            start = jnp.clip(2 * jb - 1, 0, kblk_per_req - n_kref)
            return (part, b * kblk_per_req + start + m, 0)
        return f

    kspecs = [pl.BlockSpec((N_HEADS, tk, HEAD_DIM), kmap(0, m)) for m in range(n_kref)]
    vspecs = [pl.BlockSpec((N_HEADS, tk, HEAD_DIM), kmap(1, m)) for m in range(n_kref)]
    cspec = pl.BlockSpec((None, None, N_HEADS, ctx_len, HEAD_DIM), lambda b, jb: (b, layer, 0, 0, 0))
    kind = lambda b, jb: (layer, (jb > 0).astype(I32) + (jb == n_rb - 1).astype(I32), 0, 0)
    return pl.pallas_call(
        _nbr_attn_kernel,
        grid=(n_req, n_rb),
        in_specs=[pl.BlockSpec((N_HEADS, tq, HEAD_DIM), lambda b, jb: (0, b * n_rb + jb, 0))]
        + kspecs + vspecs + [
            cspec, cspec,
            pl.BlockSpec((N_HEADS, None, tq, K_ROWS * KEY_COLS), kind),
            pl.BlockSpec((None, 1, D_A), lambda b, jb: (layer, 0, 0)),
        ],
        out_specs=pl.BlockSpec((tq, D_A), lambda b, jb: (b * n_rb + jb, 0)),
        out_shape=jax.ShapeDtypeStruct((n_req * seq_len, D_A), BF16),
        scratch_shapes=[pltpu.VMEM((N_HEADS, tq, HEAD_DIM), F32)],
        compiler_params=_params(("arbitrary", "arbitrary")),
        name="nbr_attention",
    )(q, *([kv] * (2 * n_kref)), cache_k, cache_v, bias_tab, gb_a)


def _mixers_kernel(u_ref, gv_ref, p_ref, pprev_ref, pnext_ref, gsgu_ref, wsgu_ref, bsgu_ref,
                   wpool_ref, spool_ref, gb_ref, o_ref, ext_scr, *, seq_len):
    i = pl.program_id(0)
    blocks_per_seq = seq_len // TM_MIX
    bi = i % blocks_per_seq
    t0 = bi * TM_MIX

    gu = jax.nn.gelu(u_ref[...])
    gg = _rms(jax.nn.gelu(gv_ref[...]), gsgu_ref[...]).astype(BF16)
    cols = []
    for g in range(N_GROUPS_B):
        sl = slice(g * CHUNK, (g + 1) * CHUNK)
        chunks = []
        for n in range(TM_MIX // CHUNK):
            rows = slice(n * CHUNK, (n + 1) * CHUNK)
            chunks.append(_dot(wsgu_ref[g], gg[rows, sl]) + bsgu_ref[:, g:g + 1])
        cols.append(jnp.concatenate(chunks, axis=0))
    o_b = gu * jnp.concatenate(cols, axis=1)
    o_ref[:, 0:D_B] = _rms(o_b, gb_ref[:, 0:D_B]).astype(o_ref.dtype)

    p = p_ref[...]
    ext_scr[pl.ds(0, POOL_HALO), :] = jnp.where(bi > 0, pprev_ref[...], 0.0)
    ext_scr[pl.ds(POOL_HALO, TM_MIX), :] = p
    ext_scr[pl.ds(POOL_HALO + TM_MIX, POOL_HALO), :] = jnp.where(
        bi < blocks_per_seq - 1, pnext_ref[...], 0.0)
    t = t0 + lax.broadcasted_iota(I32, (TM_MIX, 1), 0)
    outs = []
    for g, w in enumerate(POOL_WINDOWS):
        half = w // 2
        sl = slice(g * CHUNK, (g + 1) * CHUNK)
        acc = None
        for d in range(-half, half):
            part = ext_scr[pl.ds(POOL_HALO + d, TM_MIX), sl]
            acc = part if acc is None else acc + part
        cnt = (jnp.minimum(t + half, seq_len) - jnp.maximum(t - half, 0)).astype(F32)
        pooled = acc / cnt - p[:, sl]
        outs.append(_dot(pooled.astype(BF16), wpool_ref[g]))
    o_c = jnp.concatenate(outs, axis=1) * spool_ref[...]
    o_ref[:, D_B:D_B + D_C] = _rms(o_c, gb_ref[:, D_B:D_B + D_C]).astype(o_ref.dtype)


def _mixers(ugp, g_sgu, w_sgu, b_sgu_t, w_pool, s_pool, gb_bc, layer, seq_len):
    ntok = ugp.shape[0]
    nblk = ntok // TM_MIX
    hb = TM_MIX // POOL_HALO
    n_halo = ntok // POOL_HALO
    blk = lambda part: pl.BlockSpec((TM_MIX, D_B), lambda i: (i, part))
    full = lambda shape: pl.BlockSpec((None,) + shape, lambda i: (layer,) + (0,) * len(shape))
    return pl.pallas_call(
        functools.partial(_mixers_kernel, seq_len=seq_len),
        grid=(nblk,),
        in_specs=[
            blk(0), blk(1), blk(2),
            pl.BlockSpec((POOL_HALO, D_C), lambda i: (jnp.maximum(i * hb - 1, 0), 2)),
            pl.BlockSpec((POOL_HALO, D_C), lambda i: (jnp.minimum((i + 1) * hb, n_halo - 1), 2)),
            full((1, D_B)), full((N_GROUPS_B, CHUNK, CHUNK)), full((CHUNK, N_GROUPS_B)),
            full((len(POOL_WINDOWS), CHUNK, CHUNK)), full((1, D_C)), full((1, D_B + D_C)),
        ],
        out_specs=pl.BlockSpec((TM_MIX, D_B + D_C), lambda i: (i, 0)),
        out_shape=jax.ShapeDtypeStruct((ntok, D_B + D_C), BF16),
        scratch_shapes=[pltpu.VMEM((TM_MIX + 2 * POOL_HALO, D_C), F32)],
        compiler_params=_params(("arbitrary",)),
        name="mixers",
    )(ugp, ugp, ugp, ugp, ugp, g_sgu, w_sgu, b_sgu_t, w_pool, s_pool, gb_bc)


def _out_proj_kernel(x_ref, oa_ref, obc_ref, mod_ref, g_ref, w_ref, wr_ref, br_ref,
                     x1_ref, h_ref, e_ref, gate_ref):
    merged = jnp.concatenate([oa_ref[...], obc_ref[...]], axis=1)
    mix = _dot(merged, w_ref[...])
    x1 = x_ref[...] + mod_ref[2:3, :] * mix
    x1_ref[...] = x1
    h = _rms(x1, g_ref[...]) * (1.0 + mod_ref[4:5, :]) + mod_ref[3:4, :]
    h_hi = h.astype(BF16)
    h_ref[...] = _pack_bf16_pairs(h_hi)

    h_lo = (h - h_hi.astype(F32)).astype(BF16)
    part = _dot(h_hi, wr_ref[...])
    logits = (part[:, :LANES] + part[:, LANES:] + _dot(h_lo, wr_ref[:, :LANES])) + br_ref[...]
    lane = lax.broadcasted_iota(I32, logits.shape, 1)
    big = jnp.int32(LANES)

    def softmax_masked(mask):
        z = jnp.where(mask, logits, NEG)
        m = jnp.max(z, axis=-1, keepdims=True)
        e = jnp.where(mask, jnp.exp(z - m), 0.0)
        return e / jnp.sum(e, axis=-1, keepdims=True)

    def top1(vals, mask):
        v = jnp.where(mask, vals, -1.0)
        best = jnp.max(v, axis=-1, keepdims=True)
        idx = jnp.min(jnp.where(mask & (v == best), lane, big), axis=-1, keepdims=True)
        return best, idx

    gmask = lane < N_GROUPS_E
    p_grp = softmax_masked(gmask)
    p_g, g_sel = top1(p_grp, gmask)
    lo = N_GROUPS_E + EXPERTS_PER_GROUP * g_sel
    emask = (lane >= lo) & (lane < lo + EXPERTS_PER_GROUP)
    p_exp = softmax_masked(emask)
    p1, i1 = top1(p_exp, emask)
    p2, i2 = top1(p_exp, emask & (lane != i1))
    denom = p1 + p2
    g1 = p_g * p1 / denom
    g2 = p_g * p2 / denom
    e_ref[...] = jnp.where(lane == 0, i1 - N_GROUPS_E, jnp.where(lane == 1, i2 - N_GROUPS_E, 0))
    gate_ref[...] = jnp.where(lane == 0, g1, jnp.where(lane == 1, g2, 0.0))


def _out_proj(x2d, ntok, o_a, o_bc, mod, mod_row_fn, g_ffn, w_out, w_r, b_r, layer):
    nblk = ntok // TM_OUT
    full = lambda shape: pl.BlockSpec((None,) + shape, lambda i: (layer,) + (0,) * len(shape))
    return pl.pallas_call(
        _out_proj_kernel,
        grid=(nblk,),
        in_specs=[
            pl.BlockSpec((TM_OUT, D_MODEL), lambda i: (i, 0)),
            pl.BlockSpec((TM_OUT, D_A), lambda i: (i, 0)),
            pl.BlockSpec((TM_OUT, D_B + D_C), lambda i: (i, 0)),
            pl.BlockSpec((None, None, N_MOD, D_MODEL), lambda i: (layer, mod_row_fn(i), 0, 0)),
            full((1, D_MODEL)), full((D_MODEL, D_MODEL)), full((D_MODEL, 2 * LANES)), full((1, LANES)),
        ],
        out_specs=[
            pl.BlockSpec((TM_OUT, D_MODEL), lambda i: (i, 0)),
            pl.BlockSpec((TM_OUT, D_PACK), lambda i: (i, 0)),
            pl.BlockSpec((TM_OUT, LANES), lambda i: (i, 0)),
            pl.BlockSpec((TM_OUT, LANES), lambda i: (i, 0)),
        ],
        out_shape=[
            jax.ShapeDtypeStruct((ntok, D_MODEL), F32),
            jax.ShapeDtypeStruct((ntok, D_PACK), U32),
            jax.ShapeDtypeStruct((ntok, LANES), I32),
            jax.ShapeDtypeStruct((ntok, LANES), F32),
        ],
        compiler_params=_params(("arbitrary",)),
        name="out_proj_router",
    )(x2d, o_a, o_bc, mod, g_ffn, w_out, w_r, b_r)


def _route_plan(e_flat):
    n_assign = e_flat.shape[0]
    onehot = (e_flat[:, None] == jnp.arange(N_EXPERTS, dtype=I32)[None, :]).astype(I32)
    csum = jnp.cumsum(onehot, axis=0)
    counts = csum[-1]
    rank = jnp.sum(onehot * (csum - 1), axis=1)
    padded = (counts + MOE_BM - 1) // MOE_BM * MOE_BM
    pad_end = jnp.cumsum(padded)
    pad_start = pad_end - padded
    pos = jnp.sum(onehot * pad_start[None, :], axis=1) + rank
    n_blocks = n_assign // MOE_BM + N_EXPERTS
    starts = jnp.arange(n_blocks, dtype=I32) * MOE_BM
    block_e = jnp.minimum(jnp.sum((starts[:, None] >= pad_end[None, :]).astype(I32), axis=1),
                          N_EXPERTS - 1).astype(I32)
    n_used = (pad_end[-1] // MOE_BM).astype(I32).reshape(1)
    return pos.astype(I32), block_e, n_used, n_blocks


def _scatter_kernel(pos_ref, h_ref, xs_in_ref, xs_ref, sem):
    del xs_in_ref

    def issue(r, carry):
        for k in range(TOP_K):
            slot = pos_ref[0, 0, TOP_K * r + k]
            pltpu.make_async_copy(
                h_ref.at[pl.ds(r, 1)], xs_ref.at[pl.ds(slot, 1)], sem).start(priority=k)
        return carry

    lax.fori_loop(0, TM_ROW, issue, 0, unroll=8)
    for _ in range(TOP_K):
        pltpu.make_async_copy(h_ref, h_ref, sem).wait()


def _scatter_rows(pos3, h, xs):
    ntok = h.shape[0]
    return pl.pallas_call(
        _scatter_kernel,
        grid=(ntok // TM_ROW,),
        in_specs=[
            pl.BlockSpec((1, 1, TOP_K * TM_ROW), lambda i: (i, 0, 0), memory_space=pltpu.SMEM),
            pl.BlockSpec((TM_ROW, D_PACK), lambda i: (i, 0)),
            pl.BlockSpec(memory_space=pl.ANY),
        ],
        out_specs=pl.BlockSpec(memory_space=pl.ANY),
        out_shape=jax.ShapeDtypeStruct(xs.shape, xs.dtype),
        scratch_shapes=[pltpu.SemaphoreType.DMA(())],
        input_output_aliases={2: 0},
        compiler_params=_params(("arbitrary",)),
        name="moe_scatter",
    )(pos3, h, xs)


def _experts_kernel(be_ref, nu_ref, xs_ref, wg_ref, wu_ref, wd_ref, ys_ref):
    del be_ref
    i = pl.program_id(0)

    @pl.when(i < nu_ref[0])
    def _():
        x = _unpack_bf16_pairs(xs_ref[...])
        g = _dot(x, wg_ref[...].astype(BF16))
        u = _dot(x, wu_ref[...].astype(BF16))
        a = (g * jax.nn.sigmoid(g) * u).astype(BF16)
        ys_ref[...] = _dot(a, wd_ref[...].astype(BF16))

    @pl.when(i >= nu_ref[0])
    def _():
        ys_ref[...] = jnp.zeros_like(ys_ref)


def _experts(block_e, n_used, xs, wg, wu, wd, layer, n_blocks):
    grid_spec = pltpu.PrefetchScalarGridSpec(
        num_scalar_prefetch=2,
        grid=(n_blocks,),
        in_specs=[
            pl.BlockSpec((MOE_BM, D_PACK), lambda i, be, nu: (jnp.minimum(i, nu[0] - 1), 0)),
            pl.BlockSpec((None, None, D_MODEL, D_EXPERT), lambda i, be, nu: (layer, be[i], 0, 0)),
            pl.BlockSpec((None, None, D_MODEL, D_EXPERT), lambda i, be, nu: (layer, be[i], 0, 0)),
            pl.BlockSpec((None, None, D_EXPERT, D_MODEL), lambda i, be, nu: (layer, be[i], 0, 0)),
        ],
        out_specs=pl.BlockSpec((MOE_BM, D_MODEL), lambda i, be, nu: (i, 0)),
    )
    return pl.pallas_call(
        _experts_kernel,
        grid_spec=grid_spec,
        out_shape=jax.ShapeDtypeStruct((xs.shape[0], D_MODEL), F32),
        compiler_params=_params(("arbitrary",)),
        name="moe_experts",
    )(block_e, n_used, xs, wg, wu, wd)


def _combine_kernel(pos_ref, pos_next_ref, x1_ref, gate_ref, mod_ref, gfin_ref, ys_ref, o_ref,
                    ybuf, sem, *, final_norm):
    i = pl.program_id(0)
    buf = i % 2

    def issue(p_ref, b):
        def body(r, carry):
            for k in range(TOP_K):
                slot = p_ref[0, 0, TOP_K * r + k]
                pltpu.make_async_copy(
                    ys_ref.at[pl.ds(slot, 1)], ybuf.at[b, k, pl.ds(r, 1)], sem.at[b]
                ).start(priority=k)
            return carry
        lax.fori_loop(0, TM_ROW, body, 0, unroll=8)

    @pl.when(i == 0)
    def _():
        issue(pos_ref, 0)

    @pl.when(i + 1 < pl.num_programs(0))
    def _():
        issue(pos_next_ref, 1 - buf)

    for k in range(TOP_K):
        pltpu.make_async_copy(ybuf.at[buf, k], ybuf.at[buf, k], sem.at[buf]).wait()
    f = gate_ref[:, 0:1] * ybuf[buf, 0] + gate_ref[:, 1:2] * ybuf[buf, 1]
    x2 = x1_ref[...] + mod_ref[5:6, :] * f
    if final_norm:
        x2 = _rms(x2, gfin_ref[...])
    o_ref[...] = x2


def _combine(pos3, x1, gate, mod, mod_row_fn, g_final, ys, layer, final_norm):
    ntok = x1.shape[0]
    nblk = ntok // TM_ROW
    return pl.pallas_call(
        functools.partial(_combine_kernel, final_norm=final_norm),
        grid=(nblk,),
        in_specs=[
            pl.BlockSpec((1, 1, TOP_K * TM_ROW), lambda i: (i, 0, 0), memory_space=pltpu.SMEM),
            pl.BlockSpec((1, 1, TOP_K * TM_ROW), lambda i: (jnp.minimum(i + 1, nblk - 1), 0, 0),
                         memory_space=pltpu.SMEM),
            pl.BlockSpec((TM_ROW, D_MODEL), lambda i: (i, 0)),
            pl.BlockSpec((TM_ROW, LANES), lambda i: (i, 0)),
            pl.BlockSpec((None, None, N_MOD, D_MODEL), lambda i: (layer, mod_row_fn(i), 0, 0)),
            pl.BlockSpec((1, D_MODEL), lambda i: (0, 0)),
            pl.BlockSpec(memory_space=pl.ANY),
        ],
        out_specs=pl.BlockSpec((TM_ROW, D_MODEL), lambda i: (i, 0)),
        out_shape=jax.ShapeDtypeStruct((ntok, D_MODEL), F32),
        scratch_shapes=[pltpu.VMEM((2, TOP_K, TM_ROW, D_MODEL), F32),
                        pltpu.SemaphoreType.DMA((2,))],
        compiler_params=_params(("arbitrary",)),
        name="moe_combine",
    )(pos3, pos3, x1, gate, mod, g_final, ys)


def kernel(x_prompt, x_sample, cache_k, cache_v, c, c_ctx, w_mod, b_mod, g_mix, g_ffn, w_in, rpb,
           g_sgu, w_sgu, b_sgu, w_pool, s_pool, g_branch, w_out, w_rg, b_rg, w_re, b_re,
           w_e_gate, w_e_up, w_e_down, g_final):
    n_ctx, seq, _ = x_prompt.shape
    n_lat, lat_seq, _ = x_sample.shape
    ntok_p = n_ctx * seq
    ntok_s = n_lat * lat_seq
    rows = lat_seq // GRID_W

    n_mod_rows = -(-(1 + n_lat) // SUBLANES) * SUBLANES
    c_all = jnp.concatenate(
        [c_ctx[None], c, jnp.zeros((n_mod_rows - 1 - n_lat, D_MODEL), F32)], axis=0)
    mod = _modulation(c_all, w_mod, b_mod).reshape(DEPTH, n_mod_rows, N_MOD, D_MODEL)

    xp = x_prompt.reshape(ntok_p, D_MODEL)
    xs_lat = x_sample.reshape(ntok_s, D_MODEL)

    def lat_mod_row(tm):
        return lambda i: 1 + (i * tm) // lat_seq

    ctx_mod_row = lambda i: 0

    w_in_b = w_in.astype(BF16).reshape(DEPTH, D_MODEL, IN_WIDTH // TN_IN, TN_IN).transpose(0, 2, 1, 3)
    w_out_b = w_out.astype(BF16)
    w_sgu_b = w_sgu.astype(BF16)
    w_pool_b = w_pool.astype(BF16)

    gmix = g_mix[:, None, :]
    gffn = g_ffn[:, None, :]
    gb_a = g_branch[:, None, :D_A]
    gb_bc = g_branch[:, None, D_A:]
    gsgu = g_sgu[:, None, :]
    bsgu_t = jnp.swapaxes(b_sgu, 1, 2)
    spool = s_pool[:, None, :]
    n_route = N_GROUPS_E + N_EXPERTS
    w_r = jnp.concatenate(
        [w_rg, w_re, jnp.zeros((DEPTH, D_MODEL, LANES - n_route), F32)], axis=2)
    w_r_hi = lax.bitcast_convert_type(
        lax.bitcast_convert_type(w_r, U32) & jnp.uint32(0xFFFF0000), F32)
    w_r = jnp.concatenate([w_r_hi.astype(BF16), (w_r - w_r_hi).astype(BF16)], axis=2)
    b_r = jnp.concatenate(
        [b_rg, b_re, jnp.zeros((DEPTH, LANES - n_route), F32)], axis=1)[:, None, :]
    bias_tab = _local_bias_table(rpb.reshape((DEPTH * N_HEADS,) + rpb.shape[2:]), rows)

    cache_shape = (n_ctx, DEPTH, N_HEADS, seq, HEAD_DIM)
    caches = (jnp.zeros(cache_shape, F32), jnp.zeros(cache_shape, F32))
    for l in range(DEPTH):
        qkv_p, ugp_p, *caches = _in_proj(
            xp, ntok_p, mod, ctx_mod_row, gmix, w_in_b, l, seq, True, tuple(caches))
        oa_p = _ctx_attention(qkv_p, gb_a, l, seq)
        obc_p = _mixers(ugp_p, gsgu, w_sgu_b, bsgu_t, w_pool_b, spool, gb_bc, l, seq)
        x1p, hp, ep, gatep = _out_proj(
            xp, ntok_p, oa_p, obc_p, mod, ctx_mod_row, gffn, w_out_b, w_r, b_r, l)

        q_s, kv_s, ugp_s = _in_proj(
            xs_lat, ntok_s, mod, lat_mod_row(TM_IN), gmix, w_in_b, l, lat_seq, False)
        oa_s = _nbr_attention(q_s, kv_s, cache_k, cache_v, l, bias_tab, gb_a, n_lat, lat_seq)
        obc_s = _mixers(ugp_s, gsgu, w_sgu_b, bsgu_t, w_pool_b, spool, gb_bc, l, lat_seq)
        x1s, hs, es, gates = _out_proj(
            xs_lat, ntok_s, oa_s, obc_s, mod, lat_mod_row(TM_OUT), gffn, w_out_b, w_r, b_r, l)

        e_flat = jnp.concatenate([ep[:, :TOP_K].reshape(-1), es[:, :TOP_K].reshape(-1)])
        pos, block_e, n_used, n_blocks = _route_plan(e_flat)
        pos_p = pos[:TOP_K * ntok_p].reshape(ntok_p // TM_ROW, 1, TOP_K * TM_ROW)
        pos_s = pos[TOP_K * ntok_p:].reshape(ntok_s // TM_ROW, 1, TOP_K * TM_ROW)
        slots = jnp.zeros((n_blocks * MOE_BM, D_PACK), U32)
        slots = _scatter_rows(pos_p, hp, slots)
        slots = _scatter_rows(pos_s, hs, slots)
        ys = _experts(block_e, n_used, slots, w_e_gate, w_e_up, w_e_down, l, n_blocks)
        last = l == DEPTH - 1
        gfin = g_final[None]
        xp = _combine(pos_p, x1p, gatep, mod, ctx_mod_row, gfin, ys, l, last)
        xs_lat = _combine(pos_s, x1s, gates, mod, lat_mod_row(TM_ROW), gfin, ys, l, last)

    y_prompt = xp.reshape(n_ctx, seq, D_MODEL)
    y_sample = xs_lat.reshape(n_lat, lat_seq, D_MODEL)
    return (y_prompt, y_sample, caches[0], caches[1])
```

```python
import functools

import jax
import numpy as np
import jax.numpy as jnp
from jax import lax
from jax.experimental import pallas as pl
from jax.experimental.pallas import tpu as pltpu

F32 = jnp.float32
BF16 = jnp.bfloat16
I32 = jnp.int32
U32 = jnp.uint32

D_MODEL = 2048
DEPTH = 2
GRID_W = 64
HEAD_DIM = 128
D_A = 1024
N_HEADS = 8
WIN_H = 8
WIN_W = 16
D_B = 512
N_GROUPS_B = 4
CHUNK = 128
D_C = 512
POOL_WINDOWS = (2, 4, 8, 16)
IN_WIDTH = 3 * D_A + 2 * D_B + D_C
N_GROUPS_E = 4
EXPERTS_PER_GROUP = 8
N_EXPERTS = 32
TOP_K = 2
D_EXPERT = 512
N_MOD = 6
EPS = 1e-6
NEG = -1e30
LOG2E = 1.4426950408889634
SCALE_LOG2E = HEAD_DIM ** -0.5 * LOG2E

LANES = 128
SUBLANES = 8
VMEM_LIMIT_BYTES = 56 * 1024 * 1024

TM_IN = 1024
TN_IN = 512
IN_CHUNK = 256
OUT_CHUNK = 256
TM_OUT = 512
TM_MIX = 256
Q_ROWS = 8
K_ROWS = 16
MOE_BM = 512
D_PACK = D_MODEL // 2
TM_ROW = 256
POOL_HALO = 8

COL_BLOCKS = ((0, 24, 0), (24, 16, 16), (40, 24, 32))
KEY_COLS = 32


def _params(sem):
    return pltpu.CompilerParams(dimension_semantics=sem, vmem_limit_bytes=VMEM_LIMIT_BYTES)


def _dot(a, b):
    return jnp.dot(a, b, preferred_element_type=F32)


def _dot_nt(a, b):
    return lax.dot_general(a, b, (((1,), (1,)), ((), ())), preferred_element_type=F32)


def _rms(x, g):
    return x * lax.rsqrt(jnp.mean(x * x, axis=-1, keepdims=True) + EPS) * g


def _pack_bf16_pairs(x):
    n = x.shape[1] // 2
    lo = lax.bitcast_convert_type(x[:, :n].astype(F32), U32) >> 16
    hi = lax.bitcast_convert_type(x[:, n:].astype(F32), U32) & jnp.uint32(0xFFFF0000)
    return hi | lo


def _unpack_bf16_pairs(w, dtype):
    lo = lax.bitcast_convert_type(w << 16, F32).astype(dtype)
    hi = lax.bitcast_convert_type(w & jnp.uint32(0xFFFF0000), F32).astype(dtype)
    return jnp.concatenate([lo, hi], axis=1)


def _mod_kernel(c_ref, w_ref, b_ref, o_ref):
    c = c_ref[...]
    s = c * jax.nn.sigmoid(c)
    o_ref[...] = jnp.dot(s, w_ref[...], preferred_element_type=F32,
                         precision=lax.Precision.HIGHEST) + b_ref[...]


def _modulation(c_all, w_mod, b_mod):
    tn = 1024
    nrow = c_all.shape[0]
    width = w_mod.shape[2]
    return pl.pallas_call(
        _mod_kernel,
        grid=(DEPTH, width // tn),
        in_specs=[
            pl.BlockSpec((nrow, D_MODEL), lambda l, j: (0, 0)),
            pl.BlockSpec((None, D_MODEL, tn), lambda l, j: (l, 0, j)),
            pl.BlockSpec((None, 1, tn), lambda l, j: (l, 0, j)),
        ],
        out_specs=pl.BlockSpec((None, nrow, tn), lambda l, j: (l, 0, j)),
        out_shape=jax.ShapeDtypeStruct((DEPTH, nrow, width), F32),
        compiler_params=_params(("arbitrary", "arbitrary")),
        name="modulation",
    )(c_all, w_mod, b_mod.reshape(DEPTH, 1, width))


def _in_proj_kernel(x_ref, mod_ref, g_ref, w_ref, *refs, ctx, n_aliased):
    refs = refs[n_aliased:]
    if ctx:
        qkv_ref, ugp_ref, kc_ref, vc_ref, h_scr = refs
    else:
        q_ref, kv_ref, ugp_ref, h_scr = refs
    j = pl.program_id(1)
    heads = TN_IN // HEAD_DIM

    def head_store(ref, dtype):
        def store(m, rows, acc):
            for hh in range(heads):
                ref[hh, rows, :] = acc[:, hh * HEAD_DIM:(hh + 1) * HEAD_DIM].astype(dtype)
        return store

    def cache_store(ref):
        seq = ref.shape[2]
        per_chunk = IN_CHUNK // seq

        def store(m, rows, acc):
            for s in range(per_chunk):
                for hh in range(heads):
                    ref[m * per_chunk + s, hh] = acc[s * seq:(s + 1) * seq,
                                                     hh * HEAD_DIM:(hh + 1) * HEAD_DIM]
        return store

    def ugp_store(m, rows, acc):
        ugp_ref[rows, :] = acc

    def run(normalize, stores):
        for m in range(TM_IN // IN_CHUNK):
            rows = pl.ds(m * IN_CHUNK, IN_CHUNK)
            if normalize:
                h = _rms(x_ref[rows, :], g_ref[...]) * (1.0 + mod_ref[1:2, :]) + mod_ref[0:1, :]
                hb = h.astype(BF16)
                h_scr[rows, :] = hb
            else:
                hb = h_scr[rows, :]
            acc = _dot(hb, w_ref[...])
            for store in stores:
                store(m, rows, acc)

    def case(cond, normalize, stores):
        pl.when(cond)(lambda: run(normalize, stores))

    if ctx:
        case(j == 0, True, [head_store(qkv_ref, BF16)])
        case(j == 1, False, [head_store(qkv_ref, BF16)])
        case((j >= 2) & (j < 4), False, [head_store(qkv_ref, BF16), cache_store(kc_ref)])
        case((j >= 4) & (j < 6), False, [head_store(qkv_ref, BF16), cache_store(vc_ref)])
    else:
        case(j == 0, True, [head_store(q_ref, F32)])
        case(j == 1, False, [head_store(q_ref, F32)])
        case((j >= 2) & (j < 6), False, [head_store(kv_ref, BF16)])
    case(j >= 6, False, [ugp_store])


def _in_proj(x2d, ntok, mod, mod_row_fn, g_mix, w_in, layer, seq_len, ctx, caches=()):
    nblk = ntok // TM_IN
    ncol = IN_WIDTH // TN_IN
    heads = TN_IN // HEAD_DIM
    hm = lambda lo, n: (lambda i, j: (jnp.clip(j - lo, 0, n - 1), i, 0))
    hspec = lambda lo, n: pl.BlockSpec((heads, TM_IN, HEAD_DIM), hm(lo, n))
    ugp_shape = jax.ShapeDtypeStruct((ntok, 2 * D_B + D_C), F32)
    ugp_spec = pl.BlockSpec((TM_IN, TN_IN), lambda i, j: (i, jnp.clip(j - 6, 0, 2)))
    if ctx:
        nseq = ntok // seq_len
        spb = TM_IN // seq_len
        cache_shape = jax.ShapeDtypeStruct((nseq, DEPTH, N_HEADS, seq_len, HEAD_DIM), F32)
        cache_spec = lambda lo: pl.BlockSpec(
            (spb, None, heads, seq_len, HEAD_DIM),
            lambda i, j: (i, layer, jnp.clip(j - lo, 0, 1), 0, 0))
        out_shape = [jax.ShapeDtypeStruct((3 * N_HEADS, ntok, HEAD_DIM), BF16), ugp_shape,
                     cache_shape, cache_shape]
        out_specs = [hspec(0, 6), ugp_spec, cache_spec(2), cache_spec(4)]
    else:
        out_shape = [jax.ShapeDtypeStruct((N_HEADS, ntok, HEAD_DIM), F32),
                     jax.ShapeDtypeStruct((2 * N_HEADS, ntok, HEAD_DIM), BF16), ugp_shape]
        out_specs = [hspec(0, 2), hspec(2, 4), ugp_spec]
    return pl.pallas_call(
        functools.partial(_in_proj_kernel, ctx=ctx, n_aliased=len(caches)),
        grid=(nblk, ncol),
        in_specs=[
            pl.BlockSpec((TM_IN, D_MODEL), lambda i, j: (i, 0)),
            pl.BlockSpec((None, None, N_MOD, D_MODEL), lambda i, j: (layer, mod_row_fn(i), 0, 0)),
            pl.BlockSpec((None, 1, D_MODEL), lambda i, j: (layer, 0, 0)),
            pl.BlockSpec((None, None, D_MODEL, TN_IN), lambda i, j: (layer, j, 0, 0)),
        ] + [pl.BlockSpec(memory_space=pl.ANY)] * len(caches),
        out_specs=out_specs,
        out_shape=out_shape,
        scratch_shapes=[pltpu.VMEM((TM_IN, D_MODEL), BF16)],
        input_output_aliases={4 + n: 2 + n for n in range(len(caches))},
        compiler_params=_params(("arbitrary", "arbitrary")),
        name="in_proj_ctx" if ctx else "in_proj_lat",
    )(x2d, mod, g_mix, w_in, *caches)


def _write_normed_heads(o_scr, gb_ref, o_ref):
    ss = None
    for h in range(N_HEADS):
        oh = o_scr[h]
        t = jnp.sum(oh * oh, axis=-1, keepdims=True)
        ss = t if ss is None else ss + t
    r = lax.rsqrt(ss * (1.0 / D_A) + EPS)
    for h in range(N_HEADS):
        sl = slice(h * HEAD_DIM, (h + 1) * HEAD_DIM)
        o_ref[:, sl] = (o_scr[h] * r * gb_ref[:, sl]).astype(o_ref.dtype)


def _ctx_attn_kernel(q_ref, k_ref, v_ref, gb_ref, o_ref, o_scr):
    def head(h, carry):
        s = _dot_nt(q_ref[h], k_ref[h]) * SCALE_LOG2E
        m = jnp.max(s, axis=-1, keepdims=True)
        p = jnp.exp2(s - m)
        l = jnp.sum(p, axis=-1, keepdims=True)
        o_scr[h] = _dot(p.astype(BF16), v_ref[h]) / l
        return carry

    lax.fori_loop(0, N_HEADS, head, 0)
    _write_normed_heads(o_scr, gb_ref, o_ref)


def _ctx_attention(qkv, gb_a, layer, seq_len):
    ntok = qkv.shape[1]
    spec = lambda part: pl.BlockSpec((N_HEADS, seq_len, HEAD_DIM), lambda b: (part, b, 0))
    return pl.pallas_call(
        _ctx_attn_kernel,
        grid=(ntok // seq_len,),
        in_specs=[spec(0), spec(1), spec(2),
                  pl.BlockSpec((None, 1, D_A), lambda b: (layer, 0, 0))],
        out_specs=pl.BlockSpec((seq_len, D_A), lambda b: (b, 0)),
        out_shape=jax.ShapeDtypeStruct((ntok, D_A), BF16),
        scratch_shapes=[pltpu.VMEM((N_HEADS, seq_len, HEAD_DIM), F32)],
        compiler_params=_params(("arbitrary",)),
        name="ctx_attention",
    )(qkv, qkv, qkv, gb_a)


def _local_bias_table(rpb, rows):
    n_rb = rows // Q_ROWS
    exact = lax.Precision.HIGHEST
    n_dr = 2 * WIN_H - 1
    n_dc = 2 * WIN_W - 1
    sel_r, valid_r = [], []
    for jb in (0, 1, n_rb - 1):
        r0 = Q_ROWS * jb
        ks = min(max(r0 - WIN_H // 2, 0), rows - K_ROWS)
        r = r0 + np.arange(Q_ROWS)[:, None]
        kr = ks + np.arange(K_ROWS)[None, :]
        rs = np.clip(r - WIN_H // 2, 0, rows - WIN_H)
        valid_r.append((kr >= rs) & (kr < rs + WIN_H))
        ri = np.clip(kr - r + WIN_H - 1, 0, n_dr - 1)
        sel_r.append(ri[..., None] == np.arange(n_dr))
    sel_r = np.stack(sel_r).astype(np.float32)
    valid_r = np.stack(valid_r)
    kc0 = np.concatenate([np.full(nc, k0) for (_, nc, k0) in COL_BLOCKS])
    c = np.arange(GRID_W)[:, None]
    kc = kc0[:, None] + np.arange(KEY_COLS)[None, :]
    cs = np.clip(c - WIN_W // 2, 0, GRID_W - WIN_W)
    valid_c = (kc >= cs) & (kc < cs + WIN_W)
    ci = np.clip(kc - c + WIN_W - 1, 0, n_dc - 1)
    sel_c = (ci[..., None] == np.arange(n_dc)).astype(np.float32)
    by_row = jnp.einsum("trka,hab->htrkb", sel_r, rpb, precision=exact)
    b = jnp.einsum("htrkb,cjb->htrckj", by_row, sel_c, precision=exact)
    valid = valid_r[:, :, None, :, None] & valid_c[None, None, :, None, :]
    b = jnp.where(valid[None], b * LOG2E, NEG)
    return b.reshape(rpb.shape[0], 3, Q_ROWS * GRID_W, K_ROWS * KEY_COLS)


def _nbr_attn_kernel(q_ref, k0, k1, k2, k3, v0, v1, v2, v3, ck_ref, cv_ref, bias_ref, gb_ref,
                     o_ref, o_scr):
    k_refs = (k0, k1, k2, k3)
    v_refs = (v0, v1, v2, v3)
    rows_per_ref = K_ROWS // len(k_refs)

    def window(refs, h, kc0):
        parts = []
        for kr in range(K_ROWS):
            ref = refs[kr // rows_per_ref]
            parts.append(ref[h, pl.ds((kr % rows_per_ref) * GRID_W + kc0, KEY_COLS), :])
        return jnp.concatenate(parts, axis=0)

    def query_rows(ref, h, c0, nc):
        return jnp.concatenate(
            [ref[h, pl.ds(qr * GRID_W + c0, nc), :] for qr in range(Q_ROWS)], axis=0)

    def head(h, carry):
        kc = ck_ref[h].astype(BF16)
        vc = cv_ref[h].astype(BF16)
        for (c0, nc, kc0) in COL_BLOCKS:
            qi = query_rows(q_ref, h, c0, nc).astype(BF16)
            ki = window(k_refs, h, kc0)
            vi = window(v_refs, h, kc0)
            s_loc = _dot_nt(qi, ki) * SCALE_LOG2E + query_rows(bias_ref, h, c0, nc)
            s_ctx = _dot_nt(qi, kc) * SCALE_LOG2E
            m = jnp.maximum(jnp.max(s_loc, axis=-1, keepdims=True),
                            jnp.max(s_ctx, axis=-1, keepdims=True))
            p_loc = jnp.exp2(s_loc - m)
            p_ctx = jnp.exp2(s_ctx - m)
            l = jnp.sum(p_loc, axis=-1, keepdims=True) + jnp.sum(p_ctx, axis=-1, keepdims=True)
            o = (_dot(p_loc.astype(BF16), vi) + _dot(p_ctx.astype(BF16), vc)) / l
            for qr in range(Q_ROWS):
                o_scr[h, pl.ds(qr * GRID_W + c0, nc), :] = o[qr * nc:(qr + 1) * nc]
        return carry

    lax.fori_loop(0, N_HEADS, head, 0)
    _write_normed_heads(o_scr, gb_ref, o_ref)


def _nbr_attention(q, kv, cache_k, cache_v, layer, bias_tab, gb_a, n_req, seq_len):
    rows = seq_len // GRID_W
    n_rb = rows // Q_ROWS
    tq = Q_ROWS * GRID_W
    n_kref = 4
    tk = (K_ROWS // n_kref) * GRID_W
    kblk_per_req = seq_len // tk
    ctx_len = cache_k.shape[3]

    def kmap(part, m):
        def f(b, jb):
            start = jnp.clip(2 * jb - 1, 0, kblk_per_req - n_kref)
            return (part, b * kblk_per_req + start + m, 0)
        return f

    kspecs = [pl.BlockSpec((N_HEADS, tk, HEAD_DIM), kmap(0, m)) for m in range(n_kref)]
    vspecs = [pl.BlockSpec((N_HEADS, tk, HEAD_DIM), kmap(1, m)) for m in range(n_kref)]
    cspec = pl.BlockSpec((None, None, N_HEADS, ctx_len, HEAD_DIM), lambda b, jb: (b, layer, 0, 0, 0))
    kind = lambda b, jb: (layer, (jb > 0).astype(I32) + (jb == n_rb - 1).astype(I32), 0, 0)
    return pl.pallas_call(
        _nbr_attn_kernel,
        grid=(n_req, n_rb),
        in_specs=[pl.BlockSpec((N_HEADS, tq, HEAD_DIM), lambda b, jb: (0, b * n_rb + jb, 0))]
        + kspecs + vspecs + [
            cspec, cspec,
            pl.BlockSpec((N_HEADS, None, tq, K_ROWS * KEY_COLS), kind),
            pl.BlockSpec((None, 1, D_A), lambda b, jb: (layer, 0, 0)),
        ],
        out_specs=pl.BlockSpec((tq, D_A), lambda b, jb: (b * n_rb + jb, 0)),
        out_shape=jax.ShapeDtypeStruct((n_req * seq_len, D_A), BF16),
        scratch_shapes=[pltpu.VMEM((N_HEADS, tq, HEAD_DIM), F32)],
        compiler_params=_params(("arbitrary", "arbitrary")),
        name="nbr_attention",
    )(q, *([kv] * (2 * n_kref)), cache_k, cache_v, bias_tab, gb_a)


def _mixers_kernel(u_ref, gv_ref, p_ref, pprev_ref, pnext_ref, gsgu_ref, wsgu_ref, bsgu_ref,
                   wpool_ref, spool_ref, gb_ref, o_ref, ext_scr, *, seq_len):
    i = pl.program_id(0)
    blocks_per_seq = seq_len // TM_MIX
    bi = i % blocks_per_seq
    t0 = bi * TM_MIX

    gu = jax.nn.gelu(u_ref[...])
    gg = _rms(jax.nn.gelu(gv_ref[...]), gsgu_ref[...]).astype(BF16)
    cols = []
    for g in range(N_GROUPS_B):
        sl = slice(g * CHUNK, (g + 1) * CHUNK)
        chunks = []
        for n in range(TM_MIX // CHUNK):
            rows = slice(n * CHUNK, (n + 1) * CHUNK)
            chunks.append(_dot(wsgu_ref[g], gg[rows, sl]) + bsgu_ref[:, g:g + 1])
        cols.append(jnp.concatenate(chunks, axis=0))
    o_b = gu * jnp.concatenate(cols, axis=1)
    o_ref[:, 0:D_B] = _rms(o_b, gb_ref[:, 0:D_B]).astype(o_ref.dtype)

    p = p_ref[...]
    ext_scr[pl.ds(0, POOL_HALO), :] = jnp.where(bi > 0, pprev_ref[...], 0.0)
    ext_scr[pl.ds(POOL_HALO, TM_MIX), :] = p
    ext_scr[pl.ds(POOL_HALO + TM_MIX, POOL_HALO), :] = jnp.where(
        bi < blocks_per_seq - 1, pnext_ref[...], 0.0)
    t = t0 + lax.broadcasted_iota(I32, (TM_MIX, 1), 0)
    outs = []
    for g, w in enumerate(POOL_WINDOWS):
        half = w // 2
        sl = slice(g * CHUNK, (g + 1) * CHUNK)
        acc = None
        for d in range(-half, half):
            part = ext_scr[pl.ds(POOL_HALO + d, TM_MIX), sl]
            acc = part if acc is None else acc + part
        cnt = (jnp.minimum(t + half, seq_len) - jnp.maximum(t - half, 0)).astype(F32)
        pooled = acc / cnt - p[:, sl]
        outs.append(_dot(pooled.astype(BF16), wpool_ref[g]))
    o_c = jnp.concatenate(outs, axis=1) * spool_ref[...]
    o_ref[:, D_B:D_B + D_C] = _rms(o_c, gb_ref[:, D_B:D_B + D_C]).astype(o_ref.dtype)


def _mixers(ugp, g_sgu, w_sgu, b_sgu_t, w_pool, s_pool, gb_bc, layer, seq_len):
    ntok = ugp.shape[0]
    nblk = ntok // TM_MIX
    hb = TM_MIX // POOL_HALO
    n_halo = ntok // POOL_HALO
    blk = lambda part: pl.BlockSpec((TM_MIX, D_B), lambda i: (i, part))
    full = lambda shape: pl.BlockSpec((None,) + shape, lambda i: (layer,) + (0,) * len(shape))
    return pl.pallas_call(
        functools.partial(_mixers_kernel, seq_len=seq_len),
        grid=(nblk,),
        in_specs=[
            blk(0), blk(1), blk(2),
            pl.BlockSpec((POOL_HALO, D_C), lambda i: (jnp.maximum(i * hb - 1, 0), 2)),
            pl.BlockSpec((POOL_HALO, D_C), lambda i: (jnp.minimum((i + 1) * hb, n_halo - 1), 2)),
            full((1, D_B)), full((N_GROUPS_B, CHUNK, CHUNK)), full((CHUNK, N_GROUPS_B)),
            full((len(POOL_WINDOWS), CHUNK, CHUNK)), full((1, D_C)), full((1, D_B + D_C)),
        ],
        out_specs=pl.BlockSpec((TM_MIX, D_B + D_C), lambda i: (i, 0)),
        out_shape=jax.ShapeDtypeStruct((ntok, D_B + D_C), BF16),
        scratch_shapes=[pltpu.VMEM((TM_MIX + 2 * POOL_HALO, D_C), F32)],
        compiler_params=_params(("arbitrary",)),
        name="mixers",
    )(ugp, ugp, ugp, ugp, ugp, g_sgu, w_sgu, b_sgu_t, w_pool, s_pool, gb_bc)


def _out_proj_kernel(x_ref, oa_ref, obc_ref, mod_ref, g_ref, w_ref, wr_ref, br_ref,
                     x1_ref, h_ref, e_ref, gate_ref):
    for m in range(TM_OUT // OUT_CHUNK):
        rows = pl.ds(m * OUT_CHUNK, OUT_CHUNK)
        _out_proj_rows(rows, x_ref, oa_ref, obc_ref, mod_ref, g_ref, w_ref, wr_ref, br_ref,
                       x1_ref, h_ref, e_ref, gate_ref)


def _out_proj_rows(rows, x_ref, oa_ref, obc_ref, mod_ref, g_ref, w_ref, wr_ref, br_ref,
                   x1_ref, h_ref, e_ref, gate_ref):
    merged = jnp.concatenate([oa_ref[rows, :], obc_ref[rows, :]], axis=1)
    mix = _dot(merged, w_ref[...])
    x1 = x_ref[rows, :] + mod_ref[2:3, :] * mix
    x1_ref[rows, :] = x1
    h = _rms(x1, g_ref[...]) * (1.0 + mod_ref[4:5, :]) + mod_ref[3:4, :]
    h_hi = h.astype(BF16)
    h_ref[rows, :] = _pack_bf16_pairs(h_hi)

    h_lo = (h - h_hi.astype(F32)).astype(BF16)
    part = _dot(h_hi, wr_ref[...])
    logits = (part[:, :LANES] + part[:, LANES:] + _dot(h_lo, wr_ref[:, :LANES])) + br_ref[...]
    lane = lax.broadcasted_iota(I32, logits.shape, 1)
    big = jnp.int32(LANES)

    def softmax_masked(mask):
        z = jnp.where(mask, logits, NEG)
        m = jnp.max(z, axis=-1, keepdims=True)
        e = jnp.where(mask, jnp.exp(z - m), 0.0)
        return e / jnp.sum(e, axis=-1, keepdims=True)

    def top1(vals, mask):
        v = jnp.where(mask, vals, -1.0)
        best = jnp.max(v, axis=-1, keepdims=True)
        idx = jnp.min(jnp.where(mask & (v == best), lane, big), axis=-1, keepdims=True)
        return best, idx

    gmask = lane < N_GROUPS_E
    p_grp = softmax_masked(gmask)
    p_g, g_sel = top1(p_grp, gmask)
    lo = N_GROUPS_E + EXPERTS_PER_GROUP * g_sel
    emask = (lane >= lo) & (lane < lo + EXPERTS_PER_GROUP)
    p_exp = softmax_masked(emask)
    p1, i1 = top1(p_exp, emask)
    p2, i2 = top1(p_exp, emask & (lane != i1))
    denom = p1 + p2
    g1 = p_g * p1 / denom
    g2 = p_g * p2 / denom
    e_ref[rows, :] = jnp.where(lane == 0, i1 - N_GROUPS_E,
                               jnp.where(lane == 1, i2 - N_GROUPS_E, 0))
    gate_ref[rows, :] = jnp.where(lane == 0, g1, jnp.where(lane == 1, g2, 0.0))


def _out_proj(x2d, ntok, o_a, o_bc, mod, mod_row_fn, g_ffn, w_out, w_r, b_r, layer):
    nblk = ntok // TM_OUT
    full = lambda shape: pl.BlockSpec((None,) + shape, lambda i: (layer,) + (0,) * len(shape))
    return pl.pallas_call(
        _out_proj_kernel,
        grid=(nblk,),
        in_specs=[
            pl.BlockSpec((TM_OUT, D_MODEL), lambda i: (i, 0)),
            pl.BlockSpec((TM_OUT, D_A), lambda i: (i, 0)),
            pl.BlockSpec((TM_OUT, D_B + D_C), lambda i: (i, 0)),
            pl.BlockSpec((None, None, N_MOD, D_MODEL), lambda i: (layer, mod_row_fn(i), 0, 0)),
            full((1, D_MODEL)), full((D_MODEL, D_MODEL)), full((D_MODEL, 2 * LANES)), full((1, LANES)),
        ],
        out_specs=[
            pl.BlockSpec((TM_OUT, D_MODEL), lambda i: (i, 0)),
            pl.BlockSpec((TM_OUT, D_PACK), lambda i: (i, 0)),
            pl.BlockSpec((TM_OUT, LANES), lambda i: (i, 0)),
            pl.BlockSpec((TM_OUT, LANES), lambda i: (i, 0)),
        ],
        out_shape=[
            jax.ShapeDtypeStruct((ntok, D_MODEL), F32),
            jax.ShapeDtypeStruct((ntok, D_PACK), U32),
            jax.ShapeDtypeStruct((ntok, LANES), I32),
            jax.ShapeDtypeStruct((ntok, LANES), F32),
        ],
        compiler_params=_params(("arbitrary",)),
        name="out_proj_router",
    )(x2d, o_a, o_bc, mod, g_ffn, w_out, w_r, b_r)


def _route_plan(e_flat):
    n_assign = e_flat.shape[0]
    onehot = (e_flat[:, None] == jnp.arange(N_EXPERTS, dtype=I32)[None, :]).astype(I32)
    csum = jnp.cumsum(onehot, axis=0)
    counts = csum[-1]
    rank = jnp.sum(onehot * (csum - 1), axis=1)
    padded = (counts + MOE_BM - 1) // MOE_BM * MOE_BM
    pad_end = jnp.cumsum(padded)
    pad_start = pad_end - padded
    pos = jnp.sum(onehot * pad_start[None, :], axis=1) + rank
    n_blocks = n_assign // MOE_BM + N_EXPERTS
    starts = jnp.arange(n_blocks, dtype=I32) * MOE_BM
    block_e = jnp.minimum(jnp.sum((starts[:, None] >= pad_end[None, :]).astype(I32), axis=1),
                          N_EXPERTS - 1).astype(I32)
    n_used = (pad_end[-1] // MOE_BM).astype(I32).reshape(1)
    return pos.astype(I32), block_e, n_used, n_blocks


def _scatter_kernel(pos_ref, h_ref, xs_in_ref, xs_ref, sem):
    del xs_in_ref

    def issue(r, carry):
        for k in range(TOP_K):
            slot = pos_ref[0, 0, TOP_K * r + k]
            pltpu.make_async_copy(
                h_ref.at[pl.ds(r, 1)], xs_ref.at[pl.ds(slot, 1)], sem).start(priority=k)
        return carry

    lax.fori_loop(0, TM_ROW, issue, 0, unroll=8)
    for _ in range(TOP_K):
        pltpu.make_async_copy(h_ref, h_ref, sem).wait()


def _scatter_rows(pos3, h, xs):
    ntok = h.shape[0]
    return pl.pallas_call(
        _scatter_kernel,
        grid=(ntok // TM_ROW,),
        in_specs=[
            pl.BlockSpec((1, 1, TOP_K * TM_ROW), lambda i: (i, 0, 0), memory_space=pltpu.SMEM),
            pl.BlockSpec((TM_ROW, D_PACK), lambda i: (i, 0)),
            pl.BlockSpec(memory_space=pl.ANY),
        ],
        out_specs=pl.BlockSpec(memory_space=pl.ANY),
        out_shape=jax.ShapeDtypeStruct(xs.shape, xs.dtype),
        scratch_shapes=[pltpu.SemaphoreType.DMA(())],
        input_output_aliases={2: 0},
        compiler_params=_params(("arbitrary",)),
        name="moe_scatter",
    )(pos3, h, xs)


def _experts_kernel(be_ref, nu_ref, xs_ref, wg_ref, wu_ref, wd_ref, ys_ref):
    del be_ref
    i = pl.program_id(0)

    @pl.when(i < nu_ref[0])
    def _():
        x = _unpack_bf16_pairs(xs_ref[...], BF16)
        g = _dot(x, wg_ref[...].astype(BF16))
        u = _dot(x, wu_ref[...].astype(BF16))
        a = (g * jax.nn.sigmoid(g) * u).astype(BF16)
        ys_ref[...] = _pack_bf16_pairs(_dot(a, wd_ref[...].astype(BF16)).astype(BF16))

    @pl.when(i >= nu_ref[0])
    def _():
        ys_ref[...] = jnp.zeros_like(ys_ref)


def _experts(block_e, n_used, xs, wg, wu, wd, layer, n_blocks):
    grid_spec = pltpu.PrefetchScalarGridSpec(
        num_scalar_prefetch=2,
        grid=(n_blocks,),
        in_specs=[
            pl.BlockSpec((MOE_BM, D_PACK), lambda i, be, nu: (jnp.minimum(i, nu[0] - 1), 0)),
            pl.BlockSpec((None, None, D_MODEL, D_EXPERT), lambda i, be, nu: (layer, be[i], 0, 0)),
            pl.BlockSpec((None, None, D_MODEL, D_EXPERT), lambda i, be, nu: (layer, be[i], 0, 0)),
            pl.BlockSpec((None, None, D_EXPERT, D_MODEL), lambda i, be, nu: (layer, be[i], 0, 0)),
        ],
        out_specs=pl.BlockSpec((MOE_BM, D_PACK), lambda i, be, nu: (i, 0)),
    )
    return pl.pallas_call(
        _experts_kernel,
        grid_spec=grid_spec,
        out_shape=jax.ShapeDtypeStruct((xs.shape[0], D_PACK), U32),
        compiler_params=_params(("arbitrary",)),
        name="moe_experts",
    )(block_e, n_used, xs, wg, wu, wd)


def _combine_kernel(pos_ref, pos_next_ref, x1_ref, gate_ref, mod_ref, gfin_ref, ys_ref, o_ref,
                    ybuf, sem, *, final_norm):
    i = pl.program_id(0)
    buf = i % 2

    def issue(p_ref, b):
        def body(r, carry):
            for k in range(TOP_K):
                slot = p_ref[0, 0, TOP_K * r + k]
                pltpu.make_async_copy(
                    ys_ref.at[pl.ds(slot, 1)], ybuf.at[b, k, pl.ds(r, 1)], sem.at[b]
                ).start(priority=k)
            return carry
        lax.fori_loop(0, TM_ROW, body, 0, unroll=8)

    @pl.when(i == 0)
    def _():
        issue(pos_ref, 0)

    @pl.when(i + 1 < pl.num_programs(0))
    def _():
        issue(pos_next_ref, 1 - buf)

    for k in range(TOP_K):
        pltpu.make_async_copy(ybuf.at[buf, k], ybuf.at[buf, k], sem.at[buf]).wait()
    y0 = _unpack_bf16_pairs(ybuf[buf, 0], F32)
    y1 = _unpack_bf16_pairs(ybuf[buf, 1], F32)
    f = gate_ref[:, 0:1] * y0 + gate_ref[:, 1:2] * y1
    x2 = x1_ref[...] + mod_ref[5:6, :] * f
    if final_norm:
        x2 = _rms(x2, gfin_ref[...])
    o_ref[...] = x2


def _combine(pos3, x1, gate, mod, mod_row_fn, g_final, ys, layer, final_norm):
    ntok = x1.shape[0]
    nblk = ntok // TM_ROW
    return pl.pallas_call(
        functools.partial(_combine_kernel, final_norm=final_norm),
        grid=(nblk,),
        in_specs=[
            pl.BlockSpec((1, 1, TOP_K * TM_ROW), lambda i: (i, 0, 0), memory_space=pltpu.SMEM),
            pl.BlockSpec((1, 1, TOP_K * TM_ROW), lambda i: (jnp.minimum(i + 1, nblk - 1), 0, 0),
                         memory_space=pltpu.SMEM),
            pl.BlockSpec((TM_ROW, D_MODEL), lambda i: (i, 0)),
            pl.BlockSpec((TM_ROW, LANES), lambda i: (i, 0)),
            pl.BlockSpec((None, None, N_MOD, D_MODEL), lambda i: (layer, mod_row_fn(i), 0, 0)),
            pl.BlockSpec((1, D_MODEL), lambda i: (0, 0)),
            pl.BlockSpec(memory_space=pl.ANY),
        ],
        out_specs=pl.BlockSpec((TM_ROW, D_MODEL), lambda i: (i, 0)),
        out_shape=jax.ShapeDtypeStruct((ntok, D_MODEL), F32),
        scratch_shapes=[pltpu.VMEM((2, TOP_K, TM_ROW, D_PACK), U32),
                        pltpu.SemaphoreType.DMA((2,))],
        compiler_params=_params(("arbitrary",)),
        name="moe_combine",
    )(pos3, pos3, x1, gate, mod, g_final, ys)


def kernel(x_prompt, x_sample, cache_k, cache_v, c, c_ctx, w_mod, b_mod, g_mix, g_ffn, w_in, rpb,
           g_sgu, w_sgu, b_sgu, w_pool, s_pool, g_branch, w_out, w_rg, b_rg, w_re, b_re,
           w_e_gate, w_e_up, w_e_down, g_final):
    n_ctx, seq, _ = x_prompt.shape
    n_lat, lat_seq, _ = x_sample.shape
    ntok_p = n_ctx * seq
    ntok_s = n_lat * lat_seq
    rows = lat_seq // GRID_W

    n_mod_rows = -(-(1 + n_lat) // SUBLANES) * SUBLANES
    c_all = jnp.concatenate(
        [c_ctx[None], c, jnp.zeros((n_mod_rows - 1 - n_lat, D_MODEL), F32)], axis=0)
    mod = _modulation(c_all, w_mod, b_mod).reshape(DEPTH, n_mod_rows, N_MOD, D_MODEL)

    xp = x_prompt.reshape(ntok_p, D_MODEL)
    xs_lat = x_sample.reshape(ntok_s, D_MODEL)

    def lat_mod_row(tm):
        return lambda i: 1 + (i * tm) // lat_seq

    ctx_mod_row = lambda i: 0

    w_in_b = w_in.astype(BF16).reshape(DEPTH, D_MODEL, IN_WIDTH // TN_IN, TN_IN).transpose(0, 2, 1, 3)
    w_out_b = w_out.astype(BF16)
    w_sgu_b = w_sgu.astype(BF16)
    w_pool_b = w_pool.astype(BF16)

    gmix = g_mix[:, None, :]
    gffn = g_ffn[:, None, :]
    gb_a = g_branch[:, None, :D_A]
    gb_bc = g_branch[:, None, D_A:]
    gsgu = g_sgu[:, None, :]
    bsgu_t = jnp.swapaxes(b_sgu, 1, 2)
    spool = s_pool[:, None, :]
    n_route = N_GROUPS_E + N_EXPERTS
    w_r = jnp.concatenate(
        [w_rg, w_re, jnp.zeros((DEPTH, D_MODEL, LANES - n_route), F32)], axis=2)
    w_r_hi = lax.bitcast_convert_type(
        lax.bitcast_convert_type(w_r, U32) & jnp.uint32(0xFFFF0000), F32)
    w_r = jnp.concatenate([w_r_hi.astype(BF16), (w_r - w_r_hi).astype(BF16)], axis=2)
    b_r = jnp.concatenate(
        [b_rg, b_re, jnp.zeros((DEPTH, LANES - n_route), F32)], axis=1)[:, None, :]
    bias_tab = _local_bias_table(rpb.reshape((DEPTH * N_HEADS,) + rpb.shape[2:]), rows)

    cache_shape = (n_ctx, DEPTH, N_HEADS, seq, HEAD_DIM)
    caches = (jnp.zeros(cache_shape, F32), jnp.zeros(cache_shape, F32))
    for l in range(DEPTH):
        qkv_p, ugp_p, *caches = _in_proj(
            xp, ntok_p, mod, ctx_mod_row, gmix, w_in_b, l, seq, True, tuple(caches))
        oa_p = _ctx_attention(qkv_p, gb_a, l, seq)
        obc_p = _mixers(ugp_p, gsgu, w_sgu_b, bsgu_t, w_pool_b, spool, gb_bc, l, seq)
        x1p, hp, ep, gatep = _out_proj(
            xp, ntok_p, oa_p, obc_p, mod, ctx_mod_row, gffn, w_out_b, w_r, b_r, l)

        q_s, kv_s, ugp_s = _in_proj(
            xs_lat, ntok_s, mod, lat_mod_row(TM_IN), gmix, w_in_b, l, lat_seq, False)
        oa_s = _nbr_attention(q_s, kv_s, cache_k, cache_v, l, bias_tab, gb_a, n_lat, lat_seq)
        obc_s = _mixers(ugp_s, gsgu, w_sgu_b, bsgu_t, w_pool_b, spool, gb_bc, l, lat_seq)
        x1s, hs, es, gates = _out_proj(
            xs_lat, ntok_s, oa_s, obc_s, mod, lat_mod_row(TM_OUT), gffn, w_out_b, w_r, b_r, l)

        e_flat = jnp.concatenate([ep[:, :TOP_K].reshape(-1), es[:, :TOP_K].reshape(-1)])
        pos, block_e, n_used, n_blocks = _route_plan(e_flat)
        pos_p = pos[:TOP_K * ntok_p].reshape(ntok_p // TM_ROW, 1, TOP_K * TM_ROW)
        pos_s = pos[TOP_K * ntok_p:].reshape(ntok_s // TM_ROW, 1, TOP_K * TM_ROW)
        slots = jnp.zeros((n_blocks * MOE_BM, D_PACK), U32)
        slots = _scatter_rows(pos_p, hp, slots)
        slots = _scatter_rows(pos_s, hs, slots)
        ys = _experts(block_e, n_used, slots, w_e_gate, w_e_up, w_e_down, l, n_blocks)
        last = l == DEPTH - 1
        gfin = g_final[None]
        xp = _combine(pos_p, x1p, gatep, mod, ctx_mod_row, gfin, ys, l, last)
        xs_lat = _combine(pos_s, x1s, gates, mod, lat_mod_row(TM_ROW), gfin, ys, l, last)

    y_prompt = xp.reshape(n_ctx, seq, D_MODEL)
    y_sample = xs_lat.reshape(n_lat, lat_seq, D_MODEL)
    return (y_prompt, y_sample, caches[0], caches[1])
```

```python
import functools

import jax
import numpy as np
import jax.numpy as jnp
from jax import lax
from jax.experimental import pallas as pl
from jax.experimental.pallas import tpu as pltpu

F32 = jnp.float32
BF16 = jnp.bfloat16
I32 = jnp.int32
U32 = jnp.uint32

D_MODEL = 2048
DEPTH = 2
GRID_W = 64
HEAD_DIM = 128
D_A = 1024
N_HEADS = 8
WIN_H = 8
WIN_W = 16
D_B = 512
N_GROUPS_B = 4
CHUNK = 128
D_C = 512
POOL_WINDOWS = (2, 4, 8, 16)
IN_WIDTH = 3 * D_A + 2 * D_B + D_C
N_GROUPS_E = 4
EXPERTS_PER_GROUP = 8
N_EXPERTS = 32
TOP_K = 2
D_EXPERT = 512
N_MOD = 6
EPS = 1e-6
NEG = -1e30
LOG2E = 1.4426950408889634
SCALE_LOG2E = HEAD_DIM ** -0.5 * LOG2E

LANES = 128
SUBLANES = 8
VMEM_LIMIT_BYTES = 56 * 1024 * 1024

TM_IN = 1024
TN_IN = 512
IN_CHUNK = 256
OUT_CHUNK = 512
TM_OUT = 512
TM_MIX = 256
Q_ROWS = 8
K_ROWS = 16
MOE_BM = 512
D_PACK = D_MODEL // 2
TM_ROW = 512
POOL_HALO = 8

COL_BLOCKS = ((0, 24, 0), (24, 16, 16), (40, 24, 32))
KEY_COLS = 32


def _params(sem):
    return pltpu.CompilerParams(dimension_semantics=sem, vmem_limit_bytes=VMEM_LIMIT_BYTES)


def _dot(a, b):
    return jnp.dot(a, b, preferred_element_type=F32)


def _dot_nt(a, b):
    return lax.dot_general(a, b, (((1,), (1,)), ((), ())), preferred_element_type=F32)


def _rms(x, g):
    return x * lax.rsqrt(jnp.mean(x * x, axis=-1, keepdims=True) + EPS) * g


def _pack_bf16_pairs(x):
    n = x.shape[1] // 2
    lo = lax.bitcast_convert_type(x[:, :n].astype(F32), U32) >> 16
    hi = lax.bitcast_convert_type(x[:, n:].astype(F32), U32) & jnp.uint32(0xFFFF0000)
    return hi | lo


def _unpack_bf16_pairs(w, dtype):
    lo = lax.bitcast_convert_type(w << 16, F32).astype(dtype)
    hi = lax.bitcast_convert_type(w & jnp.uint32(0xFFFF0000), F32).astype(dtype)
    return jnp.concatenate([lo, hi], axis=1)


def _mod_kernel(c_ref, w_ref, b_ref, o_ref):
    c = c_ref[...]
    s = c * jax.nn.sigmoid(c)
    n = s.shape[0]
    w = w_ref[...]
    s_hi = s.astype(BF16)
    s_lo = (s - s_hi.astype(F32)).astype(BF16)
    w_hi = w.astype(BF16)
    w_lo = (w - w_hi.astype(F32)).astype(BF16)
    r = _dot(jnp.concatenate([s_hi, s_lo], axis=0), w_hi)
    o_ref[...] = r[:n] + r[n:] + _dot(s_hi, w_lo) + b_ref[...]


def _modulation(c_all, w_mod, b_mod):
    tn = 1024
    nrow = c_all.shape[0]
    width = w_mod.shape[2]
    return pl.pallas_call(
        _mod_kernel,
        grid=(DEPTH, width // tn),
        in_specs=[
            pl.BlockSpec((nrow, D_MODEL), lambda l, j: (0, 0)),
            pl.BlockSpec((None, D_MODEL, tn), lambda l, j: (l, 0, j)),
            pl.BlockSpec((None, 1, tn), lambda l, j: (l, 0, j)),
        ],
        out_specs=pl.BlockSpec((None, nrow, tn), lambda l, j: (l, 0, j)),
        out_shape=jax.ShapeDtypeStruct((DEPTH, nrow, width), F32),
        compiler_params=_params(("arbitrary", "arbitrary")),
        name="modulation",
    )(c_all, w_mod, b_mod.reshape(DEPTH, 1, width))


def _in_proj_kernel(x_ref, mod_ref, g_ref, w_ref, *refs, ctx, n_aliased):
    refs = refs[n_aliased:]
    if ctx:
        qkv_ref, ugp_ref, kc_ref, vc_ref, h_scr = refs
    else:
        q_ref, kv_ref, ugp_ref, h_scr = refs
    j = pl.program_id(1)
    heads = TN_IN // HEAD_DIM

    def head_store(ref, dtype):
        def store(m, rows, acc):
            for hh in range(heads):
                ref[hh, rows, :] = acc[:, hh * HEAD_DIM:(hh + 1) * HEAD_DIM].astype(dtype)
        return store

    def cache_store(ref):
        seq = ref.shape[2]
        per_chunk = IN_CHUNK // seq

        def store(m, rows, acc):
            for s in range(per_chunk):
                for hh in range(heads):
                    ref[m * per_chunk + s, hh] = acc[s * seq:(s + 1) * seq,
                                                     hh * HEAD_DIM:(hh + 1) * HEAD_DIM]
        return store

    def ugp_store(m, rows, acc):
        ugp_ref[rows, :] = acc

    def run(normalize, stores):
        for m in range(TM_IN // IN_CHUNK):
            rows = pl.ds(m * IN_CHUNK, IN_CHUNK)
            if normalize:
                h = _rms(x_ref[rows, :], g_ref[...]) * (1.0 + mod_ref[1:2, :]) + mod_ref[0:1, :]
                hb = h.astype(BF16)
                h_scr[rows, :] = hb
            else:
                hb = h_scr[rows, :]
            acc = _dot(hb, w_ref[...])
            for store in stores:
                store(m, rows, acc)

    def case(cond, normalize, stores):
        pl.when(cond)(lambda: run(normalize, stores))

    if ctx:
        case(j == 0, True, [head_store(qkv_ref, BF16)])
        case(j == 1, False, [head_store(qkv_ref, BF16)])
        case((j >= 2) & (j < 4), False, [head_store(qkv_ref, BF16), cache_store(kc_ref)])
        case((j >= 4) & (j < 6), False, [head_store(qkv_ref, BF16), cache_store(vc_ref)])
    else:
        case(j == 0, True, [head_store(q_ref, F32)])
        case(j == 1, False, [head_store(q_ref, F32)])
        case((j >= 2) & (j < 6), False, [head_store(kv_ref, BF16)])
    case(j >= 6, False, [ugp_store])


def _in_proj(x2d, ntok, mod, mod_row_fn, g_mix, w_in, layer, seq_len, ctx, caches=()):
    nblk = ntok // TM_IN
    ncol = IN_WIDTH // TN_IN
    heads = TN_IN // HEAD_DIM
    hm = lambda lo, n: (lambda i, j: (jnp.clip(j - lo, 0, n - 1), i, 0))
    hspec = lambda lo, n: pl.BlockSpec((heads, TM_IN, HEAD_DIM), hm(lo, n))
    ugp_shape = jax.ShapeDtypeStruct((ntok, 2 * D_B + D_C), F32)
    ugp_spec = pl.BlockSpec((TM_IN, TN_IN), lambda i, j: (i, jnp.clip(j - 6, 0, 2)))
    if ctx:
        nseq = ntok // seq_len
        spb = TM_IN // seq_len
        cache_shape = jax.ShapeDtypeStruct((nseq, DEPTH, N_HEADS, seq_len, HEAD_DIM), F32)
        cache_spec = lambda lo: pl.BlockSpec(
            (spb, None, heads, seq_len, HEAD_DIM),
            lambda i, j: (i, layer, jnp.clip(j - lo, 0, 1), 0, 0))
        out_shape = [jax.ShapeDtypeStruct((3 * N_HEADS, ntok, HEAD_DIM), BF16), ugp_shape,
                     cache_shape, cache_shape]
        out_specs = [hspec(0, 6), ugp_spec, cache_spec(2), cache_spec(4)]
    else:
        out_shape = [jax.ShapeDtypeStruct((N_HEADS, ntok, HEAD_DIM), F32),
                     jax.ShapeDtypeStruct((2 * N_HEADS, ntok, HEAD_DIM), BF16), ugp_shape]
        out_specs = [hspec(0, 2), hspec(2, 4), ugp_spec]
    return pl.pallas_call(
        functools.partial(_in_proj_kernel, ctx=ctx, n_aliased=len(caches)),
        grid=(nblk, ncol),
        in_specs=[
            pl.BlockSpec((TM_IN, D_MODEL), lambda i, j: (i, 0)),
            pl.BlockSpec((None, None, N_MOD, D_MODEL), lambda i, j: (layer, mod_row_fn(i), 0, 0)),
            pl.BlockSpec((None, 1, D_MODEL), lambda i, j: (layer, 0, 0)),
            pl.BlockSpec((None, D_MODEL, TN_IN), lambda i, j: (layer, 0, j)),
        ] + [pl.BlockSpec(memory_space=pl.ANY)] * len(caches),
        out_specs=out_specs,
        out_shape=out_shape,
        scratch_shapes=[pltpu.VMEM((TM_IN, D_MODEL), BF16)],
        input_output_aliases={4 + n: 2 + n for n in range(len(caches))},
        compiler_params=_params(("arbitrary", "arbitrary")),
        name="in_proj_ctx" if ctx else "in_proj_lat",
    )(x2d, mod, g_mix, w_in, *caches)


def _write_normed_heads(o_scr, gb_ref, o_ref):
    ss = None
    for h in range(N_HEADS):
        oh = o_scr[h]
        t = jnp.sum(oh * oh, axis=-1, keepdims=True)
        ss = t if ss is None else ss + t
    r = lax.rsqrt(ss * (1.0 / D_A) + EPS)
    for h in range(N_HEADS):
        sl = slice(h * HEAD_DIM, (h + 1) * HEAD_DIM)
        o_ref[:, sl] = (o_scr[h] * r * gb_ref[:, sl]).astype(o_ref.dtype)


def _ctx_attn_kernel(q_ref, k_ref, v_ref, gb_ref, o_ref, o_scr):
    def head(h, carry):
        s = _dot_nt(q_ref[h], k_ref[h]) * SCALE_LOG2E
        m = jnp.max(s, axis=-1, keepdims=True)
        p = jnp.exp2(s - m)
        l = jnp.sum(p, axis=-1, keepdims=True)
        o_scr[h] = _dot(p.astype(BF16), v_ref[h]) / l
        return carry

    lax.fori_loop(0, N_HEADS, head, 0)
    _write_normed_heads(o_scr, gb_ref, o_ref)


def _ctx_attention(qkv, gb_a, layer, seq_len):
    ntok = qkv.shape[1]
    spec = lambda part: pl.BlockSpec((N_HEADS, seq_len, HEAD_DIM), lambda b: (part, b, 0))
    return pl.pallas_call(
        _ctx_attn_kernel,
        grid=(ntok // seq_len,),
        in_specs=[spec(0), spec(1), spec(2),
                  pl.BlockSpec((None, 1, D_A), lambda b: (layer, 0, 0))],
        out_specs=pl.BlockSpec((seq_len, D_A), lambda b: (b, 0)),
        out_shape=jax.ShapeDtypeStruct((ntok, D_A), BF16),
        scratch_shapes=[pltpu.VMEM((N_HEADS, seq_len, HEAD_DIM), F32)],
        compiler_params=_params(("arbitrary",)),
        name="ctx_attention",
    )(qkv, qkv, qkv, gb_a)


def _local_bias_table(rpb, rows):
    n_rb = rows // Q_ROWS
    exact = lax.Precision.HIGHEST
    n_dr = 2 * WIN_H - 1
    n_dc = 2 * WIN_W - 1
    sel_r, valid_r = [], []
    for jb in (0, 1, n_rb - 1):
        r0 = Q_ROWS * jb
        ks = min(max(r0 - WIN_H // 2, 0), rows - K_ROWS)
        r = r0 + np.arange(Q_ROWS)[:, None]
        kr = ks + np.arange(K_ROWS)[None, :]
        rs = np.clip(r - WIN_H // 2, 0, rows - WIN_H)
        valid_r.append((kr >= rs) & (kr < rs + WIN_H))
        ri = np.clip(kr - r + WIN_H - 1, 0, n_dr - 1)
        sel_r.append(ri[..., None] == np.arange(n_dr))
    sel_r = np.stack(sel_r).astype(np.float32)
    valid_r = np.stack(valid_r)
    kc0 = np.concatenate([np.full(nc, k0) for (_, nc, k0) in COL_BLOCKS])
    c = np.arange(GRID_W)[:, None]
    kc = kc0[:, None] + np.arange(KEY_COLS)[None, :]
    cs = np.clip(c - WIN_W // 2, 0, GRID_W - WIN_W)
    valid_c = (kc >= cs) & (kc < cs + WIN_W)
    ci = np.clip(kc - c + WIN_W - 1, 0, n_dc - 1)
    sel_c = (ci[..., None] == np.arange(n_dc)).astype(np.float32)
    by_row = jnp.einsum("trka,hab->htrkb", sel_r, rpb, precision=exact)
    b = jnp.einsum("htrkb,cjb->htrckj", by_row, sel_c, precision=exact)
    valid = valid_r[:, :, None, :, None] & valid_c[None, None, :, None, :]
    b = jnp.where(valid[None], b * LOG2E, NEG)
    return b.reshape(rpb.shape[0], 3, Q_ROWS * GRID_W, K_ROWS * KEY_COLS)


def _nbr_attn_kernel(q_ref, k0, k1, k2, k3, v0, v1, v2, v3, ck_ref, cv_ref, bias_ref, gb_ref,
                     o_ref, o_scr):
    k_refs = (k0, k1, k2, k3)
    v_refs = (v0, v1, v2, v3)
    rows_per_ref = K_ROWS // len(k_refs)

    def window(refs, h, kc0):
        parts = []
        for kr in range(K_ROWS):
            ref = refs[kr // rows_per_ref]
            parts.append(ref[h, pl.ds((kr % rows_per_ref) * GRID_W + kc0, KEY_COLS), :])
        return jnp.concatenate(parts, axis=0)

    def query_rows(ref, h, c0, nc):
        return jnp.concatenate(
            [ref[h, pl.ds(qr * GRID_W + c0, nc), :] for qr in range(Q_ROWS)], axis=0)

    def head(h, carry):
        kc = ck_ref[h].astype(BF16)
        vc = cv_ref[h].astype(BF16)
        for (c0, nc, kc0) in COL_BLOCKS:
            qi = query_rows(q_ref, h, c0, nc).astype(BF16)
            ki = window(k_refs, h, kc0)
            vi = window(v_refs, h, kc0)
            s_loc = _dot_nt(qi, ki) * SCALE_LOG2E + query_rows(bias_ref, h, c0, nc)
            s_ctx = _dot_nt(qi, kc) * SCALE_LOG2E
            m = jnp.maximum(jnp.max(s_loc, axis=-1, keepdims=True),
                            jnp.max(s_ctx, axis=-1, keepdims=True))
            p_loc = jnp.exp2(s_loc - m)
            p_ctx = jnp.exp2(s_ctx - m)
            l = jnp.sum(p_loc, axis=-1, keepdims=True) + jnp.sum(p_ctx, axis=-1, keepdims=True)
            o = (_dot(p_loc.astype(BF16), vi) + _dot(p_ctx.astype(BF16), vc)) / l
            for qr in range(Q_ROWS):
                o_scr[h, pl.ds(qr * GRID_W + c0, nc), :] = o[qr * nc:(qr + 1) * nc]
        return carry

    lax.fori_loop(0, N_HEADS, head, 0)
    _write_normed_heads(o_scr, gb_ref, o_ref)


def _nbr_attention(q, kv, cache_k, cache_v, layer, bias_tab, gb_a, n_req, seq_len):
    rows = seq_len // GRID_W
    n_rb = rows // Q_ROWS
    tq = Q_ROWS * GRID_W
    n_kref = 4
    tk = (K_ROWS // n_kref) * GRID_W
    kblk_per_req = seq_len // tk
    ctx_len = cache_k.shape[3]

    def kmap(part, m):
        def f(b, jb):
            start = jnp.clip(2 * jb - 1, 0, kblk_per_req - n_kref)
            return (part, b * kblk_per_req + start + m, 0)
        return f

    kspecs = [pl.BlockSpec((N_HEADS, tk, HEAD_DIM), kmap(0, m)) for m in range(n_kref)]
    vspecs = [pl.BlockSpec((N_HEADS, tk, HEAD_DIM), kmap(1, m)) for m in range(n_kref)]
    cspec = pl.BlockSpec((None, None, N_HEADS, ctx_len, HEAD_DIM), lambda b, jb: (b, layer, 0, 0, 0))
    kind = lambda b, jb: (layer, (jb > 0).astype(I32) + (jb == n_rb - 1).astype(I32), 0, 0)
    return pl.pallas_call(
        _nbr_attn_kernel,
        grid=(n_req, n_rb),
        in_specs=[pl.BlockSpec((N_HEADS, tq, HEAD_DIM), lambda b, jb: (0, b * n_rb + jb, 0))]
        + kspecs + vspecs + [
            cspec, cspec,
            pl.BlockSpec((N_HEADS, None, tq, K_ROWS * KEY_COLS), kind),
            pl.BlockSpec((None, 1, D_A), lambda b, jb: (layer, 0, 0)),
        ],
        out_specs=pl.BlockSpec((tq, D_A), lambda b, jb: (b * n_rb + jb, 0)),
        out_shape=jax.ShapeDtypeStruct((n_req * seq_len, D_A), BF16),
        scratch_shapes=[pltpu.VMEM((N_HEADS, tq, HEAD_DIM), F32)],
        compiler_params=_params(("arbitrary", "arbitrary")),
        name="nbr_attention",
    )(q, *([kv] * (2 * n_kref)), cache_k, cache_v, bias_tab, gb_a)


def _mixers_kernel(u_ref, gv_ref, p_ref, pprev_ref, pnext_ref, gsgu_ref, wsgu_ref, bsgu_ref,
                   wpool_ref, spool_ref, gb_ref, o_ref, ext_scr, *, seq_len):
    i = pl.program_id(0)
    blocks_per_seq = seq_len // TM_MIX
    bi = i % blocks_per_seq
    t0 = bi * TM_MIX

    gu = jax.nn.gelu(u_ref[...])
    gg = _rms(jax.nn.gelu(gv_ref[...]), gsgu_ref[...]).astype(BF16)
    cols = []
    for g in range(N_GROUPS_B):
        sl = slice(g * CHUNK, (g + 1) * CHUNK)
        chunks = []
        for n in range(TM_MIX // CHUNK):
            rows = slice(n * CHUNK, (n + 1) * CHUNK)
            chunks.append(_dot(wsgu_ref[g], gg[rows, sl]) + bsgu_ref[:, g:g + 1])
        cols.append(jnp.concatenate(chunks, axis=0))
    o_b = gu * jnp.concatenate(cols, axis=1)
    o_ref[:, 0:D_B] = _rms(o_b, gb_ref[:, 0:D_B]).astype(o_ref.dtype)

    p = p_ref[...]
    ext_scr[pl.ds(0, POOL_HALO), :] = jnp.where(bi > 0, pprev_ref[...], 0.0)
    ext_scr[pl.ds(POOL_HALO, TM_MIX), :] = p
    ext_scr[pl.ds(POOL_HALO + TM_MIX, POOL_HALO), :] = jnp.where(
        bi < blocks_per_seq - 1, pnext_ref[...], 0.0)
    t = t0 + lax.broadcasted_iota(I32, (TM_MIX, 1), 0)
    outs = []
    for g, w in enumerate(POOL_WINDOWS):
        half = w // 2
        sl = slice(g * CHUNK, (g + 1) * CHUNK)
        acc = None
        for d in range(-half, half):
            part = ext_scr[pl.ds(POOL_HALO + d, TM_MIX), sl]
            acc = part if acc is None else acc + part
        cnt = (jnp.minimum(t + half, seq_len) - jnp.maximum(t - half, 0)).astype(F32)
        pooled = acc / cnt - p[:, sl]
        outs.append(_dot(pooled.astype(BF16), wpool_ref[g]))
    o_c = jnp.concatenate(outs, axis=1) * spool_ref[...]
    o_ref[:, D_B:D_B + D_C] = _rms(o_c, gb_ref[:, D_B:D_B + D_C]).astype(o_ref.dtype)


def _mixers(ugp, g_sgu, w_sgu, b_sgu_t, w_pool, s_pool, gb_bc, layer, seq_len):
    ntok = ugp.shape[0]
    nblk = ntok // TM_MIX
    hb = TM_MIX // POOL_HALO
    n_halo = ntok // POOL_HALO
    blk = lambda part: pl.BlockSpec((TM_MIX, D_B), lambda i: (i, part))
    full = lambda shape: pl.BlockSpec((None,) + shape, lambda i: (layer,) + (0,) * len(shape))
    return pl.pallas_call(
        functools.partial(_mixers_kernel, seq_len=seq_len),
        grid=(nblk,),
        in_specs=[
            blk(0), blk(1), blk(2),
            pl.BlockSpec((POOL_HALO, D_C), lambda i: (jnp.maximum(i * hb - 1, 0), 2)),
            pl.BlockSpec((POOL_HALO, D_C), lambda i: (jnp.minimum((i + 1) * hb, n_halo - 1), 2)),
            full((1, D_B)), full((N_GROUPS_B, CHUNK, CHUNK)), full((CHUNK, N_GROUPS_B)),
            full((len(POOL_WINDOWS), CHUNK, CHUNK)), full((1, D_C)), full((1, D_B + D_C)),
        ],
        out_specs=pl.BlockSpec((TM_MIX, D_B + D_C), lambda i: (i, 0)),
        out_shape=jax.ShapeDtypeStruct((ntok, D_B + D_C), BF16),
        scratch_shapes=[pltpu.VMEM((TM_MIX + 2 * POOL_HALO, D_C), F32)],
        compiler_params=_params(("arbitrary",)),
        name="mixers",
    )(ugp, ugp, ugp, ugp, ugp, g_sgu, w_sgu, b_sgu_t, w_pool, s_pool, gb_bc)


def _out_proj_kernel(x_ref, oa_ref, obc_ref, mod_ref, g_ref, w_ref, wr_ref, br_ref,
                     x1_ref, h_ref, e_ref, gate_ref):
    for m in range(TM_OUT // OUT_CHUNK):
        rows = pl.ds(m * OUT_CHUNK, OUT_CHUNK)
        _out_proj_rows(rows, x_ref, oa_ref, obc_ref, mod_ref, g_ref, w_ref, wr_ref, br_ref,
                       x1_ref, h_ref, e_ref, gate_ref)


def _out_proj_rows(rows, x_ref, oa_ref, obc_ref, mod_ref, g_ref, w_ref, wr_ref, br_ref,
                   x1_ref, h_ref, e_ref, gate_ref):
    merged = jnp.concatenate([oa_ref[rows, :], obc_ref[rows, :]], axis=1)
    mix = _dot(merged, w_ref[...])
    x1 = x_ref[rows, :] + mod_ref[2:3, :] * mix
    x1_ref[rows, :] = x1
    h = _rms(x1, g_ref[...]) * (1.0 + mod_ref[4:5, :]) + mod_ref[3:4, :]
    h_hi = h.astype(BF16)
    h_ref[rows, :] = _pack_bf16_pairs(h_hi)

    h_lo = (h - h_hi.astype(F32)).astype(BF16)
    part = _dot(h_hi, wr_ref[...])
    logits = (part[:, :LANES] + part[:, LANES:] + _dot(h_lo, wr_ref[:, :LANES])) + br_ref[...]
    lane = lax.broadcasted_iota(I32, logits.shape, 1)
    big = jnp.int32(LANES)

    def softmax_masked(mask):
        z = jnp.where(mask, logits, NEG)
        m = jnp.max(z, axis=-1, keepdims=True)
        e = jnp.where(mask, jnp.exp(z - m), 0.0)
        return e / jnp.sum(e, axis=-1, keepdims=True)

    def top1(vals, mask):
        v = jnp.where(mask, vals, -1.0)
        best = jnp.max(v, axis=-1, keepdims=True)
        idx = jnp.min(jnp.where(mask & (v == best), lane, big), axis=-1, keepdims=True)
        return best, idx

    gmask = lane < N_GROUPS_E
    p_grp = softmax_masked(gmask)
    p_g, g_sel = top1(p_grp, gmask)
    lo = N_GROUPS_E + EXPERTS_PER_GROUP * g_sel
    emask = (lane >= lo) & (lane < lo + EXPERTS_PER_GROUP)
    p_exp = softmax_masked(emask)
    p1, i1 = top1(p_exp, emask)
    p2, i2 = top1(p_exp, emask & (lane != i1))
    denom = p1 + p2
    g1 = p_g * p1 / denom
    g2 = p_g * p2 / denom
    e_ref[rows, :] = jnp.where(lane == 0, i1 - N_GROUPS_E,
                               jnp.where(lane == 1, i2 - N_GROUPS_E, 0))
    gate_ref[rows, :] = jnp.where(lane == 0, g1, jnp.where(lane == 1, g2, 0.0))


def _out_proj(x2d, ntok, o_a, o_bc, mod, mod_row_fn, g_ffn, w_out, w_r, b_r, layer):
    nblk = ntok // TM_OUT
    full = lambda shape: pl.BlockSpec((None,) + shape, lambda i: (layer,) + (0,) * len(shape))
    return pl.pallas_call(
        _out_proj_kernel,
        grid=(nblk,),
        in_specs=[
            pl.BlockSpec((TM_OUT, D_MODEL), lambda i: (i, 0)),
            pl.BlockSpec((TM_OUT, D_A), lambda i: (i, 0)),
            pl.BlockSpec((TM_OUT, D_B + D_C), lambda i: (i, 0)),
            pl.BlockSpec((None, None, N_MOD, D_MODEL), lambda i: (layer, mod_row_fn(i), 0, 0)),
            full((1, D_MODEL)), full((D_MODEL, D_MODEL)), full((D_MODEL, 2 * LANES)), full((1, LANES)),
        ],
        out_specs=[
            pl.BlockSpec((TM_OUT, D_MODEL), lambda i: (i, 0)),
            pl.BlockSpec((TM_OUT, D_PACK), lambda i: (i, 0)),
            pl.BlockSpec((TM_OUT, LANES), lambda i: (i, 0)),
            pl.BlockSpec((TM_OUT, LANES), lambda i: (i, 0)),
        ],
        out_shape=[
            jax.ShapeDtypeStruct((ntok, D_MODEL), F32),
            jax.ShapeDtypeStruct((ntok, D_PACK), U32),
            jax.ShapeDtypeStruct((ntok, LANES), I32),
            jax.ShapeDtypeStruct((ntok, LANES), F32),
        ],
        compiler_params=_params(("arbitrary",)),
        name="out_proj_router",
    )(x2d, o_a, o_bc, mod, g_ffn, w_out, w_r, b_r)


def _route_plan(e_flat):
    n_assign = e_flat.shape[0]
    onehot = (e_flat[:, None] == jnp.arange(N_EXPERTS, dtype=I32)[None, :]).astype(I32)
    csum = jnp.cumsum(onehot, axis=0)
    counts = csum[-1]
    rank = jnp.sum(onehot * (csum - 1), axis=1)
    padded = (counts + MOE_BM - 1) // MOE_BM * MOE_BM
    pad_end = jnp.cumsum(padded)
    pad_start = pad_end - padded
    pos = jnp.sum(onehot * pad_start[None, :], axis=1) + rank
    n_blocks = n_assign // MOE_BM + N_EXPERTS
    starts = jnp.arange(n_blocks, dtype=I32) * MOE_BM
    block_e = jnp.minimum(jnp.sum((starts[:, None] >= pad_end[None, :]).astype(I32), axis=1),
                          N_EXPERTS - 1).astype(I32)
    n_used = (pad_end[-1] // MOE_BM).astype(I32).reshape(1)
    fill_start = jnp.where(padded > 0, pad_end - MOE_BM, -1).astype(I32)
    return pos.astype(I32), block_e, n_used, fill_start, n_blocks


def _dispatch_kernel(fill_ref, nu_ref, pos_ref, hp_ref, hs_ref, xs_ref, zero_scr, sem, fill_sem,
                     *, n_ctx_blocks, n_blocks):
    i = pl.program_id(0)

    @pl.when(i == 0)
    def _():
        zero_scr[...] = jnp.zeros_like(zero_scr)

        def fill(start):
            start = pl.multiple_of(start, MOE_BM)
            return pltpu.make_async_copy(zero_scr, xs_ref.at[pl.ds(start, MOE_BM)], fill_sem)

        def expert_fill(e, carry):
            pl.when(fill_ref[e] >= 0)(lambda: fill(fill_ref[e]).start())
            return carry

        def expert_wait(e, carry):
            pl.when(fill_ref[e] >= 0)(lambda: fill(0).wait())
            return carry

        def tail_fill(b, carry):
            fill(b * MOE_BM).start()
            return carry

        def tail_wait(b, carry):
            fill(0).wait()
            return carry

        lax.fori_loop(0, N_EXPERTS, expert_fill, 0)
        lax.fori_loop(nu_ref[0], n_blocks, tail_fill, 0)
        lax.fori_loop(0, N_EXPERTS, expert_wait, 0)
        lax.fori_loop(nu_ref[0], n_blocks, tail_wait, 0)

    def scatter(h_ref):
        def issue(r, carry):
            for k in range(TOP_K):
                slot = pos_ref[0, 0, TOP_K * r + k]
                pltpu.make_async_copy(
                    h_ref.at[pl.ds(r, 1)], xs_ref.at[pl.ds(slot, 1)], sem).start(priority=k)
            return carry

        lax.fori_loop(0, TM_ROW, issue, 0, unroll=8)
        for _ in range(TOP_K):
            pltpu.make_async_copy(h_ref, h_ref, sem).wait()

    pl.when(i < n_ctx_blocks)(lambda: scatter(hp_ref))
    pl.when(i >= n_ctx_blocks)(lambda: scatter(hs_ref))


def _dispatch(fill_start, n_used, pos3, hp, hs, n_blocks):
    nbp = hp.shape[0] // TM_ROW
    nbs = hs.shape[0] // TM_ROW
    grid_spec = pltpu.PrefetchScalarGridSpec(
        num_scalar_prefetch=2,
        grid=(nbp + nbs,),
        in_specs=[
            pl.BlockSpec((1, 1, TOP_K * TM_ROW), lambda i, fs, nu: (i, 0, 0),
                         memory_space=pltpu.SMEM),
            pl.BlockSpec((TM_ROW, D_PACK), lambda i, fs, nu: (jnp.minimum(i, nbp - 1), 0)),
            pl.BlockSpec((TM_ROW, D_PACK), lambda i, fs, nu: (jnp.maximum(i - nbp, 0), 0)),
        ],
        out_specs=pl.BlockSpec(memory_space=pl.ANY),
        scratch_shapes=[pltpu.VMEM((MOE_BM, D_PACK), U32), pltpu.SemaphoreType.DMA(()),
                        pltpu.SemaphoreType.DMA(())],
    )
    return pl.pallas_call(
        functools.partial(_dispatch_kernel, n_ctx_blocks=nbp, n_blocks=n_blocks),
        grid_spec=grid_spec,
        out_shape=jax.ShapeDtypeStruct((n_blocks * MOE_BM, D_PACK), U32),
        compiler_params=_params(("arbitrary",)),
        name="moe_dispatch",
    )(fill_start, n_used, pos3, hp, hs)


def _experts_kernel(be_ref, nu_ref, xs_ref, wg_ref, wu_ref, wd_ref, ys_ref):
    del be_ref
    i = pl.program_id(0)

    @pl.when(i < nu_ref[0])
    def _():
        x = _unpack_bf16_pairs(xs_ref[...], BF16)
        g = _dot(x, wg_ref[...].astype(BF16))
        u = _dot(x, wu_ref[...].astype(BF16))
        a = (g * jax.nn.sigmoid(g) * u).astype(BF16)
        ys_ref[...] = _pack_bf16_pairs(_dot(a, wd_ref[...].astype(BF16)).astype(BF16))

    @pl.when(i >= nu_ref[0])
    def _():
        ys_ref[...] = jnp.zeros_like(ys_ref)


def _experts(block_e, n_used, xs, wg, wu, wd, layer, n_blocks):
    grid_spec = pltpu.PrefetchScalarGridSpec(
        num_scalar_prefetch=2,
        grid=(n_blocks,),
        in_specs=[
            pl.BlockSpec((MOE_BM, D_PACK), lambda i, be, nu: (jnp.minimum(i, nu[0] - 1), 0)),
            pl.BlockSpec((None, None, D_MODEL, D_EXPERT), lambda i, be, nu: (layer, be[i], 0, 0)),
            pl.BlockSpec((None, None, D_MODEL, D_EXPERT), lambda i, be, nu: (layer, be[i], 0, 0)),
            pl.BlockSpec((None, None, D_EXPERT, D_MODEL), lambda i, be, nu: (layer, be[i], 0, 0)),
        ],
        out_specs=pl.BlockSpec((MOE_BM, D_PACK), lambda i, be, nu: (i, 0)),
    )
    return pl.pallas_call(
        _experts_kernel,
        grid_spec=grid_spec,
        out_shape=jax.ShapeDtypeStruct((xs.shape[0], D_PACK), U32),
        compiler_params=_params(("arbitrary",)),
        name="moe_experts",
    )(block_e, n_used, xs, wg, wu, wd)


def _combine_kernel(pos_ref, pos_next_ref, x1_ref, gate_ref, mod_ref, gfin_ref, ys_ref, o_ref,
                    ybuf, sem, *, final_norm):
    i = pl.program_id(0)
    buf = i % 2

    def issue(p_ref, b):
        def body(r, carry):
            for k in range(TOP_K):
                slot = p_ref[0, 0, TOP_K * r + k]
                pltpu.make_async_copy(
                    ys_ref.at[pl.ds(slot, 1)], ybuf.at[b, k, pl.ds(r, 1)], sem.at[b]
                ).start(priority=k)
            return carry
        lax.fori_loop(0, TM_ROW, body, 0, unroll=8)

    @pl.when(i == 0)
    def _():
        issue(pos_ref, 0)

    @pl.when(i + 1 < pl.num_programs(0))
    def _():
        issue(pos_next_ref, 1 - buf)

    for k in range(TOP_K):
        pltpu.make_async_copy(ybuf.at[buf, k], ybuf.at[buf, k], sem.at[buf]).wait()
    y0 = _unpack_bf16_pairs(ybuf[buf, 0], F32)
    y1 = _unpack_bf16_pairs(ybuf[buf, 1], F32)
    f = gate_ref[:, 0:1] * y0 + gate_ref[:, 1:2] * y1
    x2 = x1_ref[...] + mod_ref[5:6, :] * f
    if final_norm:
        x2 = _rms(x2, gfin_ref[...])
    o_ref[...] = x2


def _combine(pos3, x1, gate, mod, mod_row_fn, g_final, ys, layer, final_norm):
    ntok = x1.shape[0]
    nblk = ntok // TM_ROW
    return pl.pallas_call(
        functools.partial(_combine_kernel, final_norm=final_norm),
        grid=(nblk,),
        in_specs=[
            pl.BlockSpec((1, 1, TOP_K * TM_ROW), lambda i: (i, 0, 0), memory_space=pltpu.SMEM),
            pl.BlockSpec((1, 1, TOP_K * TM_ROW), lambda i: (jnp.minimum(i + 1, nblk - 1), 0, 0),
                         memory_space=pltpu.SMEM),
            pl.BlockSpec((TM_ROW, D_MODEL), lambda i: (i, 0)),
            pl.BlockSpec((TM_ROW, LANES), lambda i: (i, 0)),
            pl.BlockSpec((None, None, N_MOD, D_MODEL), lambda i: (layer, mod_row_fn(i), 0, 0)),
            pl.BlockSpec((1, D_MODEL), lambda i: (0, 0)),
            pl.BlockSpec(memory_space=pl.ANY),
        ],
        out_specs=pl.BlockSpec((TM_ROW, D_MODEL), lambda i: (i, 0)),
        out_shape=jax.ShapeDtypeStruct((ntok, D_MODEL), F32),
        scratch_shapes=[pltpu.VMEM((2, TOP_K, TM_ROW, D_PACK), U32),
                        pltpu.SemaphoreType.DMA((2,))],
        compiler_params=_params(("arbitrary",)),
        name="moe_combine",
    )(pos3, pos3, x1, gate, mod, g_final, ys)


def kernel(x_prompt, x_sample, cache_k, cache_v, c, c_ctx, w_mod, b_mod, g_mix, g_ffn, w_in, rpb,
           g_sgu, w_sgu, b_sgu, w_pool, s_pool, g_branch, w_out, w_rg, b_rg, w_re, b_re,
           w_e_gate, w_e_up, w_e_down, g_final):
    n_ctx, seq, _ = x_prompt.shape
    n_lat, lat_seq, _ = x_sample.shape
    ntok_p = n_ctx * seq
    ntok_s = n_lat * lat_seq
    rows = lat_seq // GRID_W

    n_mod_rows = -(-(1 + n_lat) // SUBLANES) * SUBLANES
    c_all = jnp.concatenate(
        [c_ctx[None], c, jnp.zeros((n_mod_rows - 1 - n_lat, D_MODEL), F32)], axis=0)
    mod = _modulation(c_all, w_mod, b_mod).reshape(DEPTH, n_mod_rows, N_MOD, D_MODEL)

    xp = x_prompt.reshape(ntok_p, D_MODEL)
    xs_lat = x_sample.reshape(ntok_s, D_MODEL)

    def lat_mod_row(tm):
        return lambda i: 1 + (i * tm) // lat_seq

    ctx_mod_row = lambda i: 0

    w_in_b = w_in.astype(BF16)
    w_out_b = w_out.astype(BF16)
    w_sgu_b = w_sgu.astype(BF16)
    w_pool_b = w_pool.astype(BF16)

    gmix = g_mix[:, None, :]
    gffn = g_ffn[:, None, :]
    gb_a = g_branch[:, None, :D_A]
    gb_bc = g_branch[:, None, D_A:]
    gsgu = g_sgu[:, None, :]
    bsgu_t = jnp.swapaxes(b_sgu, 1, 2)
    spool = s_pool[:, None, :]
    n_route = N_GROUPS_E + N_EXPERTS
    w_r = jnp.concatenate(
        [w_rg, w_re, jnp.zeros((DEPTH, D_MODEL, LANES - n_route), F32)], axis=2)
    w_r_hi = lax.bitcast_convert_type(
        lax.bitcast_convert_type(w_r, U32) & jnp.uint32(0xFFFF0000), F32)
    w_r = jnp.concatenate([w_r_hi.astype(BF16), (w_r - w_r_hi).astype(BF16)], axis=2)
    b_r = jnp.concatenate(
        [b_rg, b_re, jnp.zeros((DEPTH, LANES - n_route), F32)], axis=1)[:, None, :]
    bias_tab = _local_bias_table(rpb.reshape((DEPTH * N_HEADS,) + rpb.shape[2:]), rows)

    cache_shape = (n_ctx, DEPTH, N_HEADS, seq, HEAD_DIM)
    caches = (jnp.zeros(cache_shape, F32), jnp.zeros(cache_shape, F32))
    for l in range(DEPTH):
        qkv_p, ugp_p, *caches = _in_proj(
            xp, ntok_p, mod, ctx_mod_row, gmix, w_in_b, l, seq, True, tuple(caches))
        oa_p = _ctx_attention(qkv_p, gb_a, l, seq)
        obc_p = _mixers(ugp_p, gsgu, w_sgu_b, bsgu_t, w_pool_b, spool, gb_bc, l, seq)
        x1p, hp, ep, gatep = _out_proj(
            xp, ntok_p, oa_p, obc_p, mod, ctx_mod_row, gffn, w_out_b, w_r, b_r, l)

        q_s, kv_s, ugp_s = _in_proj(
            xs_lat, ntok_s, mod, lat_mod_row(TM_IN), gmix, w_in_b, l, lat_seq, False)
        oa_s = _nbr_attention(q_s, kv_s, cache_k, cache_v, l, bias_tab, gb_a, n_lat, lat_seq)
        obc_s = _mixers(ugp_s, gsgu, w_sgu_b, bsgu_t, w_pool_b, spool, gb_bc, l, lat_seq)
        x1s, hs, es, gates = _out_proj(
            xs_lat, ntok_s, oa_s, obc_s, mod, lat_mod_row(TM_OUT), gffn, w_out_b, w_r, b_r, l)

        e_flat = jnp.concatenate([ep[:, :TOP_K].reshape(-1), es[:, :TOP_K].reshape(-1)])
        pos, block_e, n_used, fill_start, n_blocks = _route_plan(e_flat)
        pos3 = pos.reshape((ntok_p + ntok_s) // TM_ROW, 1, TOP_K * TM_ROW)
        pos_p = pos3[:ntok_p // TM_ROW]
        pos_s = pos3[ntok_p // TM_ROW:]
        slots = _dispatch(fill_start, n_used, pos3, hp, hs, n_blocks)
        ys = _experts(block_e, n_used, slots, w_e_gate, w_e_up, w_e_down, l, n_blocks)
        last = l == DEPTH - 1
        gfin = g_final[None]
        xp = _combine(pos_p, x1p, gatep, mod, ctx_mod_row, gfin, ys, l, last)
        xs_lat = _combine(pos_s, x1s, gates, mod, lat_mod_row(TM_ROW), gfin, ys, l, last)

    y_prompt = xp.reshape(n_ctx, seq, D_MODEL)
    y_sample = xs_lat.reshape(n_lat, lat_seq, D_MODEL)
    return (y_prompt, y_sample, caches[0], caches[1])
```

```python
import functools

import jax
import numpy as np
import jax.numpy as jnp
from jax import lax
from jax.experimental import pallas as pl
from jax.experimental.pallas import tpu as pltpu

F32 = jnp.float32
BF16 = jnp.bfloat16
I32 = jnp.int32
U32 = jnp.uint32

D_MODEL = 2048
DEPTH = 2
GRID_W = 64
HEAD_DIM = 128
D_A = 1024
N_HEADS = 8
WIN_H = 8
WIN_W = 16
D_B = 512
N_GROUPS_B = 4
CHUNK = 128
D_C = 512
POOL_WINDOWS = (2, 4, 8, 16)
IN_WIDTH = 3 * D_A + 2 * D_B + D_C
N_GROUPS_E = 4
EXPERTS_PER_GROUP = 8
N_EXPERTS = 32
TOP_K = 2
D_EXPERT = 512
N_MOD = 6
EPS = 1e-6
NEG = -1e30
LOG2E = 1.4426950408889634
SCALE_LOG2E = HEAD_DIM ** -0.5 * LOG2E

LANES = 128
SUBLANES = 8
VMEM_LIMIT_BYTES = 56 * 1024 * 1024

TM_IN = 1024
TN_IN = 512
IN_CHUNK = 256
OUT_CHUNK = 512
TM_OUT = 512
TM_MIX = 256
Q_ROWS = 8
K_ROWS = 16
MOE_BM = 512
D_PACK = D_MODEL // 2
TM_ROW = 512
POOL_HALO = 8

COL_BLOCKS = ((0, 24, 0), (24, 16, 16), (40, 24, 32))
KEY_COLS = 32


def _params(sem):
    return pltpu.CompilerParams(dimension_semantics=sem, vmem_limit_bytes=VMEM_LIMIT_BYTES)


def _dot(a, b):
    return jnp.dot(a, b, preferred_element_type=F32)


def _dot_nt(a, b):
    return lax.dot_general(a, b, (((1,), (1,)), ((), ())), preferred_element_type=F32)


def _rms(x, g):
    return x * lax.rsqrt(jnp.mean(x * x, axis=-1, keepdims=True) + EPS) * g


def _pack_bf16_pairs(x):
    n = x.shape[1] // 2
    lo = lax.bitcast_convert_type(x[:, :n].astype(F32), U32) >> 16
    hi = lax.bitcast_convert_type(x[:, n:].astype(F32), U32) & jnp.uint32(0xFFFF0000)
    return hi | lo


def _unpack_bf16_pairs(w, dtype):
    lo = lax.bitcast_convert_type(w << 16, F32).astype(dtype)
    hi = lax.bitcast_convert_type(w & jnp.uint32(0xFFFF0000), F32).astype(dtype)
    return jnp.concatenate([lo, hi], axis=1)


def _mod_kernel(c_ref, w_ref, b_ref, o_ref):
    c = c_ref[...]
    s = c * jax.nn.sigmoid(c)
    n = s.shape[0]
    w = w_ref[...]
    s_hi = s.astype(BF16)
    s_lo = (s - s_hi.astype(F32)).astype(BF16)
    w_hi = w.astype(BF16)
    w_lo = (w - w_hi.astype(F32)).astype(BF16)
    r = _dot(jnp.concatenate([s_hi, s_lo], axis=0), w_hi)
    o_ref[...] = r[:n] + r[n:] + _dot(s_hi, w_lo) + b_ref[...]


def _modulation(c_all, w_mod, b_mod):
    tn = 1024
    nrow = c_all.shape[0]
    width = w_mod.shape[2]
    return pl.pallas_call(
        _mod_kernel,
        grid=(DEPTH, width // tn),
        in_specs=[
            pl.BlockSpec((nrow, D_MODEL), lambda l, j: (0, 0)),
            pl.BlockSpec((None, D_MODEL, tn), lambda l, j: (l, 0, j)),
            pl.BlockSpec((None, 1, tn), lambda l, j: (l, 0, j)),
        ],
        out_specs=pl.BlockSpec((None, nrow, tn), lambda l, j: (l, 0, j)),
        out_shape=jax.ShapeDtypeStruct((DEPTH, nrow, width), F32),
        compiler_params=_params(("arbitrary", "arbitrary")),
        name="modulation",
    )(c_all, w_mod, b_mod.reshape(DEPTH, 1, width))


def _in_proj_kernel(x_ref, mod_ref, g_ref, w_ref, *refs, ctx, n_aliased):
    refs = refs[n_aliased:]
    if ctx:
        qkv_ref, ugp_ref, kc_ref, vc_ref, h_scr = refs
    else:
        q_ref, kv_ref, ugp_ref, h_scr = refs
    j = pl.program_id(1)
    heads = TN_IN // HEAD_DIM

    def head_store(ref, dtype):
        def store(m, rows, acc):
            for hh in range(heads):
                ref[hh, rows, :] = acc[:, hh * HEAD_DIM:(hh + 1) * HEAD_DIM].astype(dtype)
        return store

    def cache_store(ref):
        seq = ref.shape[2]
        per_chunk = IN_CHUNK // seq

        def store(m, rows, acc):
            for s in range(per_chunk):
                for hh in range(heads):
                    ref[m * per_chunk + s, hh] = acc[s * seq:(s + 1) * seq,
                                                     hh * HEAD_DIM:(hh + 1) * HEAD_DIM]
        return store

    def ugp_store(m, rows, acc):
        ugp_ref[rows, :] = acc

    def run(normalize, stores):
        for m in range(TM_IN // IN_CHUNK):
            rows = pl.ds(m * IN_CHUNK, IN_CHUNK)
            if normalize:
                h = _rms(x_ref[rows, :], g_ref[...]) * (1.0 + mod_ref[1:2, :]) + mod_ref[0:1, :]
                hb = h.astype(BF16)
                h_scr[rows, :] = hb
            else:
                hb = h_scr[rows, :]
            acc = _dot(hb, w_ref[...])
            for store in stores:
                store(m, rows, acc)

    def case(cond, normalize, stores):
        pl.when(cond)(lambda: run(normalize, stores))

    if ctx:
        case(j == 0, True, [head_store(qkv_ref, BF16)])
        case(j == 1, False, [head_store(qkv_ref, BF16)])
        case((j >= 2) & (j < 4), False, [head_store(qkv_ref, BF16), cache_store(kc_ref)])
        case((j >= 4) & (j < 6), False, [head_store(qkv_ref, BF16), cache_store(vc_ref)])
    else:
        case(j == 0, True, [head_store(q_ref, F32)])
        case(j == 1, False, [head_store(q_ref, F32)])
        case((j >= 2) & (j < 6), False, [head_store(kv_ref, BF16)])
    case(j >= 6, False, [ugp_store])


def _in_proj(x2d, ntok, mod, mod_row_fn, g_mix, w_in, layer, seq_len, ctx, caches=()):
    nblk = ntok // TM_IN
    ncol = IN_WIDTH // TN_IN
    heads = TN_IN // HEAD_DIM
    hm = lambda lo, n: (lambda i, j: (jnp.clip(j - lo, 0, n - 1), i, 0))
    hspec = lambda lo, n: pl.BlockSpec((heads, TM_IN, HEAD_DIM), hm(lo, n))
    ugp_shape = jax.ShapeDtypeStruct((ntok, 2 * D_B + D_C), F32)
    ugp_spec = pl.BlockSpec((TM_IN, TN_IN), lambda i, j: (i, jnp.clip(j - 6, 0, 2)))
    if ctx:
        nseq = ntok // seq_len
        spb = TM_IN // seq_len
        cache_shape = jax.ShapeDtypeStruct((nseq, DEPTH, N_HEADS, seq_len, HEAD_DIM), F32)
        cache_spec = lambda lo: pl.BlockSpec(
            (spb, None, heads, seq_len, HEAD_DIM),
            lambda i, j: (i, layer, jnp.clip(j - lo, 0, 1), 0, 0))
        out_shape = [jax.ShapeDtypeStruct((3 * N_HEADS, ntok, HEAD_DIM), BF16), ugp_shape,
                     cache_shape, cache_shape]
        out_specs = [hspec(0, 6), ugp_spec, cache_spec(2), cache_spec(4)]
    else:
        out_shape = [jax.ShapeDtypeStruct((N_HEADS, ntok, HEAD_DIM), F32),
                     jax.ShapeDtypeStruct((2 * N_HEADS, ntok, HEAD_DIM), BF16), ugp_shape]
        out_specs = [hspec(0, 2), hspec(2, 4), ugp_spec]
    return pl.pallas_call(
        functools.partial(_in_proj_kernel, ctx=ctx, n_aliased=len(caches)),
        grid=(nblk, ncol),
        in_specs=[
            pl.BlockSpec((TM_IN, D_MODEL), lambda i, j: (i, 0)),
            pl.BlockSpec((None, None, N_MOD, D_MODEL), lambda i, j: (layer, mod_row_fn(i), 0, 0)),
            pl.BlockSpec((None, 1, D_MODEL), lambda i, j: (layer, 0, 0)),
            pl.BlockSpec((None, D_MODEL, TN_IN), lambda i, j: (layer, 0, j)),
        ] + [pl.BlockSpec(memory_space=pl.ANY)] * len(caches),
        out_specs=out_specs,
        out_shape=out_shape,
        scratch_shapes=[pltpu.VMEM((TM_IN, D_MODEL), BF16)],
        input_output_aliases={4 + n: 2 + n for n in range(len(caches))},
        compiler_params=_params(("arbitrary", "arbitrary")),
        name="in_proj_ctx" if ctx else "in_proj_lat",
    )(x2d, mod, g_mix, w_in, *caches)


def _write_normed_heads(o_scr, gb_ref, o_ref):
    ss = None
    for h in range(N_HEADS):
        oh = o_scr[h]
        t = jnp.sum(oh * oh, axis=-1, keepdims=True)
        ss = t if ss is None else ss + t
    r = lax.rsqrt(ss * (1.0 / D_A) + EPS)
    for h in range(N_HEADS):
        sl = slice(h * HEAD_DIM, (h + 1) * HEAD_DIM)
        o_ref[:, sl] = (o_scr[h] * r * gb_ref[:, sl]).astype(o_ref.dtype)


def _ctx_attn_kernel(q_ref, k_ref, v_ref, gb_ref, o_ref, o_scr):
    group = 2

    def heads(g, carry):
        hs = [g * group + n for n in range(group)]
        scores = [_dot_nt(q_ref[h], k_ref[h]) * SCALE_LOG2E for h in hs]
        probs = []
        for s in scores:
            p = jnp.exp2(s - jnp.max(s, axis=-1, keepdims=True))
            probs.append((p.astype(BF16), jnp.sum(p, axis=-1, keepdims=True)))
        for h, (p, l) in zip(hs, probs):
            o_scr[h] = _dot(p, v_ref[h]) / l
        return carry

    lax.fori_loop(0, N_HEADS // group, heads, 0)
    _write_normed_heads(o_scr, gb_ref, o_ref)


def _ctx_attention(qkv, gb_a, layer, seq_len):
    ntok = qkv.shape[1]
    spec = lambda part: pl.BlockSpec((N_HEADS, seq_len, HEAD_DIM), lambda b: (part, b, 0))
    return pl.pallas_call(
        _ctx_attn_kernel,
        grid=(ntok // seq_len,),
        in_specs=[spec(0), spec(1), spec(2),
                  pl.BlockSpec((None, 1, D_A), lambda b: (layer, 0, 0))],
        out_specs=pl.BlockSpec((seq_len, D_A), lambda b: (b, 0)),
        out_shape=jax.ShapeDtypeStruct((ntok, D_A), BF16),
        scratch_shapes=[pltpu.VMEM((N_HEADS, seq_len, HEAD_DIM), F32)],
        compiler_params=_params(("arbitrary",)),
        name="ctx_attention",
    )(qkv, qkv, qkv, gb_a)


def _local_bias_table(rpb, rows):
    n_rb = rows // Q_ROWS
    exact = lax.Precision.HIGHEST
    n_dr = 2 * WIN_H - 1
    n_dc = 2 * WIN_W - 1
    sel_r, valid_r = [], []
    for jb in (0, 1, n_rb - 1):
        r0 = Q_ROWS * jb
        ks = min(max(r0 - WIN_H // 2, 0), rows - K_ROWS)
        r = r0 + np.arange(Q_ROWS)[:, None]
        kr = ks + np.arange(K_ROWS)[None, :]
        rs = np.clip(r - WIN_H // 2, 0, rows - WIN_H)
        valid_r.append((kr >= rs) & (kr < rs + WIN_H))
        ri = np.clip(kr - r + WIN_H - 1, 0, n_dr - 1)
        sel_r.append(ri[..., None] == np.arange(n_dr))
    sel_r = np.stack(sel_r).astype(np.float32)
    valid_r = np.stack(valid_r)
    kc0 = np.concatenate([np.full(nc, k0) for (_, nc, k0) in COL_BLOCKS])
    c = np.arange(GRID_W)[:, None]
    kc = kc0[:, None] + np.arange(KEY_COLS)[None, :]
    cs = np.clip(c - WIN_W // 2, 0, GRID_W - WIN_W)
    valid_c = (kc >= cs) & (kc < cs + WIN_W)
    ci = np.clip(kc - c + WIN_W - 1, 0, n_dc - 1)
    sel_c = (ci[..., None] == np.arange(n_dc)).astype(np.float32)
    by_row = jnp.einsum("trka,hab->htrkb", sel_r, rpb, precision=exact)
    b = jnp.einsum("htrkb,cjb->htrckj", by_row, sel_c, precision=exact)
    valid = valid_r[:, :, None, :, None] & valid_c[None, None, :, None, :]
    b = jnp.where(valid[None], b * LOG2E, NEG)
    return b.reshape(rpb.shape[0], 3, Q_ROWS * GRID_W, K_ROWS * KEY_COLS)


def _nbr_attn_kernel(q_ref, k0, k1, k2, k3, v0, v1, v2, v3, ck_ref, cv_ref, bias_ref, gb_ref,
                     o_ref, o_scr):
    k_refs = (k0, k1, k2, k3)
    v_refs = (v0, v1, v2, v3)
    rows_per_ref = K_ROWS // len(k_refs)

    def window(refs, h, kc0):
        parts = []
        for kr in range(K_ROWS):
            ref = refs[kr // rows_per_ref]
            parts.append(ref[h, pl.ds((kr % rows_per_ref) * GRID_W + kc0, KEY_COLS), :])
        return jnp.concatenate(parts, axis=0)

    def query_rows(ref, h, c0, nc):
        return jnp.concatenate(
            [ref[h, pl.ds(qr * GRID_W + c0, nc), :] for qr in range(Q_ROWS)], axis=0)

    def head(h, carry):
        kc = ck_ref[h].astype(BF16)
        vc = cv_ref[h].astype(BF16)
        scores = []
        for (c0, nc, kc0) in COL_BLOCKS:
            qi = query_rows(q_ref, h, c0, nc).astype(BF16)
            ki = window(k_refs, h, kc0)
            s_loc = _dot_nt(qi, ki) * SCALE_LOG2E + query_rows(bias_ref, h, c0, nc)
            s_ctx = _dot_nt(qi, kc) * SCALE_LOG2E
            scores.append((s_loc, s_ctx))
        probs = []
        for s_loc, s_ctx in scores:
            m = jnp.maximum(jnp.max(s_loc, axis=-1, keepdims=True),
                            jnp.max(s_ctx, axis=-1, keepdims=True))
            p_loc = jnp.exp2(s_loc - m)
            p_ctx = jnp.exp2(s_ctx - m)
            l = jnp.sum(p_loc, axis=-1, keepdims=True) + jnp.sum(p_ctx, axis=-1, keepdims=True)
            probs.append((p_loc.astype(BF16), p_ctx.astype(BF16), l))
        for (c0, nc, kc0), (p_loc, p_ctx, l) in zip(COL_BLOCKS, probs):
            vi = window(v_refs, h, kc0)
            o = (_dot(p_loc, vi) + _dot(p_ctx, vc)) / l
            for qr in range(Q_ROWS):
                o_scr[h, pl.ds(qr * GRID_W + c0, nc), :] = o[qr * nc:(qr + 1) * nc]
        return carry

    lax.fori_loop(0, N_HEADS, head, 0)
    _write_normed_heads(o_scr, gb_ref, o_ref)


def _nbr_attention(q, kv, cache_k, cache_v, layer, bias_tab, gb_a, n_req, seq_len):
    rows = seq_len // GRID_W
    n_rb = rows // Q_ROWS
    tq = Q_ROWS * GRID_W
    n_kref = 4
    tk = (K_ROWS // n_kref) * GRID_W
    kblk_per_req = seq_len // tk
    ctx_len = cache_k.shape[3]

    def kmap(part, m):
        def f(b, jb):
            start = jnp.clip(2 * jb - 1, 0, kblk_per_req - n_kref)
            return (part, b * kblk_per_req + start + m, 0)
        return f

    kspecs = [pl.BlockSpec((N_HEADS, tk, HEAD_DIM), kmap(0, m)) for m in range(n_kref)]
    vspecs = [pl.BlockSpec((N_HEADS, tk, HEAD_DIM), kmap(1, m)) for m in range(n_kref)]
    cspec = pl.BlockSpec((None, None, N_HEADS, ctx_len, HEAD_DIM), lambda b, jb: (b, layer, 0, 0, 0))
    kind = lambda b, jb: (layer, (jb > 0).astype(I32) + (jb == n_rb - 1).astype(I32), 0, 0)
    return pl.pallas_call(
        _nbr_attn_kernel,
        grid=(n_req, n_rb),
        in_specs=[pl.BlockSpec((N_HEADS, tq, HEAD_DIM), lambda b, jb: (0, b * n_rb + jb, 0))]
        + kspecs + vspecs + [
            cspec, cspec,
            pl.BlockSpec((N_HEADS, None, tq, K_ROWS * KEY_COLS), kind),
            pl.BlockSpec((None, 1, D_A), lambda b, jb: (layer, 0, 0)),
        ],
        out_specs=pl.BlockSpec((tq, D_A), lambda b, jb: (b * n_rb + jb, 0)),
        out_shape=jax.ShapeDtypeStruct((n_req * seq_len, D_A), BF16),
        scratch_shapes=[pltpu.VMEM((N_HEADS, tq, HEAD_DIM), F32)],
        compiler_params=_params(("arbitrary", "arbitrary")),
        name="nbr_attention",
    )(q, *([kv] * (2 * n_kref)), cache_k, cache_v, bias_tab, gb_a)


def _mixers_kernel(u_ref, gv_ref, p_ref, pprev_ref, pnext_ref, gsgu_ref, wsgu_ref, bsgu_ref,
                   wpool_ref, spool_ref, gb_ref, o_ref, ext_scr, *, seq_len):
    i = pl.program_id(0)
    blocks_per_seq = seq_len // TM_MIX
    bi = i % blocks_per_seq
    t0 = bi * TM_MIX

    gu = jax.nn.gelu(u_ref[...])
    gg = _rms(jax.nn.gelu(gv_ref[...]), gsgu_ref[...]).astype(BF16)
    cols = []
    for g in range(N_GROUPS_B):
        sl = slice(g * CHUNK, (g + 1) * CHUNK)
        chunks = []
        for n in range(TM_MIX // CHUNK):
            rows = slice(n * CHUNK, (n + 1) * CHUNK)
            chunks.append(_dot(wsgu_ref[g], gg[rows, sl]) + bsgu_ref[:, g:g + 1])
        cols.append(jnp.concatenate(chunks, axis=0))
    o_b = gu * jnp.concatenate(cols, axis=1)
    o_ref[:, 0:D_B] = _rms(o_b, gb_ref[:, 0:D_B]).astype(o_ref.dtype)

    p = p_ref[...]
    ext_scr[pl.ds(0, POOL_HALO), :] = jnp.where(bi > 0, pprev_ref[...], 0.0)
    ext_scr[pl.ds(POOL_HALO, TM_MIX), :] = p
    ext_scr[pl.ds(POOL_HALO + TM_MIX, POOL_HALO), :] = jnp.where(
        bi < blocks_per_seq - 1, pnext_ref[...], 0.0)
    t = t0 + lax.broadcasted_iota(I32, (TM_MIX, 1), 0)
    outs = []
    for g, w in enumerate(POOL_WINDOWS):
        half = w // 2
        sl = slice(g * CHUNK, (g + 1) * CHUNK)
        acc = None
        for d in range(-half, half):
            part = ext_scr[pl.ds(POOL_HALO + d, TM_MIX), sl]
            acc = part if acc is None else acc + part
        cnt = (jnp.minimum(t + half, seq_len) - jnp.maximum(t - half, 0)).astype(F32)
        pooled = acc / cnt - p[:, sl]
        outs.append(_dot(pooled.astype(BF16), wpool_ref[g]))
    o_c = jnp.concatenate(outs, axis=1) * spool_ref[...]
    o_ref[:, D_B:D_B + D_C] = _rms(o_c, gb_ref[:, D_B:D_B + D_C]).astype(o_ref.dtype)


def _mixers(ugp, g_sgu, w_sgu, b_sgu_t, w_pool, s_pool, gb_bc, layer, seq_len):
    ntok = ugp.shape[0]
    nblk = ntok // TM_MIX
    hb = TM_MIX // POOL_HALO
    n_halo = ntok // POOL_HALO
    blk = lambda part: pl.BlockSpec((TM_MIX, D_B), lambda i: (i, part))
    full = lambda shape: pl.BlockSpec((None,) + shape, lambda i: (layer,) + (0,) * len(shape))
    return pl.pallas_call(
        functools.partial(_mixers_kernel, seq_len=seq_len),
        grid=(nblk,),
        in_specs=[
            blk(0), blk(1), blk(2),
            pl.BlockSpec((POOL_HALO, D_C), lambda i: (jnp.maximum(i * hb - 1, 0), 2)),
            pl.BlockSpec((POOL_HALO, D_C), lambda i: (jnp.minimum((i + 1) * hb, n_halo - 1), 2)),
            full((1, D_B)), full((N_GROUPS_B, CHUNK, CHUNK)), full((CHUNK, N_GROUPS_B)),
            full((len(POOL_WINDOWS), CHUNK, CHUNK)), full((1, D_C)), full((1, D_B + D_C)),
        ],
        out_specs=pl.BlockSpec((TM_MIX, D_B + D_C), lambda i: (i, 0)),
        out_shape=jax.ShapeDtypeStruct((ntok, D_B + D_C), BF16),
        scratch_shapes=[pltpu.VMEM((TM_MIX + 2 * POOL_HALO, D_C), F32)],
        compiler_params=_params(("arbitrary",)),
        name="mixers",
    )(ugp, ugp, ugp, ugp, ugp, g_sgu, w_sgu, b_sgu_t, w_pool, s_pool, gb_bc)


def _out_proj_kernel(x_ref, oa_ref, obc_ref, mod_ref, g_ref, w_ref, wr_ref, br_ref,
                     x1_ref, h_ref, e_ref, gate_ref):
    chunks = [pl.ds(m * OUT_CHUNK, OUT_CHUNK) for m in range(TM_OUT // OUT_CHUNK)]
    mixes = [_dot(jnp.concatenate([oa_ref[rows, :], obc_ref[rows, :]], axis=1), w_ref[...])
             for rows in chunks]
    splits = []
    for rows, mix in zip(chunks, mixes):
        x1 = x_ref[rows, :] + mod_ref[2:3, :] * mix
        x1_ref[rows, :] = x1
        h = _rms(x1, g_ref[...]) * (1.0 + mod_ref[4:5, :]) + mod_ref[3:4, :]
        h_hi = h.astype(BF16)
        h_ref[rows, :] = _pack_bf16_pairs(h_hi)
        splits.append((h_hi, (h - h_hi.astype(F32)).astype(BF16)))
    logits = []
    for h_hi, h_lo in splits:
        part = _dot(h_hi, wr_ref[...])
        logits.append((part[:, :LANES] + part[:, LANES:] + _dot(h_lo, wr_ref[:, :LANES]))
                      + br_ref[...])
    for rows, lg in zip(chunks, logits):
        _route_rows(rows, lg, e_ref, gate_ref)


def _route_rows(rows, logits, e_ref, gate_ref):
    lane = lax.broadcasted_iota(I32, logits.shape, 1)
    big = jnp.int32(LANES)

    def softmax_masked(mask):
        z = jnp.where(mask, logits, NEG)
        m = jnp.max(z, axis=-1, keepdims=True)
        e = jnp.where(mask, jnp.exp(z - m), 0.0)
        return e / jnp.sum(e, axis=-1, keepdims=True)

    def top1(vals, mask):
        v = jnp.where(mask, vals, -1.0)
        best = jnp.max(v, axis=-1, keepdims=True)
        idx = jnp.min(jnp.where(mask & (v == best), lane, big), axis=-1, keepdims=True)
        return best, idx

    gmask = lane < N_GROUPS_E
    p_grp = softmax_masked(gmask)
    p_g, g_sel = top1(p_grp, gmask)
    lo = N_GROUPS_E + EXPERTS_PER_GROUP * g_sel
    emask = (lane >= lo) & (lane < lo + EXPERTS_PER_GROUP)
    p_exp = softmax_masked(emask)
    p1, i1 = top1(p_exp, emask)
    p2, i2 = top1(p_exp, emask & (lane != i1))
    denom = p1 + p2
    g1 = p_g * p1 / denom
    g2 = p_g * p2 / denom
    e_ref[rows, :] = jnp.where(lane == 0, i1 - N_GROUPS_E,
                               jnp.where(lane == 1, i2 - N_GROUPS_E, 0))
    gate_ref[rows, :] = jnp.where(lane == 0, g1, jnp.where(lane == 1, g2, 0.0))


def _out_proj(x2d, ntok, o_a, o_bc, mod, mod_row_fn, g_ffn, w_out, w_r, b_r, layer):
    nblk = ntok // TM_OUT
    full = lambda shape: pl.BlockSpec((None,) + shape, lambda i: (layer,) + (0,) * len(shape))
    return pl.pallas_call(
        _out_proj_kernel,
        grid=(nblk,),
        in_specs=[
            pl.BlockSpec((TM_OUT, D_MODEL), lambda i: (i, 0)),
            pl.BlockSpec((TM_OUT, D_A), lambda i: (i, 0)),
            pl.BlockSpec((TM_OUT, D_B + D_C), lambda i: (i, 0)),
            pl.BlockSpec((None, None, N_MOD, D_MODEL), lambda i: (layer, mod_row_fn(i), 0, 0)),
            full((1, D_MODEL)), full((D_MODEL, D_MODEL)), full((D_MODEL, 2 * LANES)), full((1, LANES)),
        ],
        out_specs=[
            pl.BlockSpec((TM_OUT, D_MODEL), lambda i: (i, 0)),
            pl.BlockSpec((TM_OUT, D_PACK), lambda i: (i, 0)),
            pl.BlockSpec((TM_OUT, LANES), lambda i: (i, 0)),
            pl.BlockSpec((TM_OUT, LANES), lambda i: (i, 0)),
        ],
        out_shape=[
            jax.ShapeDtypeStruct((ntok, D_MODEL), F32),
            jax.ShapeDtypeStruct((ntok, D_PACK), U32),
            jax.ShapeDtypeStruct((ntok, LANES), I32),
            jax.ShapeDtypeStruct((ntok, LANES), F32),
        ],
        compiler_params=_params(("arbitrary",)),
        name="out_proj_router",
    )(x2d, o_a, o_bc, mod, g_ffn, w_out, w_r, b_r)


def _route_plan(e_flat):
    n_assign = e_flat.shape[0]
    onehot = (e_flat[:, None] == jnp.arange(N_EXPERTS, dtype=I32)[None, :]).astype(I32)
    csum = jnp.cumsum(onehot, axis=0)
    counts = csum[-1]
    rank = jnp.sum(onehot * (csum - 1), axis=1)
    padded = (counts + MOE_BM - 1) // MOE_BM * MOE_BM
    pad_end = jnp.cumsum(padded)
    pad_start = pad_end - padded
    pos = jnp.sum(onehot * pad_start[None, :], axis=1) + rank
    n_blocks = n_assign // MOE_BM + N_EXPERTS
    starts = jnp.arange(n_blocks, dtype=I32) * MOE_BM
    block_e = jnp.minimum(jnp.sum((starts[:, None] >= pad_end[None, :]).astype(I32), axis=1),
                          N_EXPERTS - 1).astype(I32)
    n_used = (pad_end[-1] // MOE_BM).astype(I32).reshape(1)
    fill_start = jnp.where(padded > 0, pad_end - MOE_BM, -1).astype(I32)
    return pos.astype(I32), block_e, n_used, fill_start, n_blocks


def _dispatch_kernel(fill_ref, nu_ref, pos_ref, hp_ref, hs_ref, xs_ref, zero_scr, sem, fill_sem,
                     *, n_ctx_blocks, n_blocks):
    i = pl.program_id(0)

    @pl.when(i == 0)
    def _():
        zero_scr[...] = jnp.zeros_like(zero_scr)

        def fill(start):
            start = pl.multiple_of(start, MOE_BM)
            return pltpu.make_async_copy(zero_scr, xs_ref.at[pl.ds(start, MOE_BM)], fill_sem)

        def expert_fill(e, carry):
            pl.when(fill_ref[e] >= 0)(lambda: fill(fill_ref[e]).start())
            return carry

        def expert_wait(e, carry):
            pl.when(fill_ref[e] >= 0)(lambda: fill(0).wait())
            return carry

        def tail_fill(b, carry):
            fill(b * MOE_BM).start()
            return carry

        def tail_wait(b, carry):
            fill(0).wait()
            return carry

        lax.fori_loop(0, N_EXPERTS, expert_fill, 0)
        lax.fori_loop(nu_ref[0], n_blocks, tail_fill, 0)
        lax.fori_loop(0, N_EXPERTS, expert_wait, 0)
        lax.fori_loop(nu_ref[0], n_blocks, tail_wait, 0)

    def scatter(h_ref):
        def issue(r, carry):
            for k in range(TOP_K):
                slot = pos_ref[0, 0, TOP_K * r + k]
                pltpu.make_async_copy(
                    h_ref.at[pl.ds(r, 1)], xs_ref.at[pl.ds(slot, 1)], sem).start(priority=k)
            return carry

        lax.fori_loop(0, TM_ROW, issue, 0, unroll=8)
        for _ in range(TOP_K):
            pltpu.make_async_copy(h_ref, h_ref, sem).wait()

    pl.when(i < n_ctx_blocks)(lambda: scatter(hp_ref))
    pl.when(i >= n_ctx_blocks)(lambda: scatter(hs_ref))


def _dispatch(fill_start, n_used, pos3, hp, hs, n_blocks):
    nbp = hp.shape[0] // TM_ROW
    nbs = hs.shape[0] // TM_ROW
    grid_spec = pltpu.PrefetchScalarGridSpec(
        num_scalar_prefetch=2,
        grid=(nbp + nbs,),
        in_specs=[
            pl.BlockSpec((1, 1, TOP_K * TM_ROW), lambda i, fs, nu: (i, 0, 0),
                         memory_space=pltpu.SMEM),
            pl.BlockSpec((TM_ROW, D_PACK), lambda i, fs, nu: (jnp.minimum(i, nbp - 1), 0)),
            pl.BlockSpec((TM_ROW, D_PACK), lambda i, fs, nu: (jnp.maximum(i - nbp, 0), 0)),
        ],
        out_specs=pl.BlockSpec(memory_space=pl.ANY),
        scratch_shapes=[pltpu.VMEM((MOE_BM, D_PACK), U32), pltpu.SemaphoreType.DMA(()),
                        pltpu.SemaphoreType.DMA(())],
    )
    return pl.pallas_call(
        functools.partial(_dispatch_kernel, n_ctx_blocks=nbp, n_blocks=n_blocks),
        grid_spec=grid_spec,
        out_shape=jax.ShapeDtypeStruct((n_blocks * MOE_BM, D_PACK), U32),
        compiler_params=_params(("arbitrary",)),
        name="moe_dispatch",
    )(fill_start, n_used, pos3, hp, hs)


def _experts_kernel(be_ref, nu_ref, xs_ref, wg_ref, wu_ref, wd_ref, ys_ref):
    del be_ref
    i = pl.program_id(0)

    @pl.when(i < nu_ref[0])
    def _():
        x = _unpack_bf16_pairs(xs_ref[...], BF16)
        g = _dot(x, wg_ref[...].astype(BF16))
        u = _dot(x, wu_ref[...].astype(BF16))
        a = (g * jax.nn.sigmoid(g) * u).astype(BF16)
        ys_ref[...] = _pack_bf16_pairs(_dot(a, wd_ref[...].astype(BF16)).astype(BF16))

    @pl.when(i >= nu_ref[0])
    def _():
        ys_ref[...] = jnp.zeros_like(ys_ref)


def _experts(block_e, n_used, xs, wg, wu, wd, layer, n_blocks):
    grid_spec = pltpu.PrefetchScalarGridSpec(
        num_scalar_prefetch=2,
        grid=(n_blocks,),
        in_specs=[
            pl.BlockSpec((MOE_BM, D_PACK), lambda i, be, nu: (jnp.minimum(i, nu[0] - 1), 0)),
            pl.BlockSpec((None, None, D_MODEL, D_EXPERT), lambda i, be, nu: (layer, be[i], 0, 0)),
            pl.BlockSpec((None, None, D_MODEL, D_EXPERT), lambda i, be, nu: (layer, be[i], 0, 0)),
            pl.BlockSpec((None, None, D_EXPERT, D_MODEL), lambda i, be, nu: (layer, be[i], 0, 0)),
        ],
        out_specs=pl.BlockSpec((MOE_BM, D_PACK), lambda i, be, nu: (i, 0)),
    )
    return pl.pallas_call(
        _experts_kernel,
        grid_spec=grid_spec,
        out_shape=jax.ShapeDtypeStruct((xs.shape[0], D_PACK), U32),
        compiler_params=_params(("arbitrary",)),
        name="moe_experts",
    )(block_e, n_used, xs, wg, wu, wd)


def _combine_kernel(pos_ref, pos_next_ref, x1_ref, gate_ref, mod_ref, gfin_ref, ys_ref, o_ref,
                    ybuf, sem, *, final_norm):
    i = pl.program_id(0)
    buf = i % 2

    def issue(p_ref, b):
        def body(r, carry):
            for k in range(TOP_K):
                slot = p_ref[0, 0, TOP_K * r + k]
                pltpu.make_async_copy(
                    ys_ref.at[pl.ds(slot, 1)], ybuf.at[b, k, pl.ds(r, 1)], sem.at[b]
                ).start(priority=k)
            return carry
        lax.fori_loop(0, TM_ROW, body, 0, unroll=8)

    @pl.when(i == 0)
    def _():
        issue(pos_ref, 0)

    @pl.when(i + 1 < pl.num_programs(0))
    def _():
        issue(pos_next_ref, 1 - buf)

    for k in range(TOP_K):
        pltpu.make_async_copy(ybuf.at[buf, k], ybuf.at[buf, k], sem.at[buf]).wait()
    y0 = _unpack_bf16_pairs(ybuf[buf, 0], F32)
    y1 = _unpack_bf16_pairs(ybuf[buf, 1], F32)
    f = gate_ref[:, 0:1] * y0 + gate_ref[:, 1:2] * y1
    x2 = x1_ref[...] + mod_ref[5:6, :] * f
    if final_norm:
        x2 = _rms(x2, gfin_ref[...])
    o_ref[...] = x2


def _combine(pos3, x1, gate, mod, mod_row_fn, g_final, ys, layer, final_norm):
    ntok = x1.shape[0]
    nblk = ntok // TM_ROW
    return pl.pallas_call(
        functools.partial(_combine_kernel, final_norm=final_norm),
        grid=(nblk,),
        in_specs=[
            pl.BlockSpec((1, 1, TOP_K * TM_ROW), lambda i: (i, 0, 0), memory_space=pltpu.SMEM),
            pl.BlockSpec((1, 1, TOP_K * TM_ROW), lambda i: (jnp.minimum(i + 1, nblk - 1), 0, 0),
                         memory_space=pltpu.SMEM),
            pl.BlockSpec((TM_ROW, D_MODEL), lambda i: (i, 0)),
            pl.BlockSpec((TM_ROW, LANES), lambda i: (i, 0)),
            pl.BlockSpec((None, None, N_MOD, D_MODEL), lambda i: (layer, mod_row_fn(i), 0, 0)),
            pl.BlockSpec((1, D_MODEL), lambda i: (0, 0)),
            pl.BlockSpec(memory_space=pl.ANY),
        ],
        out_specs=pl.BlockSpec((TM_ROW, D_MODEL), lambda i: (i, 0)),
        out_shape=jax.ShapeDtypeStruct((ntok, D_MODEL), F32),
        scratch_shapes=[pltpu.VMEM((2, TOP_K, TM_ROW, D_PACK), U32),
                        pltpu.SemaphoreType.DMA((2,))],
        compiler_params=_params(("arbitrary",)),
        name="moe_combine",
    )(pos3, pos3, x1, gate, mod, g_final, ys)


def kernel(x_prompt, x_sample, cache_k, cache_v, c, c_ctx, w_mod, b_mod, g_mix, g_ffn, w_in, rpb,
           g_sgu, w_sgu, b_sgu, w_pool, s_pool, g_branch, w_out, w_rg, b_rg, w_re, b_re,
           w_e_gate, w_e_up, w_e_down, g_final):
    n_ctx, seq, _ = x_prompt.shape
    n_lat, lat_seq, _ = x_sample.shape
    ntok_p = n_ctx * seq
    ntok_s = n_lat * lat_seq
    rows = lat_seq // GRID_W

    n_mod_rows = -(-(1 + n_lat) // SUBLANES) * SUBLANES
    c_all = jnp.concatenate(
        [c_ctx[None], c, jnp.zeros((n_mod_rows - 1 - n_lat, D_MODEL), F32)], axis=0)
    mod = _modulation(c_all, w_mod, b_mod).reshape(DEPTH, n_mod_rows, N_MOD, D_MODEL)

    xp = x_prompt.reshape(ntok_p, D_MODEL)
    xs_lat = x_sample.reshape(ntok_s, D_MODEL)

    def lat_mod_row(tm):
        return lambda i: 1 + (i * tm) // lat_seq

    ctx_mod_row = lambda i: 0

    w_in_b = w_in.astype(BF16)
    w_out_b = w_out.astype(BF16)
    w_sgu_b = w_sgu.astype(BF16)
    w_pool_b = w_pool.astype(BF16)

    gmix = g_mix[:, None, :]
    gffn = g_ffn[:, None, :]
    gb_a = g_branch[:, None, :D_A]
    gb_bc = g_branch[:, None, D_A:]
    gsgu = g_sgu[:, None, :]
    bsgu_t = jnp.swapaxes(b_sgu, 1, 2)
    spool = s_pool[:, None, :]
    n_route = N_GROUPS_E + N_EXPERTS
    w_r = jnp.concatenate(
        [w_rg, w_re, jnp.zeros((DEPTH, D_MODEL, LANES - n_route), F32)], axis=2)
    w_r_hi = lax.bitcast_convert_type(
        lax.bitcast_convert_type(w_r, U32) & jnp.uint32(0xFFFF0000), F32)
    w_r = jnp.concatenate([w_r_hi.astype(BF16), (w_r - w_r_hi).astype(BF16)], axis=2)
    b_r = jnp.concatenate(
        [b_rg, b_re, jnp.zeros((DEPTH, LANES - n_route), F32)], axis=1)[:, None, :]
    bias_tab = _local_bias_table(rpb.reshape((DEPTH * N_HEADS,) + rpb.shape[2:]), rows)

    cache_shape = (n_ctx, DEPTH, N_HEADS, seq, HEAD_DIM)
    caches = (jnp.zeros(cache_shape, F32), jnp.zeros(cache_shape, F32))
    for l in range(DEPTH):
        qkv_p, ugp_p, *caches = _in_proj(
            xp, ntok_p, mod, ctx_mod_row, gmix, w_in_b, l, seq, True, tuple(caches))
        oa_p = _ctx_attention(qkv_p, gb_a, l, seq)
        obc_p = _mixers(ugp_p, gsgu, w_sgu_b, bsgu_t, w_pool_b, spool, gb_bc, l, seq)
        x1p, hp, ep, gatep = _out_proj(
            xp, ntok_p, oa_p, obc_p, mod, ctx_mod_row, gffn, w_out_b, w_r, b_r, l)

        q_s, kv_s, ugp_s = _in_proj(
            xs_lat, ntok_s, mod, lat_mod_row(TM_IN), gmix, w_in_b, l, lat_seq, False)
        oa_s = _nbr_attention(q_s, kv_s, cache_k, cache_v, l, bias_tab, gb_a, n_lat, lat_seq)
        obc_s = _mixers(ugp_s, gsgu, w_sgu_b, bsgu_t, w_pool_b, spool, gb_bc, l, lat_seq)
        x1s, hs, es, gates = _out_proj(
            xs_lat, ntok_s, oa_s, obc_s, mod, lat_mod_row(TM_OUT), gffn, w_out_b, w_r, b_r, l)

        e_flat = jnp.concatenate([ep[:, :TOP_K].reshape(-1), es[:, :TOP_K].reshape(-1)])
        pos, block_e, n_used, fill_start, n_blocks = _route_plan(e_flat)
        pos3 = pos.reshape((ntok_p + ntok_s) // TM_ROW, 1, TOP_K * TM_ROW)
        pos_p = pos3[:ntok_p // TM_ROW]
        pos_s = pos3[ntok_p // TM_ROW:]
        slots = _dispatch(fill_start, n_used, pos3, hp, hs, n_blocks)
        ys = _experts(block_e, n_used, slots, w_e_gate, w_e_up, w_e_down, l, n_blocks)
        last = l == DEPTH - 1
        gfin = g_final[None]
        xp = _combine(pos_p, x1p, gatep, mod, ctx_mod_row, gfin, ys, l, last)
        xs_lat = _combine(pos_s, x1s, gates, mod, lat_mod_row(TM_ROW), gfin, ys, l, last)

    y_prompt = xp.reshape(n_ctx, seq, D_MODEL)
    y_sample = xs_lat.reshape(n_lat, lat_seq, D_MODEL)
    return (y_prompt, y_sample, caches[0], caches[1])
```

```python
import functools

import jax
import numpy as np
import jax.numpy as jnp
from jax import lax
from jax.experimental import pallas as pl
from jax.experimental.pallas import tpu as pltpu

F32 = jnp.float32
BF16 = jnp.bfloat16
I32 = jnp.int32
U32 = jnp.uint32

D_MODEL = 2048
DEPTH = 2
GRID_W = 64
HEAD_DIM = 128
D_A = 1024
N_HEADS = 8
WIN_H = 8
WIN_W = 16
D_B = 512
N_GROUPS_B = 4
CHUNK = 128
D_C = 512
POOL_WINDOWS = (2, 4, 8, 16)
IN_WIDTH = 3 * D_A + 2 * D_B + D_C
N_GROUPS_E = 4
EXPERTS_PER_GROUP = 8
N_EXPERTS = 32
TOP_K = 2
D_EXPERT = 512
N_MOD = 6
EPS = 1e-6
NEG = -1e30
LOG2E = 1.4426950408889634
SCALE_LOG2E = HEAD_DIM ** -0.5 * LOG2E

LANES = 128
SUBLANES = 8
VMEM_LIMIT_BYTES = 56 * 1024 * 1024

TM_IN = 1024
TN_IN = 512
IN_CHUNK = 256
OUT_CHUNK = 512
TM_OUT = 512
TM_MIX = 256
Q_ROWS = 8
K_ROWS = 16
MOE_BM = 512
D_PACK = D_MODEL // 2
ROW_TILE = (SUBLANES, LANES)
assert D_PACK == SUBLANES * LANES
TM_ROW = 512
POOL_HALO = 8

COL_BLOCKS = ((0, 24, 0), (24, 16, 16), (40, 24, 32))
KEY_COLS = 32


def _params(sem):
    return pltpu.CompilerParams(dimension_semantics=sem, vmem_limit_bytes=VMEM_LIMIT_BYTES)


def _dot(a, b):
    return jnp.dot(a, b, preferred_element_type=F32)


def _dot_nt(a, b):
    return lax.dot_general(a, b, (((1,), (1,)), ((), ())), preferred_element_type=F32)


def _rms(x, g):
    return x * lax.rsqrt(jnp.mean(x * x, axis=-1, keepdims=True) + EPS) * g


def _pack_bf16_pairs(x):
    n = x.shape[1] // 2
    lo = lax.bitcast_convert_type(x[:, :n].astype(F32), U32) >> 16
    hi = lax.bitcast_convert_type(x[:, n:].astype(F32), U32) & jnp.uint32(0xFFFF0000)
    return hi | lo


def _unpack_bf16_pairs(w, dtype):
    lo = lax.bitcast_convert_type(w << 16, F32).astype(dtype)
    hi = lax.bitcast_convert_type(w & jnp.uint32(0xFFFF0000), F32).astype(dtype)
    return jnp.concatenate([lo, hi], axis=1)


def _mod_kernel(c_ref, w_ref, b_ref, o_ref):
    c = c_ref[...]
    s = c * jax.nn.sigmoid(c)
    n = s.shape[0]
    w = w_ref[...]
    s_hi = s.astype(BF16)
    s_lo = (s - s_hi.astype(F32)).astype(BF16)
    w_hi = w.astype(BF16)
    w_lo = (w - w_hi.astype(F32)).astype(BF16)
    r = _dot(jnp.concatenate([s_hi, s_lo], axis=0), w_hi)
    o_ref[...] = r[:n] + r[n:] + _dot(s_hi, w_lo) + b_ref[...]


def _modulation(c_all, w_mod, b_mod):
    tn = 1024
    nrow = c_all.shape[0]
    width = w_mod.shape[2]
    return pl.pallas_call(
        _mod_kernel,
        grid=(DEPTH, width // tn),
        in_specs=[
            pl.BlockSpec((nrow, D_MODEL), lambda l, j: (0, 0)),
            pl.BlockSpec((None, D_MODEL, tn), lambda l, j: (l, 0, j)),
            pl.BlockSpec((None, 1, tn), lambda l, j: (l, 0, j)),
        ],
        out_specs=pl.BlockSpec((None, nrow, tn), lambda l, j: (l, 0, j)),
        out_shape=jax.ShapeDtypeStruct((DEPTH, nrow, width), F32),
        compiler_params=_params(("arbitrary", "arbitrary")),
        name="modulation",
    )(c_all, w_mod, b_mod.reshape(DEPTH, 1, width))


def _in_proj_kernel(x_ref, mod_ref, g_ref, w_ref, *refs, ctx, n_aliased):
    refs = refs[n_aliased:]
    if ctx:
        qkv_ref, ugp_ref, kc_ref, vc_ref, h_scr = refs
    else:
        q_ref, kv_ref, ugp_ref, h_scr = refs
    j = pl.program_id(1)
    heads = TN_IN // HEAD_DIM

    def head_store(ref, dtype):
        def store(m, rows, acc):
            for hh in range(heads):
                ref[hh, rows, :] = acc[:, hh * HEAD_DIM:(hh + 1) * HEAD_DIM].astype(dtype)
        return store

    def cache_store(ref):
        seq = ref.shape[2]
        per_chunk = IN_CHUNK // seq

        def store(m, rows, acc):
            for s in range(per_chunk):
                for hh in range(heads):
                    ref[m * per_chunk + s, hh] = acc[s * seq:(s + 1) * seq,
                                                     hh * HEAD_DIM:(hh + 1) * HEAD_DIM]
        return store

    def ugp_store(m, rows, acc):
        ugp_ref[rows, :] = acc

    def run(normalize, stores):
        for m in range(TM_IN // IN_CHUNK):
            rows = pl.ds(m * IN_CHUNK, IN_CHUNK)
            if normalize:
                h = _rms(x_ref[rows, :], g_ref[...]) * (1.0 + mod_ref[1:2, :]) + mod_ref[0:1, :]
                hb = h.astype(BF16)
                h_scr[rows, :] = hb
            else:
                hb = h_scr[rows, :]
            acc = _dot(hb, w_ref[...])
            for store in stores:
                store(m, rows, acc)

    def case(cond, normalize, stores):
        pl.when(cond)(lambda: run(normalize, stores))

    if ctx:
        case(j == 0, True, [head_store(qkv_ref, BF16)])
        case(j == 1, False, [head_store(qkv_ref, BF16)])
        case((j >= 2) & (j < 4), False, [head_store(qkv_ref, BF16), cache_store(kc_ref)])
        case((j >= 4) & (j < 6), False, [head_store(qkv_ref, BF16), cache_store(vc_ref)])
    else:
        case(j == 0, True, [head_store(q_ref, F32)])
        case(j == 1, False, [head_store(q_ref, F32)])
        case((j >= 2) & (j < 6), False, [head_store(kv_ref, BF16)])
    case(j >= 6, False, [ugp_store])


def _in_proj(x2d, ntok, mod, mod_row_fn, g_mix, w_in, layer, seq_len, ctx, caches=()):
    nblk = ntok // TM_IN
    ncol = IN_WIDTH // TN_IN
    heads = TN_IN // HEAD_DIM
    hm = lambda lo, n: (lambda i, j: (jnp.clip(j - lo, 0, n - 1), i, 0))
    hspec = lambda lo, n: pl.BlockSpec((heads, TM_IN, HEAD_DIM), hm(lo, n))
    ugp_shape = jax.ShapeDtypeStruct((ntok, 2 * D_B + D_C), F32)
    ugp_spec = pl.BlockSpec((TM_IN, TN_IN), lambda i, j: (i, jnp.clip(j - 6, 0, 2)))
    if ctx:
        nseq = ntok // seq_len
        spb = TM_IN // seq_len
        cache_shape = jax.ShapeDtypeStruct((nseq, DEPTH, N_HEADS, seq_len, HEAD_DIM), F32)
        cache_spec = lambda lo: pl.BlockSpec(
            (spb, None, heads, seq_len, HEAD_DIM),
            lambda i, j: (i, layer, jnp.clip(j - lo, 0, 1), 0, 0))
        out_shape = [jax.ShapeDtypeStruct((3 * N_HEADS, ntok, HEAD_DIM), BF16), ugp_shape,
                     cache_shape, cache_shape]
        out_specs = [hspec(0, 6), ugp_spec, cache_spec(2), cache_spec(4)]
    else:
        out_shape = [jax.ShapeDtypeStruct((N_HEADS, ntok, HEAD_DIM), F32),
                     jax.ShapeDtypeStruct((2 * N_HEADS, ntok, HEAD_DIM), BF16), ugp_shape]
        out_specs = [hspec(0, 2), hspec(2, 4), ugp_spec]
    return pl.pallas_call(
        functools.partial(_in_proj_kernel, ctx=ctx, n_aliased=len(caches)),
        grid=(nblk, ncol),
        in_specs=[
            pl.BlockSpec((TM_IN, D_MODEL), lambda i, j: (i, 0)),
            pl.BlockSpec((None, None, N_MOD, D_MODEL), lambda i, j: (layer, mod_row_fn(i), 0, 0)),
            pl.BlockSpec((None, 1, D_MODEL), lambda i, j: (layer, 0, 0)),
            pl.BlockSpec((None, D_MODEL, TN_IN), lambda i, j: (layer, 0, j)),
        ] + [pl.BlockSpec(memory_space=pl.ANY)] * len(caches),
        out_specs=out_specs,
        out_shape=out_shape,
        scratch_shapes=[pltpu.VMEM((TM_IN, D_MODEL), BF16)],
        input_output_aliases={4 + n: 2 + n for n in range(len(caches))},
        compiler_params=_params(("arbitrary", "arbitrary")),
        name="in_proj_ctx" if ctx else "in_proj_lat",
    )(x2d, mod, g_mix, w_in, *caches)


def _write_normed_heads(o_scr, gb_ref, o_ref):
    ss = None
    for h in range(N_HEADS):
        oh = o_scr[h]
        t = jnp.sum(oh * oh, axis=-1, keepdims=True)
        ss = t if ss is None else ss + t
    r = lax.rsqrt(ss * (1.0 / D_A) + EPS)
    for h in range(N_HEADS):
        sl = slice(h * HEAD_DIM, (h + 1) * HEAD_DIM)
        o_ref[:, sl] = (o_scr[h] * r * gb_ref[:, sl]).astype(o_ref.dtype)


def _ctx_attn_kernel(q_ref, k_ref, v_ref, gb_ref, o_ref, o_scr):
    group = 2

    def heads(g, carry):
        hs = [g * group + n for n in range(group)]
        scores = [_dot_nt(q_ref[h], k_ref[h]) * SCALE_LOG2E for h in hs]
        probs = []
        for s in scores:
            p = jnp.exp2(s - jnp.max(s, axis=-1, keepdims=True))
            probs.append((p.astype(BF16), jnp.sum(p, axis=-1, keepdims=True)))
        for h, (p, l) in zip(hs, probs):
            o_scr[h] = _dot(p, v_ref[h]) / l
        return carry

    lax.fori_loop(0, N_HEADS // group, heads, 0)
    _write_normed_heads(o_scr, gb_ref, o_ref)


def _ctx_attention(qkv, gb_a, layer, seq_len):
    ntok = qkv.shape[1]
    spec = lambda part: pl.BlockSpec((N_HEADS, seq_len, HEAD_DIM), lambda b: (part, b, 0))
    return pl.pallas_call(
        _ctx_attn_kernel,
        grid=(ntok // seq_len,),
        in_specs=[spec(0), spec(1), spec(2),
                  pl.BlockSpec((None, 1, D_A), lambda b: (layer, 0, 0))],
        out_specs=pl.BlockSpec((seq_len, D_A), lambda b: (b, 0)),
        out_shape=jax.ShapeDtypeStruct((ntok, D_A), BF16),
        scratch_shapes=[pltpu.VMEM((N_HEADS, seq_len, HEAD_DIM), F32)],
        compiler_params=_params(("arbitrary",)),
        name="ctx_attention",
    )(qkv, qkv, qkv, gb_a)


def _local_bias_table(rpb, rows):
    n_rb = rows // Q_ROWS
    exact = lax.Precision.HIGHEST
    n_dr = 2 * WIN_H - 1
    n_dc = 2 * WIN_W - 1
    sel_r, valid_r = [], []
    for jb in (0, 1, n_rb - 1):
        r0 = Q_ROWS * jb
        ks = min(max(r0 - WIN_H // 2, 0), rows - K_ROWS)
        r = r0 + np.arange(Q_ROWS)[:, None]
        kr = ks + np.arange(K_ROWS)[None, :]
        rs = np.clip(r - WIN_H // 2, 0, rows - WIN_H)
        valid_r.append((kr >= rs) & (kr < rs + WIN_H))
        ri = np.clip(kr - r + WIN_H - 1, 0, n_dr - 1)
        sel_r.append(ri[..., None] == np.arange(n_dr))
    sel_r = np.stack(sel_r).astype(np.float32)
    valid_r = np.stack(valid_r)
    kc0 = np.concatenate([np.full(nc, k0) for (_, nc, k0) in COL_BLOCKS])
    c = np.arange(GRID_W)[:, None]
    kc = kc0[:, None] + np.arange(KEY_COLS)[None, :]
    cs = np.clip(c - WIN_W // 2, 0, GRID_W - WIN_W)
    valid_c = (kc >= cs) & (kc < cs + WIN_W)
    ci = np.clip(kc - c + WIN_W - 1, 0, n_dc - 1)
    sel_c = (ci[..., None] == np.arange(n_dc)).astype(np.float32)
    by_row = jnp.einsum("trka,hab->htrkb", sel_r, rpb, precision=exact)
    b = jnp.einsum("htrkb,cjb->htrckj", by_row, sel_c, precision=exact)
    valid = valid_r[:, :, None, :, None] & valid_c[None, None, :, None, :]
    b = jnp.where(valid[None], b * LOG2E, NEG)
    return b.reshape(rpb.shape[0], 3, Q_ROWS * GRID_W, K_ROWS * KEY_COLS)


def _nbr_attn_kernel(q_ref, k0, k1, k2, k3, v0, v1, v2, v3, ck_ref, cv_ref, bias_ref, gb_ref,
                     o_ref, o_scr):
    k_refs = (k0, k1, k2, k3)
    v_refs = (v0, v1, v2, v3)
    rows_per_ref = K_ROWS // len(k_refs)

    def window(refs, h, kc0):
        parts = []
        for kr in range(K_ROWS):
            ref = refs[kr // rows_per_ref]
            parts.append(ref[h, pl.ds((kr % rows_per_ref) * GRID_W + kc0, KEY_COLS), :])
        return jnp.concatenate(parts, axis=0)

    def query_rows(ref, h, c0, nc):
        return jnp.concatenate(
            [ref[h, pl.ds(qr * GRID_W + c0, nc), :] for qr in range(Q_ROWS)], axis=0)

    def head(h, carry):
        kc = ck_ref[h].astype(BF16)
        vc = cv_ref[h].astype(BF16)
        scores = []
        for (c0, nc, kc0) in COL_BLOCKS:
            qi = query_rows(q_ref, h, c0, nc).astype(BF16)
            ki = window(k_refs, h, kc0)
            s_loc = _dot_nt(qi, ki) * SCALE_LOG2E + query_rows(bias_ref, h, c0, nc)
            s_ctx = _dot_nt(qi, kc) * SCALE_LOG2E
            scores.append((s_loc, s_ctx))
        probs = []
        for s_loc, s_ctx in scores:
            m = jnp.maximum(jnp.max(s_loc, axis=-1, keepdims=True),
                            jnp.max(s_ctx, axis=-1, keepdims=True))
            p_loc = jnp.exp2(s_loc - m)
            p_ctx = jnp.exp2(s_ctx - m)
            l = jnp.sum(p_loc, axis=-1, keepdims=True) + jnp.sum(p_ctx, axis=-1, keepdims=True)
            probs.append((p_loc.astype(BF16), p_ctx.astype(BF16), l))
        for (c0, nc, kc0), (p_loc, p_ctx, l) in zip(COL_BLOCKS, probs):
            vi = window(v_refs, h, kc0)
            o = (_dot(p_loc, vi) + _dot(p_ctx, vc)) / l
            for qr in range(Q_ROWS):
                o_scr[h, pl.ds(qr * GRID_W + c0, nc), :] = o[qr * nc:(qr + 1) * nc]
        return carry

    lax.fori_loop(0, N_HEADS, head, 0)
    _write_normed_heads(o_scr, gb_ref, o_ref)


def _nbr_attention(q, kv, cache_k, cache_v, layer, bias_tab, gb_a, n_req, seq_len):
    rows = seq_len // GRID_W
    n_rb = rows // Q_ROWS
    tq = Q_ROWS * GRID_W
    n_kref = 4
    tk = (K_ROWS // n_kref) * GRID_W
    kblk_per_req = seq_len // tk
    ctx_len = cache_k.shape[3]

    def kmap(part, m):
        def f(b, jb):
            start = jnp.clip(2 * jb - 1, 0, kblk_per_req - n_kref)
            return (part, b * kblk_per_req + start + m, 0)
        return f

    kspecs = [pl.BlockSpec((N_HEADS, tk, HEAD_DIM), kmap(0, m)) for m in range(n_kref)]
    vspecs = [pl.BlockSpec((N_HEADS, tk, HEAD_DIM), kmap(1, m)) for m in range(n_kref)]
    cspec = pl.BlockSpec((None, None, N_HEADS, ctx_len, HEAD_DIM), lambda b, jb: (b, layer, 0, 0, 0))
    kind = lambda b, jb: (layer, (jb > 0).astype(I32) + (jb == n_rb - 1).astype(I32), 0, 0)
    return pl.pallas_call(
        _nbr_attn_kernel,
        grid=(n_req, n_rb),
        in_specs=[pl.BlockSpec((N_HEADS, tq, HEAD_DIM), lambda b, jb: (0, b * n_rb + jb, 0))]
        + kspecs + vspecs + [
            cspec, cspec,
            pl.BlockSpec((N_HEADS, None, tq, K_ROWS * KEY_COLS), kind),
            pl.BlockSpec((None, 1, D_A), lambda b, jb: (layer, 0, 0)),
        ],
        out_specs=pl.BlockSpec((tq, D_A), lambda b, jb: (b * n_rb + jb, 0)),
        out_shape=jax.ShapeDtypeStruct((n_req * seq_len, D_A), BF16),
        scratch_shapes=[pltpu.VMEM((N_HEADS, tq, HEAD_DIM), F32)],
        compiler_params=_params(("arbitrary", "arbitrary")),
        name="nbr_attention",
    )(q, *([kv] * (2 * n_kref)), cache_k, cache_v, bias_tab, gb_a)


def _mixers_kernel(u_ref, gv_ref, p_ref, pprev_ref, pnext_ref, gsgu_ref, wsgu_ref, bsgu_ref,
                   wpool_ref, spool_ref, gb_ref, o_ref, ext_scr, *, seq_len):
    i = pl.program_id(0)
    blocks_per_seq = seq_len // TM_MIX
    bi = i % blocks_per_seq
    t0 = bi * TM_MIX

    gu = jax.nn.gelu(u_ref[...])
    gg = _rms(jax.nn.gelu(gv_ref[...]), gsgu_ref[...]).astype(BF16)
    cols = []
    for g in range(N_GROUPS_B):
        sl = slice(g * CHUNK, (g + 1) * CHUNK)
        chunks = []
        for n in range(TM_MIX // CHUNK):
            rows = slice(n * CHUNK, (n + 1) * CHUNK)
            chunks.append(_dot(wsgu_ref[g], gg[rows, sl]) + bsgu_ref[:, g:g + 1])
        cols.append(jnp.concatenate(chunks, axis=0))
    o_b = gu * jnp.concatenate(cols, axis=1)
    o_ref[:, 0:D_B] = _rms(o_b, gb_ref[:, 0:D_B]).astype(o_ref.dtype)

    p = p_ref[...]
    ext_scr[pl.ds(0, POOL_HALO), :] = jnp.where(bi > 0, pprev_ref[...], 0.0)
    ext_scr[pl.ds(POOL_HALO, TM_MIX), :] = p
    ext_scr[pl.ds(POOL_HALO + TM_MIX, POOL_HALO), :] = jnp.where(
        bi < blocks_per_seq - 1, pnext_ref[...], 0.0)
    t = t0 + lax.broadcasted_iota(I32, (TM_MIX, 1), 0)
    outs = []
    for g, w in enumerate(POOL_WINDOWS):
        half = w // 2
        sl = slice(g * CHUNK, (g + 1) * CHUNK)
        acc = None
        for d in range(-half, half):
            part = ext_scr[pl.ds(POOL_HALO + d, TM_MIX), sl]
            acc = part if acc is None else acc + part
        cnt = (jnp.minimum(t + half, seq_len) - jnp.maximum(t - half, 0)).astype(F32)
        pooled = acc / cnt - p[:, sl]
        outs.append(_dot(pooled.astype(BF16), wpool_ref[g]))
    o_c = jnp.concatenate(outs, axis=1) * spool_ref[...]
    o_ref[:, D_B:D_B + D_C] = _rms(o_c, gb_ref[:, D_B:D_B + D_C]).astype(o_ref.dtype)


def _mixers(ugp, g_sgu, w_sgu, b_sgu_t, w_pool, s_pool, gb_bc, layer, seq_len):
    ntok = ugp.shape[0]
    nblk = ntok // TM_MIX
    hb = TM_MIX // POOL_HALO
    n_halo = ntok // POOL_HALO
    blk = lambda part: pl.BlockSpec((TM_MIX, D_B), lambda i: (i, part))
    full = lambda shape: pl.BlockSpec((None,) + shape, lambda i: (layer,) + (0,) * len(shape))
    return pl.pallas_call(
        functools.partial(_mixers_kernel, seq_len=seq_len),
        grid=(nblk,),
        in_specs=[
            blk(0), blk(1), blk(2),
            pl.BlockSpec((POOL_HALO, D_C), lambda i: (jnp.maximum(i * hb - 1, 0), 2)),
            pl.BlockSpec((POOL_HALO, D_C), lambda i: (jnp.minimum((i + 1) * hb, n_halo - 1), 2)),
            full((1, D_B)), full((N_GROUPS_B, CHUNK, CHUNK)), full((CHUNK, N_GROUPS_B)),
            full((len(POOL_WINDOWS), CHUNK, CHUNK)), full((1, D_C)), full((1, D_B + D_C)),
        ],
        out_specs=pl.BlockSpec((TM_MIX, D_B + D_C), lambda i: (i, 0)),
        out_shape=jax.ShapeDtypeStruct((ntok, D_B + D_C), BF16),
        scratch_shapes=[pltpu.VMEM((TM_MIX + 2 * POOL_HALO, D_C), F32)],
        compiler_params=_params(("arbitrary",)),
        name="mixers",
    )(ugp, ugp, ugp, ugp, ugp, g_sgu, w_sgu, b_sgu_t, w_pool, s_pool, gb_bc)


def _out_proj_kernel(x_ref, oa_ref, obc_ref, mod_ref, g_ref, w_ref, wr_ref, br_ref,
                     x1_ref, h_ref, e_ref, gate_ref):
    chunks = [pl.ds(m * OUT_CHUNK, OUT_CHUNK) for m in range(TM_OUT // OUT_CHUNK)]
    mixes = [_dot(jnp.concatenate([oa_ref[rows, :], obc_ref[rows, :]], axis=1), w_ref[...])
             for rows in chunks]
    splits = []
    for rows, mix in zip(chunks, mixes):
        x1 = x_ref[rows, :] + mod_ref[2:3, :] * mix
        x1_ref[rows, :] = x1
        h = _rms(x1, g_ref[...]) * (1.0 + mod_ref[4:5, :]) + mod_ref[3:4, :]
        h_hi = h.astype(BF16)
        h_ref[rows] = _pack_bf16_pairs(h_hi).reshape((OUT_CHUNK,) + ROW_TILE)
        splits.append((h_hi, (h - h_hi.astype(F32)).astype(BF16)))
    logits = []
    for h_hi, h_lo in splits:
        part = _dot(h_hi, wr_ref[...])
        logits.append((part[:, :LANES] + part[:, LANES:] + _dot(h_lo, wr_ref[:, :LANES]))
                      + br_ref[...])
    for rows, lg in zip(chunks, logits):
        _route_rows(rows, lg, e_ref, gate_ref)


def _route_rows(rows, logits, e_ref, gate_ref):
    lane = lax.broadcasted_iota(I32, logits.shape, 1)
    big = jnp.int32(LANES)

    def softmax_masked(mask):
        z = jnp.where(mask, logits, NEG)
        m = jnp.max(z, axis=-1, keepdims=True)
        e = jnp.where(mask, jnp.exp(z - m), 0.0)
        return e / jnp.sum(e, axis=-1, keepdims=True)

    def top1(vals, mask):
        v = jnp.where(mask, vals, -1.0)
        best = jnp.max(v, axis=-1, keepdims=True)
        idx = jnp.min(jnp.where(mask & (v == best), lane, big), axis=-1, keepdims=True)
        return best, idx

    gmask = lane < N_GROUPS_E
    p_grp = softmax_masked(gmask)
    p_g, g_sel = top1(p_grp, gmask)
    lo = N_GROUPS_E + EXPERTS_PER_GROUP * g_sel
    emask = (lane >= lo) & (lane < lo + EXPERTS_PER_GROUP)
    p_exp = softmax_masked(emask)
    p1, i1 = top1(p_exp, emask)
    p2, i2 = top1(p_exp, emask & (lane != i1))
    denom = p1 + p2
    g1 = p_g * p1 / denom
    g2 = p_g * p2 / denom
    e_ref[rows, :] = jnp.where(lane == 0, i1 - N_GROUPS_E,
                               jnp.where(lane == 1, i2 - N_GROUPS_E, 0))
    gate_ref[rows, :] = jnp.where(lane == 0, g1, jnp.where(lane == 1, g2, 0.0))


def _out_proj(x2d, ntok, o_a, o_bc, mod, mod_row_fn, g_ffn, w_out, w_r, b_r, layer):
    nblk = ntok // TM_OUT
    full = lambda shape: pl.BlockSpec((None,) + shape, lambda i: (layer,) + (0,) * len(shape))
    return pl.pallas_call(
        _out_proj_kernel,
        grid=(nblk,),
        in_specs=[
            pl.BlockSpec((TM_OUT, D_MODEL), lambda i: (i, 0)),
            pl.BlockSpec((TM_OUT, D_A), lambda i: (i, 0)),
            pl.BlockSpec((TM_OUT, D_B + D_C), lambda i: (i, 0)),
            pl.BlockSpec((None, None, N_MOD, D_MODEL), lambda i: (layer, mod_row_fn(i), 0, 0)),
            full((1, D_MODEL)), full((D_MODEL, D_MODEL)), full((D_MODEL, 2 * LANES)), full((1, LANES)),
        ],
        out_specs=[
            pl.BlockSpec((TM_OUT, D_MODEL), lambda i: (i, 0)),
            pl.BlockSpec((TM_OUT,) + ROW_TILE, lambda i: (i, 0, 0)),
            pl.BlockSpec((TM_OUT, LANES), lambda i: (i, 0)),
            pl.BlockSpec((TM_OUT, LANES), lambda i: (i, 0)),
        ],
        out_shape=[
            jax.ShapeDtypeStruct((ntok, D_MODEL), F32),
            jax.ShapeDtypeStruct((ntok,) + ROW_TILE, U32),
            jax.ShapeDtypeStruct((ntok, LANES), I32),
            jax.ShapeDtypeStruct((ntok, LANES), F32),
        ],
        compiler_params=_params(("arbitrary",)),
        name="out_proj_router",
    )(x2d, o_a, o_bc, mod, g_ffn, w_out, w_r, b_r)


def _route_plan(e_flat):
    n_assign = e_flat.shape[0]
    onehot = (e_flat[:, None] == jnp.arange(N_EXPERTS, dtype=I32)[None, :]).astype(I32)
    csum = jnp.cumsum(onehot, axis=0)
    counts = csum[-1]
    rank = jnp.sum(onehot * (csum - 1), axis=1)
    padded = (counts + MOE_BM - 1) // MOE_BM * MOE_BM
    pad_end = jnp.cumsum(padded)
    pad_start = pad_end - padded
    pos = jnp.sum(onehot * pad_start[None, :], axis=1) + rank
    n_blocks = n_assign // MOE_BM + N_EXPERTS
    starts = jnp.arange(n_blocks, dtype=I32) * MOE_BM
    block_e = jnp.minimum(jnp.sum((starts[:, None] >= pad_end[None, :]).astype(I32), axis=1),
                          N_EXPERTS - 1).astype(I32)
    n_used = (pad_end[-1] // MOE_BM).astype(I32).reshape(1)
    fill_start = jnp.where(padded > 0, pad_end - MOE_BM, -1).astype(I32)
    return pos.astype(I32), block_e, n_used, fill_start, n_blocks


def _dispatch_kernel(fill_ref, nu_ref, pos_ref, hp_ref, hs_ref, xs_ref, zero_scr, sem, fill_sem,
                     *, n_ctx_blocks, n_blocks):
    i = pl.program_id(0)

    @pl.when(i == 0)
    def _():
        zero_scr[...] = jnp.zeros_like(zero_scr)

        def fill(start):
            start = pl.multiple_of(start, MOE_BM)
            return pltpu.make_async_copy(zero_scr, xs_ref.at[pl.ds(start, MOE_BM)], fill_sem)

        def expert_fill(e, carry):
            pl.when(fill_ref[e] >= 0)(lambda: fill(fill_ref[e]).start())
            return carry

        def expert_wait(e, carry):
            pl.when(fill_ref[e] >= 0)(lambda: fill(0).wait())
            return carry

        def tail_fill(b, carry):
            fill(b * MOE_BM).start()
            return carry

        def tail_wait(b, carry):
            fill(0).wait()
            return carry

        lax.fori_loop(0, N_EXPERTS, expert_fill, 0)
        lax.fori_loop(nu_ref[0], n_blocks, tail_fill, 0)
        lax.fori_loop(0, N_EXPERTS, expert_wait, 0)
        lax.fori_loop(nu_ref[0], n_blocks, tail_wait, 0)

    def scatter(h_ref):
        def issue(r, carry):
            for k in range(TOP_K):
                slot = pos_ref[0, 0, TOP_K * r + k]
                pltpu.make_async_copy(h_ref.at[r], xs_ref.at[slot], sem).start(priority=k)
            return carry

        lax.fori_loop(0, TM_ROW, issue, 0, unroll=8)
        for _ in range(TOP_K):
            pltpu.make_async_copy(h_ref, h_ref, sem).wait()

    pl.when(i < n_ctx_blocks)(lambda: scatter(hp_ref))
    pl.when(i >= n_ctx_blocks)(lambda: scatter(hs_ref))


def _dispatch(fill_start, n_used, pos3, hp, hs, n_blocks):
    nbp = hp.shape[0] // TM_ROW
    nbs = hs.shape[0] // TM_ROW
    grid_spec = pltpu.PrefetchScalarGridSpec(
        num_scalar_prefetch=2,
        grid=(nbp + nbs,),
        in_specs=[
            pl.BlockSpec((1, 1, TOP_K * TM_ROW), lambda i, fs, nu: (i, 0, 0),
                         memory_space=pltpu.SMEM),
            pl.BlockSpec((TM_ROW,) + ROW_TILE, lambda i, fs, nu: (jnp.minimum(i, nbp - 1), 0, 0)),
            pl.BlockSpec((TM_ROW,) + ROW_TILE, lambda i, fs, nu: (jnp.maximum(i - nbp, 0), 0, 0)),
        ],
        out_specs=pl.BlockSpec(memory_space=pl.ANY),
        scratch_shapes=[pltpu.VMEM((MOE_BM,) + ROW_TILE, U32), pltpu.SemaphoreType.DMA(()),
                        pltpu.SemaphoreType.DMA(())],
    )
    return pl.pallas_call(
        functools.partial(_dispatch_kernel, n_ctx_blocks=nbp, n_blocks=n_blocks),
        grid_spec=grid_spec,
        out_shape=jax.ShapeDtypeStruct((n_blocks * MOE_BM,) + ROW_TILE, U32),
        compiler_params=_params(("arbitrary",)),
        name="moe_dispatch",
    )(fill_start, n_used, pos3, hp, hs)


def _experts_kernel(be_ref, nu_ref, xs_ref, wg_ref, wu_ref, wd_ref, ys_ref):
    del be_ref
    i = pl.program_id(0)

    @pl.when(i < nu_ref[0])
    def _():
        x = _unpack_bf16_pairs(xs_ref[...].reshape(MOE_BM, D_PACK), BF16)
        g = _dot(x, wg_ref[...].astype(BF16))
        u = _dot(x, wu_ref[...].astype(BF16))
        a = (g * jax.nn.sigmoid(g) * u).astype(BF16)
        y = _dot(a, wd_ref[...].astype(BF16)).astype(BF16)
        ys_ref[...] = _pack_bf16_pairs(y).reshape(ys_ref.shape)

    @pl.when(i >= nu_ref[0])
    def _():
        ys_ref[...] = jnp.zeros_like(ys_ref)


def _experts(block_e, n_used, xs, wg, wu, wd, layer, n_blocks):
    grid_spec = pltpu.PrefetchScalarGridSpec(
        num_scalar_prefetch=2,
        grid=(n_blocks,),
        in_specs=[
            pl.BlockSpec((MOE_BM,) + ROW_TILE,
                         lambda i, be, nu: (jnp.minimum(i, nu[0] - 1), 0, 0)),
            pl.BlockSpec((None, None, D_MODEL, D_EXPERT), lambda i, be, nu: (layer, be[i], 0, 0)),
            pl.BlockSpec((None, None, D_MODEL, D_EXPERT), lambda i, be, nu: (layer, be[i], 0, 0)),
            pl.BlockSpec((None, None, D_EXPERT, D_MODEL), lambda i, be, nu: (layer, be[i], 0, 0)),
        ],
        out_specs=pl.BlockSpec((MOE_BM,) + ROW_TILE, lambda i, be, nu: (i, 0, 0)),
    )
    return pl.pallas_call(
        _experts_kernel,
        grid_spec=grid_spec,
        out_shape=jax.ShapeDtypeStruct(xs.shape, U32),
        compiler_params=_params(("arbitrary",)),
        name="moe_experts",
    )(block_e, n_used, xs, wg, wu, wd)


def _combine_kernel(pos_ref, pos_next_ref, x1_ref, gate_ref, mod_ref, gfin_ref, ys_ref, o_ref,
                    ybuf, sem, *, final_norm):
    i = pl.program_id(0)
    buf = i % 2

    def issue(p_ref, b):
        def body(r, carry):
            for k in range(TOP_K):
                slot = p_ref[0, 0, TOP_K * r + k]
                pltpu.make_async_copy(
                    ys_ref.at[slot], ybuf.at[b, k, r], sem.at[b]).start(priority=k)
            return carry
        lax.fori_loop(0, TM_ROW, body, 0, unroll=8)

    @pl.when(i == 0)
    def _():
        issue(pos_ref, 0)

    @pl.when(i + 1 < pl.num_programs(0))
    def _():
        issue(pos_next_ref, 1 - buf)

    for k in range(TOP_K):
        pltpu.make_async_copy(ybuf.at[buf, k], ybuf.at[buf, k], sem.at[buf]).wait()
    y0 = _unpack_bf16_pairs(ybuf[buf, 0].reshape(TM_ROW, D_PACK), F32)
    y1 = _unpack_bf16_pairs(ybuf[buf, 1].reshape(TM_ROW, D_PACK), F32)
    f = gate_ref[:, 0:1] * y0 + gate_ref[:, 1:2] * y1
    x2 = x1_ref[...] + mod_ref[5:6, :] * f
    if final_norm:
        x2 = _rms(x2, gfin_ref[...])
    o_ref[...] = x2


def _combine(pos3, x1, gate, mod, mod_row_fn, g_final, ys, layer, final_norm):
    ntok = x1.shape[0]
    nblk = ntok // TM_ROW
    return pl.pallas_call(
        functools.partial(_combine_kernel, final_norm=final_norm),
        grid=(nblk,),
        in_specs=[
            pl.BlockSpec((1, 1, TOP_K * TM_ROW), lambda i: (i, 0, 0), memory_space=pltpu.SMEM),
            pl.BlockSpec((1, 1, TOP_K * TM_ROW), lambda i: (jnp.minimum(i + 1, nblk - 1), 0, 0),
                         memory_space=pltpu.SMEM),
            pl.BlockSpec((TM_ROW, D_MODEL), lambda i: (i, 0)),
            pl.BlockSpec((TM_ROW, LANES), lambda i: (i, 0)),
            pl.BlockSpec((None, None, N_MOD, D_MODEL), lambda i: (layer, mod_row_fn(i), 0, 0)),
            pl.BlockSpec((1, D_MODEL), lambda i: (0, 0)),
            pl.BlockSpec(memory_space=pl.ANY),
        ],
        out_specs=pl.BlockSpec((TM_ROW, D_MODEL), lambda i: (i, 0)),
        out_shape=jax.ShapeDtypeStruct((ntok, D_MODEL), F32),
        scratch_shapes=[pltpu.VMEM((2, TOP_K, TM_ROW) + ROW_TILE, U32),
                        pltpu.SemaphoreType.DMA((2,))],
        compiler_params=_params(("arbitrary",)),
        name="moe_combine",
    )(pos3, pos3, x1, gate, mod, g_final, ys)


def kernel(x_prompt, x_sample, cache_k, cache_v, c, c_ctx, w_mod, b_mod, g_mix, g_ffn, w_in, rpb,
           g_sgu, w_sgu, b_sgu, w_pool, s_pool, g_branch, w_out, w_rg, b_rg, w_re, b_re,
           w_e_gate, w_e_up, w_e_down, g_final):
    n_ctx, seq, _ = x_prompt.shape
    n_lat, lat_seq, _ = x_sample.shape
    ntok_p = n_ctx * seq
    ntok_s = n_lat * lat_seq
    rows = lat_seq // GRID_W

    n_mod_rows = -(-(1 + n_lat) // SUBLANES) * SUBLANES
    c_all = jnp.concatenate(
        [c_ctx[None], c, jnp.zeros((n_mod_rows - 1 - n_lat, D_MODEL), F32)], axis=0)
    mod = _modulation(c_all, w_mod, b_mod).reshape(DEPTH, n_mod_rows, N_MOD, D_MODEL)

    xp = x_prompt.reshape(ntok_p, D_MODEL)
    xs_lat = x_sample.reshape(ntok_s, D_MODEL)

    def lat_mod_row(tm):
        return lambda i: 1 + (i * tm) // lat_seq

    ctx_mod_row = lambda i: 0

    w_in_b = w_in.astype(BF16)
    w_out_b = w_out.astype(BF16)
    w_sgu_b = w_sgu.astype(BF16)
    w_pool_b = w_pool.astype(BF16)

    gmix = g_mix[:, None, :]
    gffn = g_ffn[:, None, :]
    gb_a = g_branch[:, None, :D_A]
    gb_bc = g_branch[:, None, D_A:]
    gsgu = g_sgu[:, None, :]
    bsgu_t = jnp.swapaxes(b_sgu, 1, 2)
    spool = s_pool[:, None, :]
    n_route = N_GROUPS_E + N_EXPERTS
    w_r = jnp.concatenate(
        [w_rg, w_re, jnp.zeros((DEPTH, D_MODEL, LANES - n_route), F32)], axis=2)
    w_r_hi = lax.bitcast_convert_type(
        lax.bitcast_convert_type(w_r, U32) & jnp.uint32(0xFFFF0000), F32)
    w_r = jnp.concatenate([w_r_hi.astype(BF16), (w_r - w_r_hi).astype(BF16)], axis=2)
    b_r = jnp.concatenate(
        [b_rg, b_re, jnp.zeros((DEPTH, LANES - n_route), F32)], axis=1)[:, None, :]
    bias_tab = _local_bias_table(rpb.reshape((DEPTH * N_HEADS,) + rpb.shape[2:]), rows)

    cache_shape = (n_ctx, DEPTH, N_HEADS, seq, HEAD_DIM)
    caches = (jnp.zeros(cache_shape, F32), jnp.zeros(cache_shape, F32))
    for l in range(DEPTH):
        qkv_p, ugp_p, *caches = _in_proj(
            xp, ntok_p, mod, ctx_mod_row, gmix, w_in_b, l, seq, True, tuple(caches))
        oa_p = _ctx_attention(qkv_p, gb_a, l, seq)
        obc_p = _mixers(ugp_p, gsgu, w_sgu_b, bsgu_t, w_pool_b, spool, gb_bc, l, seq)
        x1p, hp, ep, gatep = _out_proj(
            xp, ntok_p, oa_p, obc_p, mod, ctx_mod_row, gffn, w_out_b, w_r, b_r, l)

        q_s, kv_s, ugp_s = _in_proj(
            xs_lat, ntok_s, mod, lat_mod_row(TM_IN), gmix, w_in_b, l, lat_seq, False)
        oa_s = _nbr_attention(q_s, kv_s, cache_k, cache_v, l, bias_tab, gb_a, n_lat, lat_seq)
        obc_s = _mixers(ugp_s, gsgu, w_sgu_b, bsgu_t, w_pool_b, spool, gb_bc, l, lat_seq)
        x1s, hs, es, gates = _out_proj(
            xs_lat, ntok_s, oa_s, obc_s, mod, lat_mod_row(TM_OUT), gffn, w_out_b, w_r, b_r, l)

        e_flat = jnp.concatenate([ep[:, :TOP_K].reshape(-1), es[:, :TOP_K].reshape(-1)])
        pos, block_e, n_used, fill_start, n_blocks = _route_plan(e_flat)
        pos3 = pos.reshape((ntok_p + ntok_s) // TM_ROW, 1, TOP_K * TM_ROW)
        pos_p = pos3[:ntok_p // TM_ROW]
        pos_s = pos3[ntok_p // TM_ROW:]
        slots = _dispatch(fill_start, n_used, pos3, hp, hs, n_blocks)
        ys = _experts(block_e, n_used, slots, w_e_gate, w_e_up, w_e_down, l, n_blocks)
        last = l == DEPTH - 1
        gfin = g_final[None]
        xp = _combine(pos_p, x1p, gatep, mod, ctx_mod_row, gfin, ys, l, last)
        xs_lat = _combine(pos_s, x1s, gates, mod, lat_mod_row(TM_ROW), gfin, ys, l, last)

    y_prompt = xp.reshape(n_ctx, seq, D_MODEL)
    y_sample = xs_lat.reshape(n_lat, lat_seq, D_MODEL)
    return (y_prompt, y_sample, caches[0], caches[1])
```

```python
import functools

import jax
import numpy as np
import jax.numpy as jnp
from jax import lax
from jax.experimental import pallas as pl
from jax.experimental.pallas import tpu as pltpu

F32 = jnp.float32
BF16 = jnp.bfloat16
I32 = jnp.int32
U32 = jnp.uint32

D_MODEL = 2048
DEPTH = 2
GRID_W = 64
HEAD_DIM = 128
D_A = 1024
N_HEADS = 8
WIN_H = 8
WIN_W = 16
D_B = 512
N_GROUPS_B = 4
CHUNK = 128
D_C = 512
POOL_WINDOWS = (2, 4, 8, 16)
IN_WIDTH = 3 * D_A + 2 * D_B + D_C
N_GROUPS_E = 4
EXPERTS_PER_GROUP = 8
N_EXPERTS = 32
TOP_K = 2
D_EXPERT = 512
N_MOD = 6
EPS = 1e-6
NEG = -1e30
LOG2E = 1.4426950408889634
SCALE_LOG2E = HEAD_DIM ** -0.5 * LOG2E

LANES = 128
SUBLANES = 8
VMEM_LIMIT_BYTES = 56 * 1024 * 1024

TM_IN = 1024
TN_IN = 512
IN_CHUNK = 256
W_SLOTS = 3
OUT_CHUNK = 512
TM_OUT = 512
TM_MIX = 512
Q_ROWS = 8
K_ROWS = 16
MOE_BM = 512
D_PACK = D_MODEL // 2
ROW_TILE = (SUBLANES, LANES)
assert D_PACK == SUBLANES * LANES
TM_ROW = 512
POOL_HALO = 8

COL_BLOCKS = ((0, 24, 0), (24, 16, 16), (40, 24, 32))
KEY_COLS = 32


def _params(sem):
    return pltpu.CompilerParams(dimension_semantics=sem, vmem_limit_bytes=VMEM_LIMIT_BYTES)


def _dot(a, b):
    return jnp.dot(a, b, preferred_element_type=F32)


def _dot_nt(a, b):
    return lax.dot_general(a, b, (((1,), (1,)), ((), ())), preferred_element_type=F32)


def _rms(x, g):
    return x * lax.rsqrt(jnp.mean(x * x, axis=-1, keepdims=True) + EPS) * g


def _pack_bf16_pairs(x):
    n = x.shape[1] // 2
    lo = lax.bitcast_convert_type(x[:, :n].astype(F32), U32) >> 16
    hi = lax.bitcast_convert_type(x[:, n:].astype(F32), U32) & jnp.uint32(0xFFFF0000)
    return hi | lo


def _unpack_bf16_pairs(w, dtype):
    lo = lax.bitcast_convert_type(w << 16, F32).astype(dtype)
    hi = lax.bitcast_convert_type(w & jnp.uint32(0xFFFF0000), F32).astype(dtype)
    return jnp.concatenate([lo, hi], axis=1)


def _mod_kernel(c_ref, w_ref, b_ref, o_ref):
    c = c_ref[...]
    s = c * jax.nn.sigmoid(c)
    n = s.shape[0]
    w = w_ref[...]
    s_hi = s.astype(BF16)
    s_lo = (s - s_hi.astype(F32)).astype(BF16)
    w_hi = w.astype(BF16)
    w_lo = (w - w_hi.astype(F32)).astype(BF16)
    r = _dot(jnp.concatenate([s_hi, s_lo], axis=0), w_hi)
    o_ref[...] = r[:n] + r[n:] + _dot(s_hi, w_lo) + b_ref[...]


def _modulation(c_all, w_mod, b_mod):
    tn = 1024
    nrow = c_all.shape[0]
    width = w_mod.shape[2]
    return pl.pallas_call(
        _mod_kernel,
        grid=(DEPTH, width // tn),
        in_specs=[
            pl.BlockSpec((nrow, D_MODEL), lambda l, j: (0, 0)),
            pl.BlockSpec((None, D_MODEL, tn), lambda l, j: (l, 0, j)),
            pl.BlockSpec((None, 1, tn), lambda l, j: (l, 0, j)),
        ],
        out_specs=pl.BlockSpec((None, nrow, tn), lambda l, j: (l, 0, j)),
        out_shape=jax.ShapeDtypeStruct((DEPTH, nrow, width), F32),
        compiler_params=_params(("arbitrary", "arbitrary")),
        name="modulation",
    )(c_all, w_mod, b_mod.reshape(DEPTH, 1, width))


def _in_proj_kernel(x_hbm, mod_ref, g_ref, w_hbm, *refs, ctx, n_aliased, layer, nblk):
    refs = refs[n_aliased:]
    if ctx:
        qkv_ref, ugp_ref, kc_ref, vc_ref, h_scr, xbuf, wbuf, xsem, wsem = refs
    else:
        q_ref, kv_ref, ugp_ref, h_scr, xbuf, wbuf, xsem, wsem = refs
    i = pl.program_id(0)
    j = pl.program_id(1)
    ncol = IN_WIDTH // TN_IN
    step = i * ncol + j
    heads = TN_IN // HEAD_DIM

    def x_copy(blk):
        slot = blk % 2
        return pltpu.make_async_copy(
            x_hbm.at[pl.ds(pl.multiple_of(blk * TM_IN, TM_IN), TM_IN)], xbuf.at[slot],
            xsem.at[slot])

    def w_copy(s):
        slot = s % W_SLOTS
        col = pl.multiple_of((s % ncol) * TN_IN, TN_IN)
        return pltpu.make_async_copy(
            w_hbm.at[layer, :, pl.ds(col, TN_IN)], wbuf.at[slot], wsem.at[slot])

    @pl.when(step == 0)
    def _():
        x_copy(0).start()
        for s in range(W_SLOTS - 1):
            w_copy(s).start()

    @pl.when((j == 0) & (i + 1 < nblk))
    def _():
        x_copy(i + 1).start()

    @pl.when(step + W_SLOTS - 1 < nblk * ncol)
    def _():
        w_copy(step + W_SLOTS - 1).start()

    @pl.when(j == 0)
    def _():
        x_copy(i).wait()

    w_copy(step).wait()
    x_ref = xbuf.at[i % 2]
    w_ref = wbuf.at[step % W_SLOTS]

    def head_store(ref, dtype):
        def store(m, rows, acc):
            for hh in range(heads):
                ref[hh, rows, :] = acc[:, hh * HEAD_DIM:(hh + 1) * HEAD_DIM].astype(dtype)
        return store

    def cache_store(ref):
        seq = ref.shape[2]
        per_chunk = IN_CHUNK // seq

        def store(m, rows, acc):
            for s in range(per_chunk):
                for hh in range(heads):
                    ref[m * per_chunk + s, hh] = acc[s * seq:(s + 1) * seq,
                                                     hh * HEAD_DIM:(hh + 1) * HEAD_DIM]
        return store

    def ugp_store(m, rows, acc):
        ugp_ref[rows, :] = acc

    def run(normalize, stores):
        for m in range(TM_IN // IN_CHUNK):
            rows = pl.ds(m * IN_CHUNK, IN_CHUNK)
            if normalize:
                h = _rms(x_ref[rows, :], g_ref[...]) * (1.0 + mod_ref[1:2, :]) + mod_ref[0:1, :]
                hb = h.astype(BF16)
                h_scr[rows, :] = hb
            else:
                hb = h_scr[rows, :]
            acc = _dot(hb, w_ref[...])
            for store in stores:
                store(m, rows, acc)

    def case(cond, normalize, stores):
        pl.when(cond)(lambda: run(normalize, stores))

    if ctx:
        case(j == 0, True, [head_store(qkv_ref, BF16)])
        case(j == 1, False, [head_store(qkv_ref, BF16)])
        case((j >= 2) & (j < 4), False, [head_store(qkv_ref, BF16), cache_store(kc_ref)])
        case((j >= 4) & (j < 6), False, [head_store(qkv_ref, BF16), cache_store(vc_ref)])
    else:
        case(j == 0, True, [head_store(q_ref, F32)])
        case(j == 1, False, [head_store(q_ref, F32)])
        case((j >= 2) & (j < 6), False, [head_store(kv_ref, BF16)])
    case(j >= 6, False, [ugp_store])


def _in_proj(x2d, ntok, mod, mod_row_fn, g_mix, w_in, layer, seq_len, ctx, caches=()):
    nblk = ntok // TM_IN
    ncol = IN_WIDTH // TN_IN
    heads = TN_IN // HEAD_DIM
    hm = lambda lo, n: (lambda i, j: (jnp.clip(j - lo, 0, n - 1), i, 0))
    hspec = lambda lo, n: pl.BlockSpec((heads, TM_IN, HEAD_DIM), hm(lo, n))
    ugp_shape = jax.ShapeDtypeStruct((ntok, 2 * D_B + D_C), F32)
    ugp_spec = pl.BlockSpec((TM_IN, TN_IN), lambda i, j: (i, jnp.clip(j - 6, 0, 2)))
    if ctx:
        nseq = ntok // seq_len
        spb = TM_IN // seq_len
        cache_shape = jax.ShapeDtypeStruct((nseq, DEPTH, N_HEADS, seq_len, HEAD_DIM), F32)
        cache_spec = lambda lo: pl.BlockSpec(
            (spb, None, heads, seq_len, HEAD_DIM),
            lambda i, j: (i, layer, jnp.clip(j - lo, 0, 1), 0, 0))
        out_shape = [jax.ShapeDtypeStruct((3 * N_HEADS, ntok, HEAD_DIM), BF16), ugp_shape,
                     cache_shape, cache_shape]
        out_specs = [hspec(0, 6), ugp_spec, cache_spec(2), cache_spec(4)]
    else:
        out_shape = [jax.ShapeDtypeStruct((N_HEADS, ntok, HEAD_DIM), F32),
                     jax.ShapeDtypeStruct((2 * N_HEADS, ntok, HEAD_DIM), BF16), ugp_shape]
        out_specs = [hspec(0, 2), hspec(2, 4), ugp_spec]
    return pl.pallas_call(
        functools.partial(_in_proj_kernel, ctx=ctx, n_aliased=len(caches), layer=layer, nblk=nblk),
        grid=(nblk, ncol),
        in_specs=[
            pl.BlockSpec(memory_space=pl.ANY),
            pl.BlockSpec((None, None, N_MOD, D_MODEL), lambda i, j: (layer, mod_row_fn(i), 0, 0)),
            pl.BlockSpec((None, 1, D_MODEL), lambda i, j: (layer, 0, 0)),
            pl.BlockSpec(memory_space=pl.ANY),
        ] + [pl.BlockSpec(memory_space=pl.ANY)] * len(caches),
        out_specs=out_specs,
        out_shape=out_shape,
        scratch_shapes=[
            pltpu.VMEM((TM_IN, D_MODEL), BF16),
            pltpu.VMEM((2, TM_IN, D_MODEL), F32),
            pltpu.VMEM((W_SLOTS, D_MODEL, TN_IN), BF16),
            pltpu.SemaphoreType.DMA((2,)),
            pltpu.SemaphoreType.DMA((W_SLOTS,)),
        ],
        input_output_aliases={4 + n: 2 + n for n in range(len(caches))},
        compiler_params=_params(("arbitrary", "arbitrary")),
        name="in_proj_ctx" if ctx else "in_proj_lat",
    )(x2d, mod, g_mix, w_in, *caches)


def _write_normed_heads(o_scr, gb_ref, o_ref):
    ss = None
    for h in range(N_HEADS):
        oh = o_scr[h]
        t = jnp.sum(oh * oh, axis=-1, keepdims=True)
        ss = t if ss is None else ss + t
    r = lax.rsqrt(ss * (1.0 / D_A) + EPS)
    for h in range(N_HEADS):
        sl = slice(h * HEAD_DIM, (h + 1) * HEAD_DIM)
        o_ref[:, sl] = (o_scr[h] * r * gb_ref[:, sl]).astype(o_ref.dtype)


def _ctx_attn_kernel(q_ref, k_ref, v_ref, gb_ref, o_ref, o_scr):
    group = 2

    def heads(g, carry):
        hs = [g * group + n for n in range(group)]
        scores = [_dot_nt(q_ref[h], k_ref[h]) * SCALE_LOG2E for h in hs]
        probs = []
        for s in scores:
            p = jnp.exp2(s - jnp.max(s, axis=-1, keepdims=True))
            probs.append((p.astype(BF16), jnp.sum(p, axis=-1, keepdims=True)))
        for h, (p, l) in zip(hs, probs):
            o_scr[h] = _dot(p, v_ref[h]) / l
        return carry

    lax.fori_loop(0, N_HEADS // group, heads, 0)
    _write_normed_heads(o_scr, gb_ref, o_ref)


def _ctx_attention(qkv, gb_a, layer, seq_len):
    ntok = qkv.shape[1]
    spec = lambda part: pl.BlockSpec((N_HEADS, seq_len, HEAD_DIM), lambda b: (part, b, 0))
    return pl.pallas_call(
        _ctx_attn_kernel,
        grid=(ntok // seq_len,),
        in_specs=[spec(0), spec(1), spec(2),
                  pl.BlockSpec((None, 1, D_A), lambda b: (layer, 0, 0))],
        out_specs=pl.BlockSpec((seq_len, D_A), lambda b: (b, 0)),
        out_shape=jax.ShapeDtypeStruct((ntok, D_A), BF16),
        scratch_shapes=[pltpu.VMEM((N_HEADS, seq_len, HEAD_DIM), F32)],
        compiler_params=_params(("arbitrary",)),
        name="ctx_attention",
    )(qkv, qkv, qkv, gb_a)


def _local_bias_table(rpb, rows):
    n_rb = rows // Q_ROWS
    exact = lax.Precision.HIGHEST
    n_dr = 2 * WIN_H - 1
    n_dc = 2 * WIN_W - 1
    sel_r, valid_r = [], []
    for jb in (0, 1, n_rb - 1):
        r0 = Q_ROWS * jb
        ks = min(max(r0 - WIN_H // 2, 0), rows - K_ROWS)
        r = r0 + np.arange(Q_ROWS)[:, None]
        kr = ks + np.arange(K_ROWS)[None, :]
        rs = np.clip(r - WIN_H // 2, 0, rows - WIN_H)
        valid_r.append((kr >= rs) & (kr < rs + WIN_H))
        ri = np.clip(kr - r + WIN_H - 1, 0, n_dr - 1)
        sel_r.append(ri[..., None] == np.arange(n_dr))
    sel_r = np.stack(sel_r).astype(np.float32)
    valid_r = np.stack(valid_r)
    kc0 = np.concatenate([np.full(nc, k0) for (_, nc, k0) in COL_BLOCKS])
    c = np.arange(GRID_W)[:, None]
    kc = kc0[:, None] + np.arange(KEY_COLS)[None, :]
    cs = np.clip(c - WIN_W // 2, 0, GRID_W - WIN_W)
    valid_c = (kc >= cs) & (kc < cs + WIN_W)
    ci = np.clip(kc - c + WIN_W - 1, 0, n_dc - 1)
    sel_c = (ci[..., None] == np.arange(n_dc)).astype(np.float32)
    by_row = jnp.einsum("trka,hab->htrkb", sel_r, rpb, precision=exact)
    b = jnp.einsum("htrkb,cjb->htrckj", by_row, sel_c, precision=exact)
    valid = valid_r[:, :, None, :, None] & valid_c[None, None, :, None, :]
    b = jnp.where(valid[None], b * LOG2E, NEG)
    return b.reshape(rpb.shape[0], 3, Q_ROWS * GRID_W, K_ROWS * KEY_COLS)


def _nbr_attn_kernel(q_ref, k0, k1, k2, k3, v0, v1, v2, v3, ck_ref, cv_ref, bias_ref, gb_ref,
                     o_ref, o_scr):
    k_refs = (k0, k1, k2, k3)
    v_refs = (v0, v1, v2, v3)
    rows_per_ref = K_ROWS // len(k_refs)

    def window(refs, h, kc0):
        parts = []
        for kr in range(K_ROWS):
            ref = refs[kr // rows_per_ref]
            parts.append(ref[h, pl.ds((kr % rows_per_ref) * GRID_W + kc0, KEY_COLS), :])
        return jnp.concatenate(parts, axis=0)

    def query_rows(ref, h, c0, nc):
        return jnp.concatenate(
            [ref[h, pl.ds(qr * GRID_W + c0, nc), :] for qr in range(Q_ROWS)], axis=0)

    def head(h, carry):
        kc = ck_ref[h].astype(BF16)
        vc = cv_ref[h].astype(BF16)
        scores = []
        for (c0, nc, kc0) in COL_BLOCKS:
            qi = query_rows(q_ref, h, c0, nc).astype(BF16)
            ki = window(k_refs, h, kc0)
            s_loc = _dot_nt(qi, ki) * SCALE_LOG2E + query_rows(bias_ref, h, c0, nc)
            s_ctx = _dot_nt(qi, kc) * SCALE_LOG2E
            scores.append((s_loc, s_ctx))
        probs = []
        for s_loc, s_ctx in scores:
            m = jnp.maximum(jnp.max(s_loc, axis=-1, keepdims=True),
                            jnp.max(s_ctx, axis=-1, keepdims=True))
            p_loc = jnp.exp2(s_loc - m)
            p_ctx = jnp.exp2(s_ctx - m)
            l = jnp.sum(p_loc, axis=-1, keepdims=True) + jnp.sum(p_ctx, axis=-1, keepdims=True)
            probs.append((p_loc.astype(BF16), p_ctx.astype(BF16), l))
        for (c0, nc, kc0), (p_loc, p_ctx, l) in zip(COL_BLOCKS, probs):
            vi = window(v_refs, h, kc0)
            o = (_dot(p_loc, vi) + _dot(p_ctx, vc)) / l
            for qr in range(Q_ROWS):
                o_scr[h, pl.ds(qr * GRID_W + c0, nc), :] = o[qr * nc:(qr + 1) * nc]
        return carry

    lax.fori_loop(0, N_HEADS, head, 0)
    _write_normed_heads(o_scr, gb_ref, o_ref)


def _nbr_attention(q, kv, cache_k, cache_v, layer, bias_tab, gb_a, n_req, seq_len):
    rows = seq_len // GRID_W
    n_rb = rows // Q_ROWS
    tq = Q_ROWS * GRID_W
    n_kref = 4
    tk = (K_ROWS // n_kref) * GRID_W
    kblk_per_req = seq_len // tk
    ctx_len = cache_k.shape[3]

    def kmap(part, m):
        def f(b, jb):
            start = jnp.clip(2 * jb - 1, 0, kblk_per_req - n_kref)
            return (part, b * kblk_per_req + start + m, 0)
        return f

    kspecs = [pl.BlockSpec((N_HEADS, tk, HEAD_DIM), kmap(0, m)) for m in range(n_kref)]
    vspecs = [pl.BlockSpec((N_HEADS, tk, HEAD_DIM), kmap(1, m)) for m in range(n_kref)]
    cspec = pl.BlockSpec((None, None, N_HEADS, ctx_len, HEAD_DIM), lambda b, jb: (b, layer, 0, 0, 0))
    kind = lambda b, jb: (layer, (jb > 0).astype(I32) + (jb == n_rb - 1).astype(I32), 0, 0)
    return pl.pallas_call(
        _nbr_attn_kernel,
        grid=(n_req, n_rb),
        in_specs=[pl.BlockSpec((N_HEADS, tq, HEAD_DIM), lambda b, jb: (0, b * n_rb + jb, 0))]
        + kspecs + vspecs + [
            cspec, cspec,
            pl.BlockSpec((N_HEADS, None, tq, K_ROWS * KEY_COLS), kind),
            pl.BlockSpec((None, 1, D_A), lambda b, jb: (layer, 0, 0)),
        ],
        out_specs=pl.BlockSpec((tq, D_A), lambda b, jb: (b * n_rb + jb, 0)),
        out_shape=jax.ShapeDtypeStruct((n_req * seq_len, D_A), BF16),
        scratch_shapes=[pltpu.VMEM((N_HEADS, tq, HEAD_DIM), F32)],
        compiler_params=_params(("arbitrary", "arbitrary")),
        name="nbr_attention",
    )(q, *([kv] * (2 * n_kref)), cache_k, cache_v, bias_tab, gb_a)


def _mixers_kernel(u_ref, gv_ref, p_ref, pprev_ref, pnext_ref, gsgu_ref, wsgu_ref, bsgu_ref,
                   wpool_ref, spool_ref, gb_ref, o_ref, ext_scr, *, seq_len):
    i = pl.program_id(0)
    tm = u_ref.shape[0]
    blocks_per_seq = seq_len // tm
    bi = i % blocks_per_seq
    t0 = bi * tm

    gu = jax.nn.gelu(u_ref[...])
    gg = _rms(jax.nn.gelu(gv_ref[...]), gsgu_ref[...]).astype(BF16)
    cols = []
    for g in range(N_GROUPS_B):
        sl = slice(g * CHUNK, (g + 1) * CHUNK)
        chunks = []
        for n in range(tm // CHUNK):
            rows = slice(n * CHUNK, (n + 1) * CHUNK)
            chunks.append(_dot(wsgu_ref[g], gg[rows, sl]) + bsgu_ref[:, g:g + 1])
        cols.append(jnp.concatenate(chunks, axis=0))
    o_b = gu * jnp.concatenate(cols, axis=1)
    o_ref[:, 0:D_B] = _rms(o_b, gb_ref[:, 0:D_B]).astype(o_ref.dtype)

    p = p_ref[...]
    ext_scr[pl.ds(0, POOL_HALO), :] = jnp.where(bi > 0, pprev_ref[...], 0.0)
    ext_scr[pl.ds(POOL_HALO, tm), :] = p
    ext_scr[pl.ds(POOL_HALO + tm, POOL_HALO), :] = jnp.where(
        bi < blocks_per_seq - 1, pnext_ref[...], 0.0)
    t = t0 + lax.broadcasted_iota(I32, (tm, 1), 0)
    outs = []
    for g, w in enumerate(POOL_WINDOWS):
        half = w // 2
        sl = slice(g * CHUNK, (g + 1) * CHUNK)
        acc = None
        for d in range(-half, half):
            part = ext_scr[pl.ds(POOL_HALO + d, tm), sl]
            acc = part if acc is None else acc + part
        cnt = (jnp.minimum(t + half, seq_len) - jnp.maximum(t - half, 0)).astype(F32)
        pooled = acc / cnt - p[:, sl]
        outs.append(_dot(pooled.astype(BF16), wpool_ref[g]))
    o_c = jnp.concatenate(outs, axis=1) * spool_ref[...]
    o_ref[:, D_B:D_B + D_C] = _rms(o_c, gb_ref[:, D_B:D_B + D_C]).astype(o_ref.dtype)


def _mixers(ugp, g_sgu, w_sgu, b_sgu_t, w_pool, s_pool, gb_bc, layer, seq_len):
    ntok = ugp.shape[0]
    tm = min(TM_MIX, seq_len)
    nblk = ntok // tm
    hb = tm // POOL_HALO
    n_halo = ntok // POOL_HALO
    blk = lambda part: pl.BlockSpec((tm, D_B), lambda i: (i, part))
    full = lambda shape: pl.BlockSpec((None,) + shape, lambda i: (layer,) + (0,) * len(shape))
    return pl.pallas_call(
        functools.partial(_mixers_kernel, seq_len=seq_len),
        grid=(nblk,),
        in_specs=[
            blk(0), blk(1), blk(2),
            pl.BlockSpec((POOL_HALO, D_C), lambda i: (jnp.maximum(i * hb - 1, 0), 2)),
            pl.BlockSpec((POOL_HALO, D_C), lambda i: (jnp.minimum((i + 1) * hb, n_halo - 1), 2)),
            full((1, D_B)), full((N_GROUPS_B, CHUNK, CHUNK)), full((CHUNK, N_GROUPS_B)),
            full((len(POOL_WINDOWS), CHUNK, CHUNK)), full((1, D_C)), full((1, D_B + D_C)),
        ],
        out_specs=pl.BlockSpec((tm, D_B + D_C), lambda i: (i, 0)),
        out_shape=jax.ShapeDtypeStruct((ntok, D_B + D_C), BF16),
        scratch_shapes=[pltpu.VMEM((tm + 2 * POOL_HALO, D_C), F32)],
        compiler_params=_params(("arbitrary",)),
        name="mixers",
    )(ugp, ugp, ugp, ugp, ugp, g_sgu, w_sgu, b_sgu_t, w_pool, s_pool, gb_bc)


def _out_proj_kernel(x_ref, oa_ref, obc_ref, mod_ref, g_ref, w_ref, wr_ref, br_ref,
                     x1_ref, h_ref, e_ref, gate_ref):
    chunks = [pl.ds(m * OUT_CHUNK, OUT_CHUNK) for m in range(TM_OUT // OUT_CHUNK)]
    mixes = [_dot(jnp.concatenate([oa_ref[rows, :], obc_ref[rows, :]], axis=1), w_ref[...])
             for rows in chunks]
    splits = []
    for rows, mix in zip(chunks, mixes):
        x1 = x_ref[rows, :] + mod_ref[2:3, :] * mix
        x1_ref[rows, :] = x1
        h = _rms(x1, g_ref[...]) * (1.0 + mod_ref[4:5, :]) + mod_ref[3:4, :]
        h_hi = h.astype(BF16)
        h_ref[rows] = _pack_bf16_pairs(h_hi).reshape((OUT_CHUNK,) + ROW_TILE)
        splits.append((h_hi, (h - h_hi.astype(F32)).astype(BF16)))
    logits = []
    for h_hi, h_lo in splits:
        part = _dot(h_hi, wr_ref[...])
        logits.append((part[:, :LANES] + part[:, LANES:] + _dot(h_lo, wr_ref[:, :LANES]))
                      + br_ref[...])
    for rows, lg in zip(chunks, logits):
        _route_rows(rows, lg, e_ref, gate_ref)


def _route_rows(rows, logits, e_ref, gate_ref):
    lane = lax.broadcasted_iota(I32, logits.shape, 1)
    big = jnp.int32(LANES)

    def softmax_masked(mask):
        z = jnp.where(mask, logits, NEG)
        m = jnp.max(z, axis=-1, keepdims=True)
        e = jnp.where(mask, jnp.exp(z - m), 0.0)
        return e / jnp.sum(e, axis=-1, keepdims=True)

    def top1(vals, mask):
        v = jnp.where(mask, vals, -1.0)
        best = jnp.max(v, axis=-1, keepdims=True)
        idx = jnp.min(jnp.where(mask & (v == best), lane, big), axis=-1, keepdims=True)
        return best, idx

    gmask = lane < N_GROUPS_E
    p_grp = softmax_masked(gmask)
    p_g, g_sel = top1(p_grp, gmask)
    lo = N_GROUPS_E + EXPERTS_PER_GROUP * g_sel
    emask = (lane >= lo) & (lane < lo + EXPERTS_PER_GROUP)
    p_exp = softmax_masked(emask)
    p1, i1 = top1(p_exp, emask)
    p2, i2 = top1(p_exp, emask & (lane != i1))
    denom = p1 + p2
    g1 = p_g * p1 / denom
    g2 = p_g * p2 / denom
    e_ref[rows, :] = jnp.where(lane == 0, i1 - N_GROUPS_E,
                               jnp.where(lane == 1, i2 - N_GROUPS_E, 0))
    gate_ref[rows, :] = jnp.where(lane == 0, g1, jnp.where(lane == 1, g2, 0.0))


def _out_proj(x2d, ntok, o_a, o_bc, mod, mod_row_fn, g_ffn, w_out, w_r, b_r, layer):
    nblk = ntok // TM_OUT
    full = lambda shape: pl.BlockSpec((None,) + shape, lambda i: (layer,) + (0,) * len(shape))
    return pl.pallas_call(
        _out_proj_kernel,
        grid=(nblk,),
        in_specs=[
            pl.BlockSpec((TM_OUT, D_MODEL), lambda i: (i, 0)),
            pl.BlockSpec((TM_OUT, D_A), lambda i: (i, 0)),
            pl.BlockSpec((TM_OUT, D_B + D_C), lambda i: (i, 0)),
            pl.BlockSpec((None, None, N_MOD, D_MODEL), lambda i: (layer, mod_row_fn(i), 0, 0)),
            full((1, D_MODEL)), full((D_MODEL, D_MODEL)), full((D_MODEL, 2 * LANES)), full((1, LANES)),
        ],
        out_specs=[
            pl.BlockSpec((TM_OUT, D_MODEL), lambda i: (i, 0)),
            pl.BlockSpec((TM_OUT,) + ROW_TILE, lambda i: (i, 0, 0)),
            pl.BlockSpec((TM_OUT, LANES), lambda i: (i, 0)),
            pl.BlockSpec((TM_OUT, LANES), lambda i: (i, 0)),
        ],
        out_shape=[
            jax.ShapeDtypeStruct((ntok, D_MODEL), F32),
            jax.ShapeDtypeStruct((ntok,) + ROW_TILE, U32),
            jax.ShapeDtypeStruct((ntok, LANES), I32),
            jax.ShapeDtypeStruct((ntok, LANES), F32),
        ],
        compiler_params=_params(("arbitrary",)),
        name="out_proj_router",
    )(x2d, o_a, o_bc, mod, g_ffn, w_out, w_r, b_r)


def _route_plan(e_flat):
    n_assign = e_flat.shape[0]
    onehot = (e_flat[:, None] == jnp.arange(N_EXPERTS, dtype=I32)[None, :]).astype(I32)
    csum = jnp.cumsum(onehot, axis=0)
    counts = csum[-1]
    rank = jnp.sum(onehot * (csum - 1), axis=1)
    padded = (counts + MOE_BM - 1) // MOE_BM * MOE_BM
    pad_end = jnp.cumsum(padded)
    pad_start = pad_end - padded
    pos = jnp.sum(onehot * pad_start[None, :], axis=1) + rank
    n_blocks = n_assign // MOE_BM + N_EXPERTS
    starts = jnp.arange(n_blocks, dtype=I32) * MOE_BM
    block_e = jnp.minimum(jnp.sum((starts[:, None] >= pad_end[None, :]).astype(I32), axis=1),
                          N_EXPERTS - 1).astype(I32)
    n_used = (pad_end[-1] // MOE_BM).astype(I32).reshape(1)
    fill_start = jnp.where(padded > 0, pad_end - MOE_BM, -1).astype(I32)
    return pos.astype(I32), block_e, n_used, fill_start, n_blocks


def _dispatch_kernel(fill_ref, nu_ref, pos_ref, hp_ref, hs_ref, xs_ref, zero_scr, sem, fill_sem,
                     *, n_ctx_blocks, n_blocks):
    i = pl.program_id(0)

    @pl.when(i == 0)
    def _():
        zero_scr[...] = jnp.zeros_like(zero_scr)

        def fill(start):
            start = pl.multiple_of(start, MOE_BM)
            return pltpu.make_async_copy(zero_scr, xs_ref.at[pl.ds(start, MOE_BM)], fill_sem)

        def expert_fill(e, carry):
            pl.when(fill_ref[e] >= 0)(lambda: fill(fill_ref[e]).start())
            return carry

        def expert_wait(e, carry):
            pl.when(fill_ref[e] >= 0)(lambda: fill(0).wait())
            return carry

        def tail_fill(b, carry):
            fill(b * MOE_BM).start()
            return carry

        def tail_wait(b, carry):
            fill(0).wait()
            return carry

        lax.fori_loop(0, N_EXPERTS, expert_fill, 0)
        lax.fori_loop(nu_ref[0], n_blocks, tail_fill, 0)
        lax.fori_loop(0, N_EXPERTS, expert_wait, 0)
        lax.fori_loop(nu_ref[0], n_blocks, tail_wait, 0)

    def scatter(h_ref):
        def issue(r, carry):
            for k in range(TOP_K):
                slot = pos_ref[0, 0, TOP_K * r + k]
                pltpu.make_async_copy(h_ref.at[r], xs_ref.at[slot], sem).start(priority=k)
            return carry

        lax.fori_loop(0, TM_ROW, issue, 0, unroll=8)
        for _ in range(TOP_K):
            pltpu.make_async_copy(h_ref, h_ref, sem).wait()

    pl.when(i < n_ctx_blocks)(lambda: scatter(hp_ref))
    pl.when(i >= n_ctx_blocks)(lambda: scatter(hs_ref))


def _dispatch(fill_start, n_used, pos3, hp, hs, n_blocks):
    nbp = hp.shape[0] // TM_ROW
    nbs = hs.shape[0] // TM_ROW
    grid_spec = pltpu.PrefetchScalarGridSpec(
        num_scalar_prefetch=2,
        grid=(nbp + nbs,),
        in_specs=[
            pl.BlockSpec((1, 1, TOP_K * TM_ROW), lambda i, fs, nu: (i, 0, 0),
                         memory_space=pltpu.SMEM),
            pl.BlockSpec((TM_ROW,) + ROW_TILE, lambda i, fs, nu: (jnp.minimum(i, nbp - 1), 0, 0)),
            pl.BlockSpec((TM_ROW,) + ROW_TILE, lambda i, fs, nu: (jnp.maximum(i - nbp, 0), 0, 0)),
        ],
        out_specs=pl.BlockSpec(memory_space=pl.ANY),
        scratch_shapes=[pltpu.VMEM((MOE_BM,) + ROW_TILE, U32), pltpu.SemaphoreType.DMA(()),
                        pltpu.SemaphoreType.DMA(())],
    )
    return pl.pallas_call(
        functools.partial(_dispatch_kernel, n_ctx_blocks=nbp, n_blocks=n_blocks),
        grid_spec=grid_spec,
        out_shape=jax.ShapeDtypeStruct((n_blocks * MOE_BM,) + ROW_TILE, U32),
        compiler_params=_params(("arbitrary",)),
        name="moe_dispatch",
    )(fill_start, n_used, pos3, hp, hs)


def _experts_kernel(be_ref, nu_ref, xs_ref, wg_ref, wu_ref, wd_ref, ys_ref):
    del be_ref
    i = pl.program_id(0)

    @pl.when(i < nu_ref[0])
    def _():
        x = _unpack_bf16_pairs(xs_ref[...].reshape(MOE_BM, D_PACK), BF16)
        g = _dot(x, wg_ref[...].astype(BF16))
        u = _dot(x, wu_ref[...].astype(BF16))
        a = (g * jax.nn.sigmoid(g) * u).astype(BF16)
        y = _dot(a, wd_ref[...].astype(BF16)).astype(BF16)
        ys_ref[...] = _pack_bf16_pairs(y).reshape(ys_ref.shape)

    @pl.when(i >= nu_ref[0])
    def _():
        ys_ref[...] = jnp.zeros_like(ys_ref)


def _experts(block_e, n_used, xs, wg, wu, wd, layer, n_blocks):
    grid_spec = pltpu.PrefetchScalarGridSpec(
        num_scalar_prefetch=2,
        grid=(n_blocks,),
        in_specs=[
            pl.BlockSpec((MOE_BM,) + ROW_TILE,
                         lambda i, be, nu: (jnp.minimum(i, nu[0] - 1), 0, 0)),
            pl.BlockSpec((None, None, D_MODEL, D_EXPERT), lambda i, be, nu: (layer, be[i], 0, 0)),
            pl.BlockSpec((None, None, D_MODEL, D_EXPERT), lambda i, be, nu: (layer, be[i], 0, 0)),
            pl.BlockSpec((None, None, D_EXPERT, D_MODEL), lambda i, be, nu: (layer, be[i], 0, 0)),
        ],
        out_specs=pl.BlockSpec((MOE_BM,) + ROW_TILE, lambda i, be, nu: (i, 0, 0)),
    )
    return pl.pallas_call(
        _experts_kernel,
        grid_spec=grid_spec,
        out_shape=jax.ShapeDtypeStruct(xs.shape, U32),
        compiler_params=_params(("arbitrary",)),
        name="moe_experts",
    )(block_e, n_used, xs, wg, wu, wd)


def _combine_kernel(pos_ref, pos_next_ref, x1_ref, gate_ref, mod_ref, gfin_ref, ys_ref, o_ref,
                    ybuf, sem, *, final_norm):
    i = pl.program_id(0)
    buf = i % 2

    def issue(p_ref, b):
        def body(r, carry):
            for k in range(TOP_K):
                slot = p_ref[0, 0, TOP_K * r + k]
                pltpu.make_async_copy(
                    ys_ref.at[slot], ybuf.at[b, k, r], sem.at[b]).start(priority=k)
            return carry
        lax.fori_loop(0, TM_ROW, body, 0, unroll=8)

    @pl.when(i == 0)
    def _():
        issue(pos_ref, 0)

    @pl.when(i + 1 < pl.num_programs(0))
    def _():
        issue(pos_next_ref, 1 - buf)

    for k in range(TOP_K):
        pltpu.make_async_copy(ybuf.at[buf, k], ybuf.at[buf, k], sem.at[buf]).wait()
    y0 = _unpack_bf16_pairs(ybuf[buf, 0].reshape(TM_ROW, D_PACK), F32)
    y1 = _unpack_bf16_pairs(ybuf[buf, 1].reshape(TM_ROW, D_PACK), F32)
    f = gate_ref[:, 0:1] * y0 + gate_ref[:, 1:2] * y1
    x2 = x1_ref[...] + mod_ref[5:6, :] * f
    if final_norm:
        x2 = _rms(x2, gfin_ref[...])
    o_ref[...] = x2


def _combine(pos3, x1, gate, mod, mod_row_fn, g_final, ys, layer, final_norm):
    ntok = x1.shape[0]
    nblk = ntok // TM_ROW
    return pl.pallas_call(
        functools.partial(_combine_kernel, final_norm=final_norm),
        grid=(nblk,),
        in_specs=[
            pl.BlockSpec((1, 1, TOP_K * TM_ROW), lambda i: (i, 0, 0), memory_space=pltpu.SMEM),
            pl.BlockSpec((1, 1, TOP_K * TM_ROW), lambda i: (jnp.minimum(i + 1, nblk - 1), 0, 0),
                         memory_space=pltpu.SMEM),
            pl.BlockSpec((TM_ROW, D_MODEL), lambda i: (i, 0)),
            pl.BlockSpec((TM_ROW, LANES), lambda i: (i, 0)),
            pl.BlockSpec((None, None, N_MOD, D_MODEL), lambda i: (layer, mod_row_fn(i), 0, 0)),
            pl.BlockSpec((1, D_MODEL), lambda i: (0, 0)),
            pl.BlockSpec(memory_space=pl.ANY),
        ],
        out_specs=pl.BlockSpec((TM_ROW, D_MODEL), lambda i: (i, 0)),
        out_shape=jax.ShapeDtypeStruct((ntok, D_MODEL), F32),
        scratch_shapes=[pltpu.VMEM((2, TOP_K, TM_ROW) + ROW_TILE, U32),
                        pltpu.SemaphoreType.DMA((2,))],
        compiler_params=_params(("arbitrary",)),
        name="moe_combine",
    )(pos3, pos3, x1, gate, mod, g_final, ys)


def kernel(x_prompt, x_sample, cache_k, cache_v, c, c_ctx, w_mod, b_mod, g_mix, g_ffn, w_in, rpb,
           g_sgu, w_sgu, b_sgu, w_pool, s_pool, g_branch, w_out, w_rg, b_rg, w_re, b_re,
           w_e_gate, w_e_up, w_e_down, g_final):
    n_ctx, seq, _ = x_prompt.shape
    n_lat, lat_seq, _ = x_sample.shape
    ntok_p = n_ctx * seq
    ntok_s = n_lat * lat_seq
    rows = lat_seq // GRID_W

    n_mod_rows = -(-(1 + n_lat) // SUBLANES) * SUBLANES
    c_all = jnp.concatenate(
        [c_ctx[None], c, jnp.zeros((n_mod_rows - 1 - n_lat, D_MODEL), F32)], axis=0)
    mod = _modulation(c_all, w_mod, b_mod).reshape(DEPTH, n_mod_rows, N_MOD, D_MODEL)

    xp = x_prompt.reshape(ntok_p, D_MODEL)
    xs_lat = x_sample.reshape(ntok_s, D_MODEL)

    def lat_mod_row(tm):
        return lambda i: 1 + (i * tm) // lat_seq

    ctx_mod_row = lambda i: 0

    w_in_b = w_in.astype(BF16)
    w_out_b = w_out.astype(BF16)
    w_sgu_b = w_sgu.astype(BF16)
    w_pool_b = w_pool.astype(BF16)

    gmix = g_mix[:, None, :]
    gffn = g_ffn[:, None, :]
    gb_a = g_branch[:, None, :D_A]
    gb_bc = g_branch[:, None, D_A:]
    gsgu = g_sgu[:, None, :]
    bsgu_t = jnp.swapaxes(b_sgu, 1, 2)
    spool = s_pool[:, None, :]
    n_route = N_GROUPS_E + N_EXPERTS
    w_r = jnp.concatenate(
        [w_rg, w_re, jnp.zeros((DEPTH, D_MODEL, LANES - n_route), F32)], axis=2)
    w_r_hi = lax.bitcast_convert_type(
        lax.bitcast_convert_type(w_r, U32) & jnp.uint32(0xFFFF0000), F32)
    w_r = jnp.concatenate([w_r_hi.astype(BF16), (w_r - w_r_hi).astype(BF16)], axis=2)
    b_r = jnp.concatenate(
        [b_rg, b_re, jnp.zeros((DEPTH, LANES - n_route), F32)], axis=1)[:, None, :]
    bias_tab = _local_bias_table(rpb.reshape((DEPTH * N_HEADS,) + rpb.shape[2:]), rows)

    cache_shape = (n_ctx, DEPTH, N_HEADS, seq, HEAD_DIM)
    caches = (jnp.zeros(cache_shape, F32), jnp.zeros(cache_shape, F32))
    for l in range(DEPTH):
        qkv_p, ugp_p, *caches = _in_proj(
            xp, ntok_p, mod, ctx_mod_row, gmix, w_in_b, l, seq, True, tuple(caches))
        oa_p = _ctx_attention(qkv_p, gb_a, l, seq)
        obc_p = _mixers(ugp_p, gsgu, w_sgu_b, bsgu_t, w_pool_b, spool, gb_bc, l, seq)
        x1p, hp, ep, gatep = _out_proj(
            xp, ntok_p, oa_p, obc_p, mod, ctx_mod_row, gffn, w_out_b, w_r, b_r, l)

        q_s, kv_s, ugp_s = _in_proj(
            xs_lat, ntok_s, mod, lat_mod_row(TM_IN), gmix, w_in_b, l, lat_seq, False)
        oa_s = _nbr_attention(q_s, kv_s, cache_k, cache_v, l, bias_tab, gb_a, n_lat, lat_seq)
        obc_s = _mixers(ugp_s, gsgu, w_sgu_b, bsgu_t, w_pool_b, spool, gb_bc, l, lat_seq)
        x1s, hs, es, gates = _out_proj(
            xs_lat, ntok_s, oa_s, obc_s, mod, lat_mod_row(TM_OUT), gffn, w_out_b, w_r, b_r, l)

        e_flat = jnp.concatenate([ep[:, :TOP_K].reshape(-1), es[:, :TOP_K].reshape(-1)])
        pos, block_e, n_used, fill_start, n_blocks = _route_plan(e_flat)
        pos3 = pos.reshape((ntok_p + ntok_s) // TM_ROW, 1, TOP_K * TM_ROW)
        pos_p = pos3[:ntok_p // TM_ROW]
        pos_s = pos3[ntok_p // TM_ROW:]
        slots = _dispatch(fill_start, n_used, pos3, hp, hs, n_blocks)
        ys = _experts(block_e, n_used, slots, w_e_gate, w_e_up, w_e_down, l, n_blocks)
        last = l == DEPTH - 1
        gfin = g_final[None]
        xp = _combine(pos_p, x1p, gatep, mod, ctx_mod_row, gfin, ys, l, last)
        xs_lat = _combine(pos_s, x1s, gates, mod, lat_mod_row(TM_ROW), gfin, ys, l, last)

    y_prompt = xp.reshape(n_ctx, seq, D_MODEL)
    y_sample = xs_lat.reshape(n_lat, lat_seq, D_MODEL)
    return (y_prompt, y_sample, caches[0], caches[1])
```

```python
import functools

import jax
import numpy as np
import jax.numpy as jnp
from jax import lax
from jax.experimental import pallas as pl
from jax.experimental.pallas import tpu as pltpu

F32 = jnp.float32
BF16 = jnp.bfloat16
I32 = jnp.int32
U32 = jnp.uint32

D_MODEL = 2048
DEPTH = 2
GRID_W = 64
HEAD_DIM = 128
D_A = 1024
N_HEADS = 8
WIN_H = 8
WIN_W = 16
D_B = 512
N_GROUPS_B = 4
CHUNK = 128
D_C = 512
POOL_WINDOWS = (2, 4, 8, 16)
IN_WIDTH = 3 * D_A + 2 * D_B + D_C
N_GROUPS_E = 4
EXPERTS_PER_GROUP = 8
N_EXPERTS = 32
TOP_K = 2
D_EXPERT = 512
N_MOD = 6
EPS = 1e-6
NEG = -1e30
LOG2E = 1.4426950408889634
SCALE_LOG2E = HEAD_DIM ** -0.5 * LOG2E

LANES = 128
SUBLANES = 8
VMEM_LIMIT_BYTES = 56 * 1024 * 1024

TM_IN = 1024
TN_IN = 512
IN_CHUNK = 256
W_SLOTS = 3
OUT_CHUNK = 512
TM_OUT = 512
TM_MIX = 512
Q_ROWS = 8
K_ROWS = 16
MOE_BM = 512
D_PACK = D_MODEL // 2
ROW_TILE = (SUBLANES, LANES)
assert D_PACK == SUBLANES * LANES
TM_ROW = 512
POOL_HALO = 8

COL_BLOCKS = ((0, 24, 0), (24, 16, 16), (40, 24, 32))
KEY_COLS = 32


def _params(sem):
    return pltpu.CompilerParams(dimension_semantics=sem, vmem_limit_bytes=VMEM_LIMIT_BYTES)


def _dot(a, b):
    return jnp.dot(a, b, preferred_element_type=F32)


def _dot_nt(a, b):
    return lax.dot_general(a, b, (((1,), (1,)), ((), ())), preferred_element_type=F32)


def _rms(x, g):
    return x * lax.rsqrt(jnp.mean(x * x, axis=-1, keepdims=True) + EPS) * g


def _pack_bf16_pairs(x):
    n = x.shape[1] // 2
    lo = lax.bitcast_convert_type(x[:, :n].astype(F32), U32) >> 16
    hi = lax.bitcast_convert_type(x[:, n:].astype(F32), U32) & jnp.uint32(0xFFFF0000)
    return hi | lo


def _unpack_bf16_pairs(w, dtype):
    lo = lax.bitcast_convert_type(w << 16, F32).astype(dtype)
    hi = lax.bitcast_convert_type(w & jnp.uint32(0xFFFF0000), F32).astype(dtype)
    return jnp.concatenate([lo, hi], axis=1)


def _mod_kernel(c_ref, w_ref, b_ref, o_ref):
    c = c_ref[...]
    s = c * jax.nn.sigmoid(c)
    n = s.shape[0]
    w = w_ref[...]
    s_hi = s.astype(BF16)
    s_lo = (s - s_hi.astype(F32)).astype(BF16)
    w_hi = w.astype(BF16)
    w_lo = (w - w_hi.astype(F32)).astype(BF16)
    r = _dot(jnp.concatenate([s_hi, s_lo], axis=0), w_hi)
    o_ref[...] = r[:n] + r[n:] + _dot(s_hi, w_lo) + b_ref[...]


def _modulation(c_all, w_mod, b_mod):
    tn = 1024
    nrow = c_all.shape[0]
    width = w_mod.shape[2]
    return pl.pallas_call(
        _mod_kernel,
        grid=(DEPTH, width // tn),
        in_specs=[
            pl.BlockSpec((nrow, D_MODEL), lambda l, j: (0, 0)),
            pl.BlockSpec((None, D_MODEL, tn), lambda l, j: (l, 0, j)),
            pl.BlockSpec((None, 1, tn), lambda l, j: (l, 0, j)),
        ],
        out_specs=pl.BlockSpec((None, nrow, tn), lambda l, j: (l, 0, j)),
        out_shape=jax.ShapeDtypeStruct((DEPTH, nrow, width), F32),
        compiler_params=_params(("arbitrary", "arbitrary")),
        name="modulation",
    )(c_all, w_mod, b_mod.reshape(DEPTH, 1, width))


def _in_proj_kernel(x_hbm, mod_ref, g_ref, w_hbm, *refs, ctx, n_aliased, layer, nblk):
    refs = refs[n_aliased:]
    if ctx:
        qkv_ref, ugp_ref, kc_ref, vc_ref, h_scr, xbuf, wbuf, xsem, wsem = refs
    else:
        q_ref, kv_ref, ugp_ref, h_scr, xbuf, wbuf, xsem, wsem = refs
    i = pl.program_id(0)
    j = pl.program_id(1)
    ncol = IN_WIDTH // TN_IN
    step = i * ncol + j
    heads = TN_IN // HEAD_DIM

    def x_copy(blk):
        slot = blk % 2
        return pltpu.make_async_copy(
            x_hbm.at[pl.ds(pl.multiple_of(blk * TM_IN, TM_IN), TM_IN)], xbuf.at[slot],
            xsem.at[slot])

    def w_copy(s):
        slot = s % W_SLOTS
        col = pl.multiple_of((s % ncol) * TN_IN, TN_IN)
        return pltpu.make_async_copy(
            w_hbm.at[layer, :, pl.ds(col, TN_IN)], wbuf.at[slot], wsem.at[slot])

    @pl.when(step == 0)
    def _():
        x_copy(0).start()
        for s in range(W_SLOTS - 1):
            w_copy(s).start()

    @pl.when((j == 0) & (i + 1 < nblk))
    def _():
        x_copy(i + 1).start()

    @pl.when(step + W_SLOTS - 1 < nblk * ncol)
    def _():
        w_copy(step + W_SLOTS - 1).start()

    @pl.when(j == 0)
    def _():
        x_copy(i).wait()

    w_copy(step).wait()
    x_ref = xbuf.at[i % 2]
    w_ref = wbuf.at[step % W_SLOTS]

    def head_store(ref, dtype):
        def store(m, rows, acc):
            for hh in range(heads):
                ref[hh, rows, :] = acc[:, hh * HEAD_DIM:(hh + 1) * HEAD_DIM].astype(dtype)
        return store

    def cache_store(ref):
        seq = ref.shape[2]
        per_chunk = IN_CHUNK // seq

        def store(m, rows, acc):
            for s in range(per_chunk):
                for hh in range(heads):
                    ref[m * per_chunk + s, hh] = acc[s * seq:(s + 1) * seq,
                                                     hh * HEAD_DIM:(hh + 1) * HEAD_DIM]
        return store

    def ugp_store(m, rows, acc):
        ugp_ref[rows, :] = acc

    def run(normalize, stores):
        for m in range(TM_IN // IN_CHUNK):
            rows = pl.ds(m * IN_CHUNK, IN_CHUNK)
            if normalize:
                h = _rms(x_ref[rows, :], g_ref[...]) * (1.0 + mod_ref[1:2, :]) + mod_ref[0:1, :]
                hb = h.astype(BF16)
                h_scr[rows, :] = hb
            else:
                hb = h_scr[rows, :]
            acc = _dot(hb, w_ref[...])
            for store in stores:
                store(m, rows, acc)

    def case(cond, normalize, stores):
        pl.when(cond)(lambda: run(normalize, stores))

    if ctx:
        case(j == 0, True, [head_store(qkv_ref, BF16)])
        case(j == 1, False, [head_store(qkv_ref, BF16)])
        case((j >= 2) & (j < 4), False, [head_store(qkv_ref, BF16), cache_store(kc_ref)])
        case((j >= 4) & (j < 6), False, [head_store(qkv_ref, BF16), cache_store(vc_ref)])
    else:
        case(j == 0, True, [head_store(q_ref, F32)])
        case(j == 1, False, [head_store(q_ref, F32)])
        case((j >= 2) & (j < 6), False, [head_store(kv_ref, BF16)])
    case(j >= 6, False, [ugp_store])


def _in_proj(x2d, ntok, mod, mod_row_fn, g_mix, w_in, layer, seq_len, ctx, caches=()):
    nblk = ntok // TM_IN
    ncol = IN_WIDTH // TN_IN
    heads = TN_IN // HEAD_DIM
    hm = lambda lo, n: (lambda i, j: (jnp.clip(j - lo, 0, n - 1), i, 0))
    hspec = lambda lo, n: pl.BlockSpec((heads, TM_IN, HEAD_DIM), hm(lo, n))
    ugp_shape = jax.ShapeDtypeStruct((ntok, 2 * D_B + D_C), F32)
    ugp_spec = pl.BlockSpec((TM_IN, TN_IN), lambda i, j: (i, jnp.clip(j - 6, 0, 2)))
    if ctx:
        nseq = ntok // seq_len
        spb = TM_IN // seq_len
        cache_shape = jax.ShapeDtypeStruct((nseq, DEPTH, N_HEADS, seq_len, HEAD_DIM), F32)
        cache_spec = lambda lo: pl.BlockSpec(
            (spb, None, heads, seq_len, HEAD_DIM),
            lambda i, j: (i, layer, jnp.clip(j - lo, 0, 1), 0, 0))
        out_shape = [jax.ShapeDtypeStruct((3 * N_HEADS, ntok, HEAD_DIM), BF16), ugp_shape,
                     cache_shape, cache_shape]
        out_specs = [hspec(0, 6), ugp_spec, cache_spec(2), cache_spec(4)]
    else:
        out_shape = [jax.ShapeDtypeStruct((N_HEADS, ntok, HEAD_DIM), F32),
                     jax.ShapeDtypeStruct((2 * N_HEADS, ntok, HEAD_DIM), BF16), ugp_shape]
        out_specs = [hspec(0, 2), hspec(2, 4), ugp_spec]
    return pl.pallas_call(
        functools.partial(_in_proj_kernel, ctx=ctx, n_aliased=len(caches), layer=layer, nblk=nblk),
        grid=(nblk, ncol),
        in_specs=[
            pl.BlockSpec(memory_space=pl.ANY),
            pl.BlockSpec((None, None, N_MOD, D_MODEL), lambda i, j: (layer, mod_row_fn(i), 0, 0)),
            pl.BlockSpec((None, 1, D_MODEL), lambda i, j: (layer, 0, 0)),
            pl.BlockSpec(memory_space=pl.ANY),
        ] + [pl.BlockSpec(memory_space=pl.ANY)] * len(caches),
        out_specs=out_specs,
        out_shape=out_shape,
        scratch_shapes=[
            pltpu.VMEM((TM_IN, D_MODEL), BF16),
            pltpu.VMEM((2, TM_IN, D_MODEL), F32),
            pltpu.VMEM((W_SLOTS, D_MODEL, TN_IN), BF16),
            pltpu.SemaphoreType.DMA((2,)),
            pltpu.SemaphoreType.DMA((W_SLOTS,)),
        ],
        input_output_aliases={4 + n: 2 + n for n in range(len(caches))},
        compiler_params=_params(("arbitrary", "arbitrary")),
        name="in_proj_ctx" if ctx else "in_proj_lat",
    )(x2d, mod, g_mix, w_in, *caches)


def _write_normed_heads(o_scr, gb_ref, o_ref):
    ss = None
    for h in range(N_HEADS):
        oh = o_scr[h]
        t = jnp.sum(oh * oh, axis=-1, keepdims=True)
        ss = t if ss is None else ss + t
    r = lax.rsqrt(ss * (1.0 / D_A) + EPS)
    for h in range(N_HEADS):
        sl = slice(h * HEAD_DIM, (h + 1) * HEAD_DIM)
        o_ref[:, sl] = (o_scr[h] * r * gb_ref[:, sl]).astype(o_ref.dtype)


def _ctx_attn_kernel(q_ref, k_ref, v_ref, gb_ref, o_ref, o_scr):
    group = 2

    def heads(g, carry):
        hs = [g * group + n for n in range(group)]
        scores = [_dot_nt(q_ref[h], k_ref[h]) * SCALE_LOG2E for h in hs]
        probs = []
        for s in scores:
            p = jnp.exp2(s - jnp.max(s, axis=-1, keepdims=True))
            probs.append((p.astype(BF16), jnp.sum(p, axis=-1, keepdims=True)))
        for h, (p, l) in zip(hs, probs):
            o_scr[h] = _dot(p, v_ref[h]) / l
        return carry

    lax.fori_loop(0, N_HEADS // group, heads, 0)
    _write_normed_heads(o_scr, gb_ref, o_ref)


def _ctx_attention(qkv, gb_a, layer, seq_len):
    ntok = qkv.shape[1]
    spec = lambda part: pl.BlockSpec((N_HEADS, seq_len, HEAD_DIM), lambda b: (part, b, 0))
    return pl.pallas_call(
        _ctx_attn_kernel,
        grid=(ntok // seq_len,),
        in_specs=[spec(0), spec(1), spec(2),
                  pl.BlockSpec((None, 1, D_A), lambda b: (layer, 0, 0))],
        out_specs=pl.BlockSpec((seq_len, D_A), lambda b: (b, 0)),
        out_shape=jax.ShapeDtypeStruct((ntok, D_A), BF16),
        scratch_shapes=[pltpu.VMEM((N_HEADS, seq_len, HEAD_DIM), F32)],
        compiler_params=_params(("arbitrary",)),
        name="ctx_attention",
    )(qkv, qkv, qkv, gb_a)


def _bias_table_kernel(sel_ref, rpb_ref, o_ref):
    kr_per_tile = LANES // KEY_COLS
    by_row = jnp.dot(sel_ref[...], rpb_ref[...], preferred_element_type=F32,
                     precision=lax.Precision.HIGHEST)
    for (c0, nc, kc0) in COL_BLOCKS:
        lane = lax.broadcasted_iota(I32, (nc, LANES), 1)
        c = c0 + lax.broadcasted_iota(I32, (nc, LANES), 0)
        cs = jnp.clip(c - WIN_W // 2, 0, GRID_W - WIN_W)
        masks = []
        for k4 in range(kr_per_tile):
            kc = kc0 + lane - k4 * KEY_COLS
            masks.append((lane >= k4 * KEY_COLS) & (lane < (k4 + 1) * KEY_COLS)
                         & (kc >= cs) & (kc < cs + WIN_W))
        for qr in range(Q_ROWS):
            for t in range(K_ROWS // kr_per_tile):
                tile = jnp.full((nc, LANES), NEG, F32)
                for k4 in range(kr_per_tile):
                    row = qr * K_ROWS + t * kr_per_tile + k4
                    x = jnp.broadcast_to(by_row[row:row + 1, :], (nc, LANES))
                    shift = (k4 * KEY_COLS - kc0 - (WIN_W - 1) + c0) % LANES
                    x = pltpu.roll(x, shift, 1, stride=1, stride_axis=0)
                    tile = jnp.where(masks[k4], x * LOG2E, tile)
                o_ref[pl.ds(qr * GRID_W + c0, nc), pl.ds(t * LANES, LANES)] = tile


def _local_bias_table(rpb, rows):
    n_rb = rows // Q_ROWS
    n_dr = 2 * WIN_H - 1
    n_dc = 2 * WIN_W - 1
    n_sel = n_dr + 1
    sel = np.zeros((3, Q_ROWS, K_ROWS, n_sel), np.float32)
    for t, jb in enumerate((0, 1, n_rb - 1)):
        r0 = Q_ROWS * jb
        ks = min(max(r0 - WIN_H // 2, 0), rows - K_ROWS)
        r = r0 + np.arange(Q_ROWS)[:, None]
        kr = ks + np.arange(K_ROWS)[None, :]
        rs = np.clip(r - WIN_H // 2, 0, rows - WIN_H)
        valid = (kr >= rs) & (kr < rs + WIN_H)
        ri = np.where(valid, kr - r + WIN_H - 1, n_dr)
        sel[t] = ri[..., None] == np.arange(n_sel)
    sel = sel.reshape(3, Q_ROWS * K_ROWS, n_sel)
    n_lh = rpb.shape[0]
    rpb_pad = jnp.concatenate(
        [jnp.pad(rpb, ((0, 0), (0, 0), (0, LANES - n_dc))), jnp.full((n_lh, 1, LANES), NEG, F32)],
        axis=1)
    nq = Q_ROWS * GRID_W
    nk = K_ROWS * KEY_COLS
    return pl.pallas_call(
        _bias_table_kernel,
        grid=(n_lh, 3),
        in_specs=[
            pl.BlockSpec((None, Q_ROWS * K_ROWS, n_sel), lambda h, t: (t, 0, 0)),
            pl.BlockSpec((None, n_sel, LANES), lambda h, t: (h, 0, 0)),
        ],
        out_specs=pl.BlockSpec((None, None, nq, nk), lambda h, t: (h, t, 0, 0)),
        out_shape=jax.ShapeDtypeStruct((n_lh, 3, nq, nk), F32),
        compiler_params=_params(("arbitrary", "arbitrary")),
        name="bias_table",
    )(jnp.asarray(sel), rpb_pad)


def _nbr_attn_kernel(q_ref, k0, k1, k2, k3, v0, v1, v2, v3, ck_ref, cv_ref, bias_ref, gb_ref,
                     o_ref, o_scr):
    k_refs = (k0, k1, k2, k3)
    v_refs = (v0, v1, v2, v3)
    rows_per_ref = K_ROWS // len(k_refs)

    def window(refs, h, kc0):
        parts = []
        for kr in range(K_ROWS):
            ref = refs[kr // rows_per_ref]
            parts.append(ref[h, pl.ds((kr % rows_per_ref) * GRID_W + kc0, KEY_COLS), :])
        return jnp.concatenate(parts, axis=0)

    def query_rows(ref, h, c0, nc):
        return jnp.concatenate(
            [ref[h, pl.ds(qr * GRID_W + c0, nc), :] for qr in range(Q_ROWS)], axis=0)

    def head(h, carry):
        kc = ck_ref[h].astype(BF16)
        vc = cv_ref[h].astype(BF16)
        scores = []
        for (c0, nc, kc0) in COL_BLOCKS:
            qi = query_rows(q_ref, h, c0, nc).astype(BF16)
            ki = window(k_refs, h, kc0)
            s_loc = _dot_nt(qi, ki) * SCALE_LOG2E + query_rows(bias_ref, h, c0, nc)
            s_ctx = _dot_nt(qi, kc) * SCALE_LOG2E
            scores.append((s_loc, s_ctx))
        probs = []
        for s_loc, s_ctx in scores:
            m = jnp.maximum(jnp.max(s_loc, axis=-1, keepdims=True),
                            jnp.max(s_ctx, axis=-1, keepdims=True))
            p_loc = jnp.exp2(s_loc - m)
            p_ctx = jnp.exp2(s_ctx - m)
            l = jnp.sum(p_loc, axis=-1, keepdims=True) + jnp.sum(p_ctx, axis=-1, keepdims=True)
            probs.append((p_loc.astype(BF16), p_ctx.astype(BF16), l))
        for (c0, nc, kc0), (p_loc, p_ctx, l) in zip(COL_BLOCKS, probs):
            vi = window(v_refs, h, kc0)
            o = (_dot(p_loc, vi) + _dot(p_ctx, vc)) / l
            for qr in range(Q_ROWS):
                o_scr[h, pl.ds(qr * GRID_W + c0, nc), :] = o[qr * nc:(qr + 1) * nc]
        return carry

    lax.fori_loop(0, N_HEADS, head, 0)
    _write_normed_heads(o_scr, gb_ref, o_ref)


def _nbr_attention(q, kv, cache_k, cache_v, layer, bias_tab, gb_a, n_req, seq_len):
    rows = seq_len // GRID_W
    n_rb = rows // Q_ROWS
    tq = Q_ROWS * GRID_W
    n_kref = 4
    tk = (K_ROWS // n_kref) * GRID_W
    kblk_per_req = seq_len // tk
    ctx_len = cache_k.shape[3]

    def kmap(part, m):
        def f(b, jb):
            start = jnp.clip(2 * jb - 1, 0, kblk_per_req - n_kref)
            return (part, b * kblk_per_req + start + m, 0)
        return f

    kspecs = [pl.BlockSpec((N_HEADS, tk, HEAD_DIM), kmap(0, m)) for m in range(n_kref)]
    vspecs = [pl.BlockSpec((N_HEADS, tk, HEAD_DIM), kmap(1, m)) for m in range(n_kref)]
    cspec = pl.BlockSpec((None, None, N_HEADS, ctx_len, HEAD_DIM), lambda b, jb: (b, layer, 0, 0, 0))
    kind = lambda b, jb: (layer, (jb > 0).astype(I32) + (jb == n_rb - 1).astype(I32), 0, 0)
    return pl.pallas_call(
        _nbr_attn_kernel,
        grid=(n_req, n_rb),
        in_specs=[pl.BlockSpec((N_HEADS, tq, HEAD_DIM), lambda b, jb: (0, b * n_rb + jb, 0))]
        + kspecs + vspecs + [
            cspec, cspec,
            pl.BlockSpec((N_HEADS, None, tq, K_ROWS * KEY_COLS), kind),
            pl.BlockSpec((None, 1, D_A), lambda b, jb: (layer, 0, 0)),
        ],
        out_specs=pl.BlockSpec((tq, D_A), lambda b, jb: (b * n_rb + jb, 0)),
        out_shape=jax.ShapeDtypeStruct((n_req * seq_len, D_A), BF16),
        scratch_shapes=[pltpu.VMEM((N_HEADS, tq, HEAD_DIM), F32)],
        compiler_params=_params(("arbitrary", "arbitrary")),
        name="nbr_attention",
    )(q, *([kv] * (2 * n_kref)), cache_k, cache_v, bias_tab, gb_a)


def _mixers_kernel(u_ref, gv_ref, p_ref, pprev_ref, pnext_ref, gsgu_ref, wsgu_ref, bsgu_ref,
                   wpool_ref, spool_ref, gb_ref, o_ref, ext_scr, *, seq_len):
    i = pl.program_id(0)
    tm = u_ref.shape[0]
    blocks_per_seq = seq_len // tm
    bi = i % blocks_per_seq
    t0 = bi * tm

    gu = jax.nn.gelu(u_ref[...])
    gg = _rms(jax.nn.gelu(gv_ref[...]), gsgu_ref[...]).astype(BF16)
    cols = []
    for g in range(N_GROUPS_B):
        sl = slice(g * CHUNK, (g + 1) * CHUNK)
        chunks = []
        for n in range(tm // CHUNK):
            rows = slice(n * CHUNK, (n + 1) * CHUNK)
            chunks.append(_dot(wsgu_ref[g], gg[rows, sl]) + bsgu_ref[:, g:g + 1])
        cols.append(jnp.concatenate(chunks, axis=0))
    o_b = gu * jnp.concatenate(cols, axis=1)
    o_ref[:, 0:D_B] = _rms(o_b, gb_ref[:, 0:D_B]).astype(o_ref.dtype)

    p = p_ref[...]
    ext_scr[pl.ds(0, POOL_HALO), :] = jnp.where(bi > 0, pprev_ref[...], 0.0)
    ext_scr[pl.ds(POOL_HALO, tm), :] = p
    ext_scr[pl.ds(POOL_HALO + tm, POOL_HALO), :] = jnp.where(
        bi < blocks_per_seq - 1, pnext_ref[...], 0.0)
    t = t0 + lax.broadcasted_iota(I32, (tm, 1), 0)
    outs = []
    for g, w in enumerate(POOL_WINDOWS):
        half = w // 2
        sl = slice(g * CHUNK, (g + 1) * CHUNK)
        acc = None
        for d in range(-half, half):
            part = ext_scr[pl.ds(POOL_HALO + d, tm), sl]
            acc = part if acc is None else acc + part
        cnt = (jnp.minimum(t + half, seq_len) - jnp.maximum(t - half, 0)).astype(F32)
        pooled = acc / cnt - p[:, sl]
        outs.append(_dot(pooled.astype(BF16), wpool_ref[g]))
    o_c = jnp.concatenate(outs, axis=1) * spool_ref[...]
    o_ref[:, D_B:D_B + D_C] = _rms(o_c, gb_ref[:, D_B:D_B + D_C]).astype(o_ref.dtype)


def _mixers(ugp, g_sgu, w_sgu, b_sgu_t, w_pool, s_pool, gb_bc, layer, seq_len):
    ntok = ugp.shape[0]
    tm = min(TM_MIX, seq_len)
    nblk = ntok // tm
    hb = tm // POOL_HALO
    n_halo = ntok // POOL_HALO
    blk = lambda part: pl.BlockSpec((tm, D_B), lambda i: (i, part))
    full = lambda shape: pl.BlockSpec((None,) + shape, lambda i: (layer,) + (0,) * len(shape))
    return pl.pallas_call(
        functools.partial(_mixers_kernel, seq_len=seq_len),
        grid=(nblk,),
        in_specs=[
            blk(0), blk(1), blk(2),
            pl.BlockSpec((POOL_HALO, D_C), lambda i: (jnp.maximum(i * hb - 1, 0), 2)),
            pl.BlockSpec((POOL_HALO, D_C), lambda i: (jnp.minimum((i + 1) * hb, n_halo - 1), 2)),
            full((1, D_B)), full((N_GROUPS_B, CHUNK, CHUNK)), full((CHUNK, N_GROUPS_B)),
            full((len(POOL_WINDOWS), CHUNK, CHUNK)), full((1, D_C)), full((1, D_B + D_C)),
        ],
        out_specs=pl.BlockSpec((tm, D_B + D_C), lambda i: (i, 0)),
        out_shape=jax.ShapeDtypeStruct((ntok, D_B + D_C), BF16),
        scratch_shapes=[pltpu.VMEM((tm + 2 * POOL_HALO, D_C), F32)],
        compiler_params=_params(("arbitrary",)),
        name="mixers",
    )(ugp, ugp, ugp, ugp, ugp, g_sgu, w_sgu, b_sgu_t, w_pool, s_pool, gb_bc)


def _out_proj_kernel(x_ref, oa_ref, obc_ref, mod_ref, g_ref, w_ref, wr_ref, br_ref,
                     x1_ref, h_ref, e_ref, gate_ref):
    chunks = [pl.ds(m * OUT_CHUNK, OUT_CHUNK) for m in range(TM_OUT // OUT_CHUNK)]
    mixes = [_dot(jnp.concatenate([oa_ref[rows, :], obc_ref[rows, :]], axis=1), w_ref[...])
             for rows in chunks]
    splits = []
    for rows, mix in zip(chunks, mixes):
        x1 = x_ref[rows, :] + mod_ref[2:3, :] * mix
        x1_ref[rows, :] = x1
        h = _rms(x1, g_ref[...]) * (1.0 + mod_ref[4:5, :]) + mod_ref[3:4, :]
        h_hi = h.astype(BF16)
        h_ref[rows] = _pack_bf16_pairs(h_hi).reshape((OUT_CHUNK,) + ROW_TILE)
        splits.append((h_hi, (h - h_hi.astype(F32)).astype(BF16)))
    logits = []
    for h_hi, h_lo in splits:
        part = _dot(h_hi, wr_ref[...])
        logits.append((part[:, :LANES] + part[:, LANES:] + _dot(h_lo, wr_ref[:, :LANES]))
                      + br_ref[...])
    for rows, lg in zip(chunks, logits):
        _route_rows(rows, lg, e_ref, gate_ref)


def _route_rows(rows, logits, e_ref, gate_ref):
    lane = lax.broadcasted_iota(I32, logits.shape, 1)
    big = jnp.int32(LANES)

    def softmax_masked(mask):
        z = jnp.where(mask, logits, NEG)
        m = jnp.max(z, axis=-1, keepdims=True)
        e = jnp.where(mask, jnp.exp(z - m), 0.0)
        return e / jnp.sum(e, axis=-1, keepdims=True)

    def top1(vals, mask):
        v = jnp.where(mask, vals, -1.0)
        best = jnp.max(v, axis=-1, keepdims=True)
        idx = jnp.min(jnp.where(mask & (v == best), lane, big), axis=-1, keepdims=True)
        return best, idx

    gmask = lane < N_GROUPS_E
    p_grp = softmax_masked(gmask)
    p_g, g_sel = top1(p_grp, gmask)
    lo = N_GROUPS_E + EXPERTS_PER_GROUP * g_sel
    emask = (lane >= lo) & (lane < lo + EXPERTS_PER_GROUP)
    p_exp = softmax_masked(emask)
    p1, i1 = top1(p_exp, emask)
    p2, i2 = top1(p_exp, emask & (lane != i1))
    denom = p1 + p2
    g1 = p_g * p1 / denom
    g2 = p_g * p2 / denom
    e_ref[rows, :] = jnp.where(lane == 0, i1 - N_GROUPS_E,
                               jnp.where(lane == 1, i2 - N_GROUPS_E, 0))
    gate_ref[rows, :] = jnp.where(lane == 0, g1, jnp.where(lane == 1, g2, 0.0))


def _out_proj(x2d, ntok, o_a, o_bc, mod, mod_row_fn, g_ffn, w_out, w_r, b_r, layer):
    nblk = ntok // TM_OUT
    full = lambda shape: pl.BlockSpec((None,) + shape, lambda i: (layer,) + (0,) * len(shape))
    return pl.pallas_call(
        _out_proj_kernel,
        grid=(nblk,),
        in_specs=[
            pl.BlockSpec((TM_OUT, D_MODEL), lambda i: (i, 0)),
            pl.BlockSpec((TM_OUT, D_A), lambda i: (i, 0)),
            pl.BlockSpec((TM_OUT, D_B + D_C), lambda i: (i, 0)),
            pl.BlockSpec((None, None, N_MOD, D_MODEL), lambda i: (layer, mod_row_fn(i), 0, 0)),
            full((1, D_MODEL)), full((D_MODEL, D_MODEL)), full((D_MODEL, 2 * LANES)), full((1, LANES)),
        ],
        out_specs=[
            pl.BlockSpec((TM_OUT, D_MODEL), lambda i: (i, 0)),
            pl.BlockSpec((TM_OUT,) + ROW_TILE, lambda i: (i, 0, 0)),
            pl.BlockSpec((TM_OUT, LANES), lambda i: (i, 0)),
            pl.BlockSpec((TM_OUT, LANES), lambda i: (i, 0)),
        ],
        out_shape=[
            jax.ShapeDtypeStruct((ntok, D_MODEL), F32),
            jax.ShapeDtypeStruct((ntok,) + ROW_TILE, U32),
            jax.ShapeDtypeStruct((ntok, LANES), I32),
            jax.ShapeDtypeStruct((ntok, LANES), F32),
        ],
        compiler_params=_params(("arbitrary",)),
        name="out_proj_router",
    )(x2d, o_a, o_bc, mod, g_ffn, w_out, w_r, b_r)


def _route_plan(e_flat):
    n_assign = e_flat.shape[0]
    onehot = (e_flat[:, None] == jnp.arange(N_EXPERTS, dtype=I32)[None, :]).astype(I32)
    csum = jnp.cumsum(onehot, axis=0)
    counts = csum[-1]
    rank = jnp.sum(onehot * (csum - 1), axis=1)
    padded = (counts + MOE_BM - 1) // MOE_BM * MOE_BM
    pad_end = jnp.cumsum(padded)
    pad_start = pad_end - padded
    pos = jnp.sum(onehot * pad_start[None, :], axis=1) + rank
    n_blocks = n_assign // MOE_BM + N_EXPERTS
    starts = jnp.arange(n_blocks, dtype=I32) * MOE_BM
    block_e = jnp.minimum(jnp.sum((starts[:, None] >= pad_end[None, :]).astype(I32), axis=1),
                          N_EXPERTS - 1).astype(I32)
    n_used = (pad_end[-1] // MOE_BM).astype(I32).reshape(1)
    fill_start = jnp.where(padded > 0, pad_end - MOE_BM, -1).astype(I32)
    has_rows = padded > 0
    block_first = jnp.concatenate(
        [jnp.ones((1,), I32), (block_e[1:] != block_e[:-1]).astype(I32)])
    ids = jnp.arange(N_EXPERTS, dtype=I32)
    later = jnp.where(has_rows[None, :] & (ids[None, :] > ids[:, None]), ids[None, :], N_EXPERTS)
    next_e = jnp.min(later, axis=1)
    next_e = jnp.where(next_e < N_EXPERTS, next_e, -1).astype(I32)
    buf_e = ((jnp.cumsum(has_rows.astype(I32)) - 1) % 2).astype(I32)
    plan = (block_e, n_used, block_first, next_e, buf_e)
    return pos.astype(I32), plan, n_used, fill_start, n_blocks


def _dispatch_kernel(fill_ref, nu_ref, pos_ref, hp_ref, hs_ref, xs_ref, zero_scr, sem, fill_sem,
                     *, n_ctx_blocks, n_blocks):
    i = pl.program_id(0)

    @pl.when(i == 0)
    def _():
        zero_scr[...] = jnp.zeros_like(zero_scr)

        def fill(start):
            start = pl.multiple_of(start, MOE_BM)
            return pltpu.make_async_copy(zero_scr, xs_ref.at[pl.ds(start, MOE_BM)], fill_sem)

        def expert_fill(e, carry):
            pl.when(fill_ref[e] >= 0)(lambda: fill(fill_ref[e]).start())
            return carry

        def expert_wait(e, carry):
            pl.when(fill_ref[e] >= 0)(lambda: fill(0).wait())
            return carry

        def tail_fill(b, carry):
            fill(b * MOE_BM).start()
            return carry

        def tail_wait(b, carry):
            fill(0).wait()
            return carry

        lax.fori_loop(0, N_EXPERTS, expert_fill, 0)
        lax.fori_loop(nu_ref[0], n_blocks, tail_fill, 0)
        lax.fori_loop(0, N_EXPERTS, expert_wait, 0)
        lax.fori_loop(nu_ref[0], n_blocks, tail_wait, 0)

    def scatter(h_ref):
        def issue(r, carry):
            for k in range(TOP_K):
                slot = pos_ref[0, 0, TOP_K * r + k]
                pltpu.make_async_copy(h_ref.at[r], xs_ref.at[slot], sem).start(priority=k)
            return carry

        lax.fori_loop(0, TM_ROW, issue, 0, unroll=8)
        for _ in range(TOP_K):
            pltpu.make_async_copy(h_ref, h_ref, sem).wait()

    pl.when(i < n_ctx_blocks)(lambda: scatter(hp_ref))
    pl.when(i >= n_ctx_blocks)(lambda: scatter(hs_ref))


def _dispatch(fill_start, n_used, pos3, hp, hs, n_blocks):
    nbp = hp.shape[0] // TM_ROW
    nbs = hs.shape[0] // TM_ROW
    grid_spec = pltpu.PrefetchScalarGridSpec(
        num_scalar_prefetch=2,
        grid=(nbp + nbs,),
        in_specs=[
            pl.BlockSpec((1, 1, TOP_K * TM_ROW), lambda i, fs, nu: (i, 0, 0),
                         memory_space=pltpu.SMEM),
            pl.BlockSpec((TM_ROW,) + ROW_TILE, lambda i, fs, nu: (jnp.minimum(i, nbp - 1), 0, 0)),
            pl.BlockSpec((TM_ROW,) + ROW_TILE, lambda i, fs, nu: (jnp.maximum(i - nbp, 0), 0, 0)),
        ],
        out_specs=pl.BlockSpec(memory_space=pl.ANY),
        scratch_shapes=[pltpu.VMEM((MOE_BM,) + ROW_TILE, U32), pltpu.SemaphoreType.DMA(()),
                        pltpu.SemaphoreType.DMA(())],
    )
    return pl.pallas_call(
        functools.partial(_dispatch_kernel, n_ctx_blocks=nbp, n_blocks=n_blocks),
        grid_spec=grid_spec,
        out_shape=jax.ShapeDtypeStruct((n_blocks * MOE_BM,) + ROW_TILE, U32),
        compiler_params=_params(("arbitrary",)),
        name="moe_dispatch",
    )(fill_start, n_used, pos3, hp, hs)


def _experts_kernel(be_ref, nu_ref, first_ref, next_ref, buf_ref, xs_ref, wg_hbm, wu_hbm, wd_hbm,
                    ys_ref, wg_buf, wu_buf, wd_buf, sem, *, layer):
    i = pl.program_id(0)

    def weight_copies(e, b):
        return [pltpu.make_async_copy(src.at[layer, e], dst.at[b], sem.at[b])
                for src, dst in ((wg_hbm, wg_buf), (wu_hbm, wu_buf), (wd_hbm, wd_buf))]

    @pl.when(i < nu_ref[0])
    def _():
        e = be_ref[i]
        b = buf_ref[e]

        @pl.when(i == 0)
        def _():
            for cp in weight_copies(e, b):
                cp.start()

        @pl.when(first_ref[i] == 1)
        def _():
            nxt = next_ref[e]

            @pl.when(nxt >= 0)
            def _():
                for cp in weight_copies(nxt, 1 - b):
                    cp.start()

            for cp in weight_copies(e, b):
                cp.wait()

        x = _unpack_bf16_pairs(xs_ref[...].reshape(MOE_BM, D_PACK), BF16)
        g = _dot(x, wg_buf[b].astype(BF16))
        u = _dot(x, wu_buf[b].astype(BF16))
        a = (g * jax.nn.sigmoid(g) * u).astype(BF16)
        y = _dot(a, wd_buf[b].astype(BF16)).astype(BF16)
        ys_ref[...] = _pack_bf16_pairs(y).reshape(ys_ref.shape)

    @pl.when(i >= nu_ref[0])
    def _():
        ys_ref[...] = jnp.zeros_like(ys_ref)


def _experts(plan, xs, wg, wu, wd, layer, n_blocks):
    grid_spec = pltpu.PrefetchScalarGridSpec(
        num_scalar_prefetch=len(plan),
        grid=(n_blocks,),
        in_specs=[
            pl.BlockSpec((MOE_BM,) + ROW_TILE,
                         lambda i, be, nu, *_: (jnp.minimum(i, nu[0] - 1), 0, 0)),
            pl.BlockSpec(memory_space=pl.ANY),
            pl.BlockSpec(memory_space=pl.ANY),
            pl.BlockSpec(memory_space=pl.ANY),
        ],
        out_specs=pl.BlockSpec((MOE_BM,) + ROW_TILE, lambda i, *_: (i, 0, 0)),
        scratch_shapes=[
            pltpu.VMEM((2, D_MODEL, D_EXPERT), F32),
            pltpu.VMEM((2, D_MODEL, D_EXPERT), F32),
            pltpu.VMEM((2, D_EXPERT, D_MODEL), F32),
            pltpu.SemaphoreType.DMA((2,)),
        ],
    )
    return pl.pallas_call(
        functools.partial(_experts_kernel, layer=layer),
        grid_spec=grid_spec,
        out_shape=jax.ShapeDtypeStruct(xs.shape, U32),
        compiler_params=_params(("arbitrary",)),
        name="moe_experts",
    )(*plan, xs, wg, wu, wd)


def _combine_kernel(pos_ref, pos_next_ref, x1_ref, gate_ref, mod_ref, gfin_ref, ys_ref, o_ref,
                    ybuf, sem, *, final_norm):
    i = pl.program_id(0)
    buf = i % 2

    def issue(p_ref, b):
        def body(r, carry):
            for k in range(TOP_K):
                slot = p_ref[0, 0, TOP_K * r + k]
                pltpu.make_async_copy(
                    ys_ref.at[slot], ybuf.at[b, k, r], sem.at[b]).start(priority=k)
            return carry
        lax.fori_loop(0, TM_ROW, body, 0, unroll=8)

    @pl.when(i == 0)
    def _():
        issue(pos_ref, 0)

    @pl.when(i + 1 < pl.num_programs(0))
    def _():
        issue(pos_next_ref, 1 - buf)

    for k in range(TOP_K):
        pltpu.make_async_copy(ybuf.at[buf, k], ybuf.at[buf, k], sem.at[buf]).wait()
    y0 = _unpack_bf16_pairs(ybuf[buf, 0].reshape(TM_ROW, D_PACK), F32)
    y1 = _unpack_bf16_pairs(ybuf[buf, 1].reshape(TM_ROW, D_PACK), F32)
    f = gate_ref[:, 0:1] * y0 + gate_ref[:, 1:2] * y1
    x2 = x1_ref[...] + mod_ref[5:6, :] * f
    if final_norm:
        x2 = _rms(x2, gfin_ref[...])
    o_ref[...] = x2


def _combine(pos3, x1, gate, mod, mod_row_fn, g_final, ys, layer, final_norm):
    ntok = x1.shape[0]
    nblk = ntok // TM_ROW
    return pl.pallas_call(
        functools.partial(_combine_kernel, final_norm=final_norm),
        grid=(nblk,),
        in_specs=[
            pl.BlockSpec((1, 1, TOP_K * TM_ROW), lambda i: (i, 0, 0), memory_space=pltpu.SMEM),
            pl.BlockSpec((1, 1, TOP_K * TM_ROW), lambda i: (jnp.minimum(i + 1, nblk - 1), 0, 0),
                         memory_space=pltpu.SMEM),
            pl.BlockSpec((TM_ROW, D_MODEL), lambda i: (i, 0)),
            pl.BlockSpec((TM_ROW, LANES), lambda i: (i, 0)),
            pl.BlockSpec((None, None, N_MOD, D_MODEL), lambda i: (layer, mod_row_fn(i), 0, 0)),
            pl.BlockSpec((1, D_MODEL), lambda i: (0, 0)),
            pl.BlockSpec(memory_space=pl.ANY),
        ],
        out_specs=pl.BlockSpec((TM_ROW, D_MODEL), lambda i: (i, 0)),
        out_shape=jax.ShapeDtypeStruct((ntok, D_MODEL), F32),
        scratch_shapes=[pltpu.VMEM((2, TOP_K, TM_ROW) + ROW_TILE, U32),
                        pltpu.SemaphoreType.DMA((2,))],
        compiler_params=_params(("arbitrary",)),
        name="moe_combine",
    )(pos3, pos3, x1, gate, mod, g_final, ys)


def kernel(x_prompt, x_sample, cache_k, cache_v, c, c_ctx, w_mod, b_mod, g_mix, g_ffn, w_in, rpb,
           g_sgu, w_sgu, b_sgu, w_pool, s_pool, g_branch, w_out, w_rg, b_rg, w_re, b_re,
           w_e_gate, w_e_up, w_e_down, g_final):
    n_ctx, seq, _ = x_prompt.shape
    n_lat, lat_seq, _ = x_sample.shape
    ntok_p = n_ctx * seq
    ntok_s = n_lat * lat_seq
    rows = lat_seq // GRID_W

    n_mod_rows = -(-(1 + n_lat) // SUBLANES) * SUBLANES
    c_all = jnp.concatenate(
        [c_ctx[None], c, jnp.zeros((n_mod_rows - 1 - n_lat, D_MODEL), F32)], axis=0)
    mod = _modulation(c_all, w_mod, b_mod).reshape(DEPTH, n_mod_rows, N_MOD, D_MODEL)

    xp = x_prompt.reshape(ntok_p, D_MODEL)
    xs_lat = x_sample.reshape(ntok_s, D_MODEL)

    def lat_mod_row(tm):
        return lambda i: 1 + (i * tm) // lat_seq

    ctx_mod_row = lambda i: 0

    w_in_b = w_in.astype(BF16)
    w_out_b = w_out.astype(BF16)
    w_sgu_b = w_sgu.astype(BF16)
    w_pool_b = w_pool.astype(BF16)

    gmix = g_mix[:, None, :]
    gffn = g_ffn[:, None, :]
    gb_a = g_branch[:, None, :D_A]
    gb_bc = g_branch[:, None, D_A:]
    gsgu = g_sgu[:, None, :]
    bsgu_t = jnp.swapaxes(b_sgu, 1, 2)
    spool = s_pool[:, None, :]
    n_route = N_GROUPS_E + N_EXPERTS
    w_r = jnp.concatenate(
        [w_rg, w_re, jnp.zeros((DEPTH, D_MODEL, LANES - n_route), F32)], axis=2)
    w_r_hi = lax.bitcast_convert_type(
        lax.bitcast_convert_type(w_r, U32) & jnp.uint32(0xFFFF0000), F32)
    w_r = jnp.concatenate([w_r_hi.astype(BF16), (w_r - w_r_hi).astype(BF16)], axis=2)
    b_r = jnp.concatenate(
        [b_rg, b_re, jnp.zeros((DEPTH, LANES - n_route), F32)], axis=1)[:, None, :]
    bias_tab = _local_bias_table(rpb.reshape((DEPTH * N_HEADS,) + rpb.shape[2:]), rows)

    cache_shape = (n_ctx, DEPTH, N_HEADS, seq, HEAD_DIM)
    caches = (jnp.zeros(cache_shape, F32), jnp.zeros(cache_shape, F32))
    for l in range(DEPTH):
        qkv_p, ugp_p, *caches = _in_proj(
            xp, ntok_p, mod, ctx_mod_row, gmix, w_in_b, l, seq, True, tuple(caches))
        oa_p = _ctx_attention(qkv_p, gb_a, l, seq)
        obc_p = _mixers(ugp_p, gsgu, w_sgu_b, bsgu_t, w_pool_b, spool, gb_bc, l, seq)
        x1p, hp, ep, gatep = _out_proj(
            xp, ntok_p, oa_p, obc_p, mod, ctx_mod_row, gffn, w_out_b, w_r, b_r, l)

        q_s, kv_s, ugp_s = _in_proj(
            xs_lat, ntok_s, mod, lat_mod_row(TM_IN), gmix, w_in_b, l, lat_seq, False)
        oa_s = _nbr_attention(q_s, kv_s, cache_k, cache_v, l, bias_tab, gb_a, n_lat, lat_seq)
        obc_s = _mixers(ugp_s, gsgu, w_sgu_b, bsgu_t, w_pool_b, spool, gb_bc, l, lat_seq)
        x1s, hs, es, gates = _out_proj(
            xs_lat, ntok_s, oa_s, obc_s, mod, lat_mod_row(TM_OUT), gffn, w_out_b, w_r, b_r, l)

        e_flat = jnp.concatenate([ep[:, :TOP_K].reshape(-1), es[:, :TOP_K].reshape(-1)])
        pos, plan, n_used, fill_start, n_blocks = _route_plan(e_flat)
        pos3 = pos.reshape((ntok_p + ntok_s) // TM_ROW, 1, TOP_K * TM_ROW)
        pos_p = pos3[:ntok_p // TM_ROW]
        pos_s = pos3[ntok_p // TM_ROW:]
        slots = _dispatch(fill_start, n_used, pos3, hp, hs, n_blocks)
        ys = _experts(plan, slots, w_e_gate, w_e_up, w_e_down, l, n_blocks)
        last = l == DEPTH - 1
        gfin = g_final[None]
        xp = _combine(pos_p, x1p, gatep, mod, ctx_mod_row, gfin, ys, l, last)
        xs_lat = _combine(pos_s, x1s, gates, mod, lat_mod_row(TM_ROW), gfin, ys, l, last)

    y_prompt = xp.reshape(n_ctx, seq, D_MODEL)
    y_sample = xs_lat.reshape(n_lat, lat_seq, D_MODEL)
    return (y_prompt, y_sample, caches[0], caches[1])
```

```python
import functools

import jax
import numpy as np
import jax.numpy as jnp
from jax import lax
from jax.experimental import pallas as pl
from jax.experimental.pallas import tpu as pltpu

F32 = jnp.float32
BF16 = jnp.bfloat16
I32 = jnp.int32
U32 = jnp.uint32

D_MODEL = 2048
DEPTH = 2
GRID_W = 64
HEAD_DIM = 128
D_A = 1024
N_HEADS = 8
WIN_H = 8
WIN_W = 16
D_B = 512
N_GROUPS_B = 4
CHUNK = 128
D_C = 512
POOL_WINDOWS = (2, 4, 8, 16)
IN_WIDTH = 3 * D_A + 2 * D_B + D_C
N_GROUPS_E = 4
EXPERTS_PER_GROUP = 8
N_EXPERTS = 32
TOP_K = 2
D_EXPERT = 512
N_MOD = 6
EPS = 1e-6
NEG = -1e30
LOG2E = 1.4426950408889634
SCALE_LOG2E = HEAD_DIM ** -0.5 * LOG2E

LANES = 128
SUBLANES = 8
VMEM_LIMIT_BYTES = 56 * 1024 * 1024

TM_IN = 1024
TN_IN = 512
IN_CHUNK = 256
W_SLOTS = 3
OUT_CHUNK = 512
TM_OUT = 512
TM_MIX = 512
Q_ROWS = 8
K_ROWS = 16
MOE_BM = 512
D_PACK = D_MODEL // 2
ROW_TILE = (SUBLANES, LANES)
assert D_PACK == SUBLANES * LANES
TM_ROW = 512
COMBINE_CHUNK = 128
POOL_HALO = 8

COL_BLOCKS = ((0, 24, 0), (24, 16, 16), (40, 24, 32))
KEY_COLS = 32


def _params(sem):
    return pltpu.CompilerParams(dimension_semantics=sem, vmem_limit_bytes=VMEM_LIMIT_BYTES)


def _dot(a, b):
    return jnp.dot(a, b, preferred_element_type=F32)


def _dot_nt(a, b):
    return lax.dot_general(a, b, (((1,), (1,)), ((), ())), preferred_element_type=F32)


def _rms(x, g):
    return x * lax.rsqrt(jnp.mean(x * x, axis=-1, keepdims=True) + EPS) * g


def _pack_bf16_pairs(x):
    n = x.shape[1] // 2
    lo = lax.bitcast_convert_type(x[:, :n].astype(F32), U32) >> 16
    hi = lax.bitcast_convert_type(x[:, n:].astype(F32), U32) & jnp.uint32(0xFFFF0000)
    return hi | lo


def _unpack_bf16_pairs(w, dtype):
    lo = lax.bitcast_convert_type(w << 16, F32).astype(dtype)
    hi = lax.bitcast_convert_type(w & jnp.uint32(0xFFFF0000), F32).astype(dtype)
    return jnp.concatenate([lo, hi], axis=1)


def _mod_kernel(c_ref, w_ref, b_ref, o_ref):
    c = c_ref[...]
    s = c * jax.nn.sigmoid(c)
    n = s.shape[0]
    w = w_ref[...]
    s_hi = s.astype(BF16)
    s_lo = (s - s_hi.astype(F32)).astype(BF16)
    w_hi = w.astype(BF16)
    w_lo = (w - w_hi.astype(F32)).astype(BF16)
    r = _dot(jnp.concatenate([s_hi, s_lo], axis=0), w_hi)
    o_ref[...] = r[:n] + r[n:] + _dot(s_hi, w_lo) + b_ref[...]


def _modulation(c_all, w_mod, b_mod):
    tn = 1024
    nrow = c_all.shape[0]
    width = w_mod.shape[2]
    return pl.pallas_call(
        _mod_kernel,
        grid=(DEPTH, width // tn),
        in_specs=[
            pl.BlockSpec((nrow, D_MODEL), lambda l, j: (0, 0)),
            pl.BlockSpec((None, D_MODEL, tn), lambda l, j: (l, 0, j)),
            pl.BlockSpec((None, 1, tn), lambda l, j: (l, 0, j)),
        ],
        out_specs=pl.BlockSpec((None, nrow, tn), lambda l, j: (l, 0, j)),
        out_shape=jax.ShapeDtypeStruct((DEPTH, nrow, width), F32),
        compiler_params=_params(("arbitrary", "arbitrary")),
        name="modulation",
    )(c_all, w_mod, b_mod.reshape(DEPTH, 1, width))


def _in_proj_kernel(x_hbm, mod_ref, g_ref, w_hbm, *refs, ctx, n_aliased, layer, nblk):
    refs = refs[n_aliased:]
    if ctx:
        qkv_ref, ugp_ref, kc_ref, vc_ref, h_scr, xbuf, wbuf, xsem, wsem = refs
    else:
        q_ref, kv_ref, ugp_ref, h_scr, xbuf, wbuf, xsem, wsem = refs
    i = pl.program_id(0)
    j = pl.program_id(1)
    ncol = IN_WIDTH // TN_IN
    step = i * ncol + j
    heads = TN_IN // HEAD_DIM

    def x_copy(blk):
        slot = blk % 2
        return pltpu.make_async_copy(
            x_hbm.at[pl.ds(pl.multiple_of(blk * TM_IN, TM_IN), TM_IN)], xbuf.at[slot],
            xsem.at[slot])

    def w_copy(s):
        slot = s % W_SLOTS
        col = pl.multiple_of((s % ncol) * TN_IN, TN_IN)
        return pltpu.make_async_copy(
            w_hbm.at[layer, :, pl.ds(col, TN_IN)], wbuf.at[slot], wsem.at[slot])

    @pl.when(step == 0)
    def _():
        x_copy(0).start()
        for s in range(W_SLOTS - 1):
            w_copy(s).start()

    @pl.when((j == 0) & (i + 1 < nblk))
    def _():
        x_copy(i + 1).start()

    @pl.when(step + W_SLOTS - 1 < nblk * ncol)
    def _():
        w_copy(step + W_SLOTS - 1).start()

    @pl.when(j == 0)
    def _():
        x_copy(i).wait()

    w_copy(step).wait()
    x_ref = xbuf.at[i % 2]
    w_ref = wbuf.at[step % W_SLOTS]

    def head_store(ref, dtype):
        def store(m, rows, acc):
            for hh in range(heads):
                ref[hh, rows, :] = acc[:, hh * HEAD_DIM:(hh + 1) * HEAD_DIM].astype(dtype)
        return store

    def cache_store(ref):
        seq = ref.shape[2]
        per_chunk = IN_CHUNK // seq

        def store(m, rows, acc):
            for s in range(per_chunk):
                for hh in range(heads):
                    ref[m * per_chunk + s, hh] = acc[s * seq:(s + 1) * seq,
                                                     hh * HEAD_DIM:(hh + 1) * HEAD_DIM]
        return store

    def ugp_store(m, rows, acc):
        ugp_ref[rows, :] = acc

    def run(normalize, stores):
        for m in range(TM_IN // IN_CHUNK):
            rows = pl.ds(m * IN_CHUNK, IN_CHUNK)
            if normalize:
                h = _rms(x_ref[rows, :], g_ref[...]) * (1.0 + mod_ref[1:2, :]) + mod_ref[0:1, :]
                hb = h.astype(BF16)
                h_scr[rows, :] = hb
            else:
                hb = h_scr[rows, :]
            acc = _dot(hb, w_ref[...])
            for store in stores:
                store(m, rows, acc)

    def case(cond, normalize, stores):
        pl.when(cond)(lambda: run(normalize, stores))

    if ctx:
        case(j == 0, True, [head_store(qkv_ref, BF16)])
        case(j == 1, False, [head_store(qkv_ref, BF16)])
        case((j >= 2) & (j < 4), False, [head_store(qkv_ref, BF16), cache_store(kc_ref)])
        case((j >= 4) & (j < 6), False, [head_store(qkv_ref, BF16), cache_store(vc_ref)])
    else:
        case(j == 0, True, [head_store(q_ref, F32)])
        case(j == 1, False, [head_store(q_ref, F32)])
        case((j >= 2) & (j < 6), False, [head_store(kv_ref, BF16)])
    case(j >= 6, False, [ugp_store])


def _in_proj(x2d, ntok, mod, mod_row_fn, g_mix, w_in, layer, seq_len, ctx, caches=()):
    nblk = ntok // TM_IN
    ncol = IN_WIDTH // TN_IN
    heads = TN_IN // HEAD_DIM
    hm = lambda lo, n: (lambda i, j: (jnp.clip(j - lo, 0, n - 1), i, 0))
    hspec = lambda lo, n: pl.BlockSpec((heads, TM_IN, HEAD_DIM), hm(lo, n))
    ugp_shape = jax.ShapeDtypeStruct((ntok, 2 * D_B + D_C), F32)
    ugp_spec = pl.BlockSpec((TM_IN, TN_IN), lambda i, j: (i, jnp.clip(j - 6, 0, 2)))
    if ctx:
        nseq = ntok // seq_len
        spb = TM_IN // seq_len
        cache_shape = jax.ShapeDtypeStruct((nseq, DEPTH, N_HEADS, seq_len, HEAD_DIM), F32)
        cache_spec = lambda lo: pl.BlockSpec(
            (spb, None, heads, seq_len, HEAD_DIM),
            lambda i, j: (i, layer, jnp.clip(j - lo, 0, 1), 0, 0))
        out_shape = [jax.ShapeDtypeStruct((3 * N_HEADS, ntok, HEAD_DIM), BF16), ugp_shape,
                     cache_shape, cache_shape]
        out_specs = [hspec(0, 6), ugp_spec, cache_spec(2), cache_spec(4)]
    else:
        out_shape = [jax.ShapeDtypeStruct((N_HEADS, ntok, HEAD_DIM), F32),
                     jax.ShapeDtypeStruct((2 * N_HEADS, ntok, HEAD_DIM), BF16), ugp_shape]
        out_specs = [hspec(0, 2), hspec(2, 4), ugp_spec]
    return pl.pallas_call(
        functools.partial(_in_proj_kernel, ctx=ctx, n_aliased=len(caches), layer=layer, nblk=nblk),
        grid=(nblk, ncol),
        in_specs=[
            pl.BlockSpec(memory_space=pl.ANY),
            pl.BlockSpec((None, None, N_MOD, D_MODEL), lambda i, j: (layer, mod_row_fn(i), 0, 0)),
            pl.BlockSpec((None, 1, D_MODEL), lambda i, j: (layer, 0, 0)),
            pl.BlockSpec(memory_space=pl.ANY),
        ] + [pl.BlockSpec(memory_space=pl.ANY)] * len(caches),
        out_specs=out_specs,
        out_shape=out_shape,
        scratch_shapes=[
            pltpu.VMEM((TM_IN, D_MODEL), BF16),
            pltpu.VMEM((2, TM_IN, D_MODEL), F32),
            pltpu.VMEM((W_SLOTS, D_MODEL, TN_IN), BF16),
            pltpu.SemaphoreType.DMA((2,)),
            pltpu.SemaphoreType.DMA((W_SLOTS,)),
        ],
        input_output_aliases={4 + n: 2 + n for n in range(len(caches))},
        compiler_params=_params(("arbitrary", "arbitrary")),
        name="in_proj_ctx" if ctx else "in_proj_lat",
    )(x2d, mod, g_mix, w_in, *caches)


def _write_normed_heads(o_scr, gb_ref, o_ref):
    ss = None
    for h in range(N_HEADS):
        oh = o_scr[h]
        t = jnp.sum(oh * oh, axis=-1, keepdims=True)
        ss = t if ss is None else ss + t
    r = lax.rsqrt(ss * (1.0 / D_A) + EPS)
    for h in range(N_HEADS):
        sl = slice(h * HEAD_DIM, (h + 1) * HEAD_DIM)
        o_ref[:, sl] = (o_scr[h] * r * gb_ref[:, sl]).astype(o_ref.dtype)


def _ctx_attn_kernel(q_ref, k_ref, v_ref, gb_ref, o_ref, o_scr):
    group = 2

    def heads(g, carry):
        hs = [g * group + n for n in range(group)]
        scores = [_dot_nt(q_ref[h], k_ref[h]) * SCALE_LOG2E for h in hs]
        probs = []
        for s in scores:
            p = jnp.exp2(s - jnp.max(s, axis=-1, keepdims=True))
            probs.append((p.astype(BF16), jnp.sum(p, axis=-1, keepdims=True)))
        for h, (p, l) in zip(hs, probs):
            o_scr[h] = _dot(p, v_ref[h]) / l
        return carry

    lax.fori_loop(0, N_HEADS // group, heads, 0)
    _write_normed_heads(o_scr, gb_ref, o_ref)


def _ctx_attention(qkv, gb_a, layer, seq_len):
    ntok = qkv.shape[1]
    spec = lambda part: pl.BlockSpec((N_HEADS, seq_len, HEAD_DIM), lambda b: (part, b, 0))
    return pl.pallas_call(
        _ctx_attn_kernel,
        grid=(ntok // seq_len,),
        in_specs=[spec(0), spec(1), spec(2),
                  pl.BlockSpec((None, 1, D_A), lambda b: (layer, 0, 0))],
        out_specs=pl.BlockSpec((seq_len, D_A), lambda b: (b, 0)),
        out_shape=jax.ShapeDtypeStruct((ntok, D_A), BF16),
        scratch_shapes=[pltpu.VMEM((N_HEADS, seq_len, HEAD_DIM), F32)],
        compiler_params=_params(("arbitrary",)),
        name="ctx_attention",
    )(qkv, qkv, qkv, gb_a)


def _bias_table_kernel(sel_ref, rpb_ref, o_ref):
    kr_per_tile = LANES // KEY_COLS
    by_row = jnp.dot(sel_ref[...], rpb_ref[...], preferred_element_type=F32,
                     precision=lax.Precision.HIGHEST)
    for (c0, nc, kc0) in COL_BLOCKS:
        lane = lax.broadcasted_iota(I32, (nc, LANES), 1)
        c = c0 + lax.broadcasted_iota(I32, (nc, LANES), 0)
        cs = jnp.clip(c - WIN_W // 2, 0, GRID_W - WIN_W)
        masks = []
        for k4 in range(kr_per_tile):
            kc = kc0 + lane - k4 * KEY_COLS
            masks.append((lane >= k4 * KEY_COLS) & (lane < (k4 + 1) * KEY_COLS)
                         & (kc >= cs) & (kc < cs + WIN_W))
        for qr in range(Q_ROWS):
            for t in range(K_ROWS // kr_per_tile):
                tile = jnp.full((nc, LANES), NEG, F32)
                for k4 in range(kr_per_tile):
                    row = qr * K_ROWS + t * kr_per_tile + k4
                    x = jnp.broadcast_to(by_row[row:row + 1, :], (nc, LANES))
                    shift = (k4 * KEY_COLS - kc0 - (WIN_W - 1) + c0) % LANES
                    x = pltpu.roll(x, shift, 1, stride=1, stride_axis=0)
                    tile = jnp.where(masks[k4], x * LOG2E, tile)
                o_ref[pl.ds(qr * GRID_W + c0, nc), pl.ds(t * LANES, LANES)] = tile


def _local_bias_table(rpb, rows):
    n_rb = rows // Q_ROWS
    n_dr = 2 * WIN_H - 1
    n_dc = 2 * WIN_W - 1
    n_sel = n_dr + 1
    sel = np.zeros((3, Q_ROWS, K_ROWS, n_sel), np.float32)
    for t, jb in enumerate((0, 1, n_rb - 1)):
        r0 = Q_ROWS * jb
        ks = min(max(r0 - WIN_H // 2, 0), rows - K_ROWS)
        r = r0 + np.arange(Q_ROWS)[:, None]
        kr = ks + np.arange(K_ROWS)[None, :]
        rs = np.clip(r - WIN_H // 2, 0, rows - WIN_H)
        valid = (kr >= rs) & (kr < rs + WIN_H)
        ri = np.where(valid, kr - r + WIN_H - 1, n_dr)
        sel[t] = ri[..., None] == np.arange(n_sel)
    sel = sel.reshape(3, Q_ROWS * K_ROWS, n_sel)
    n_lh = rpb.shape[0]
    rpb_pad = jnp.concatenate(
        [jnp.pad(rpb, ((0, 0), (0, 0), (0, LANES - n_dc))), jnp.full((n_lh, 1, LANES), NEG, F32)],
        axis=1)
    nq = Q_ROWS * GRID_W
    nk = K_ROWS * KEY_COLS
    return pl.pallas_call(
        _bias_table_kernel,
        grid=(n_lh, 3),
        in_specs=[
            pl.BlockSpec((None, Q_ROWS * K_ROWS, n_sel), lambda h, t: (t, 0, 0)),
            pl.BlockSpec((None, n_sel, LANES), lambda h, t: (h, 0, 0)),
        ],
        out_specs=pl.BlockSpec((None, None, nq, nk), lambda h, t: (h, t, 0, 0)),
        out_shape=jax.ShapeDtypeStruct((n_lh, 3, nq, nk), F32),
        compiler_params=_params(("arbitrary", "arbitrary")),
        name="bias_table",
    )(jnp.asarray(sel), rpb_pad)


def _nbr_attn_kernel(q_ref, k0, k1, k2, k3, v0, v1, v2, v3, ck_ref, cv_ref, bias_ref, gb_ref,
                     o_ref, o_scr):
    k_refs = (k0, k1, k2, k3)
    v_refs = (v0, v1, v2, v3)
    rows_per_ref = K_ROWS // len(k_refs)

    def window(refs, h, kc0):
        parts = []
        for kr in range(K_ROWS):
            ref = refs[kr // rows_per_ref]
            parts.append(ref[h, pl.ds((kr % rows_per_ref) * GRID_W + kc0, KEY_COLS), :])
        return jnp.concatenate(parts, axis=0)

    def query_rows(ref, h, c0, nc):
        return jnp.concatenate(
            [ref[h, pl.ds(qr * GRID_W + c0, nc), :] for qr in range(Q_ROWS)], axis=0)

    def head(h, carry):
        kc = ck_ref[h].astype(BF16)
        vc = cv_ref[h].astype(BF16)
        scores = []
        for (c0, nc, kc0) in COL_BLOCKS:
            qi = query_rows(q_ref, h, c0, nc).astype(BF16)
            ki = window(k_refs, h, kc0)
            s_loc = _dot_nt(qi, ki) * SCALE_LOG2E + query_rows(bias_ref, h, c0, nc)
            s_ctx = _dot_nt(qi, kc) * SCALE_LOG2E
            scores.append((s_loc, s_ctx))
        probs = []
        for s_loc, s_ctx in scores:
            m = jnp.maximum(jnp.max(s_loc, axis=-1, keepdims=True),
                            jnp.max(s_ctx, axis=-1, keepdims=True))
            p_loc = jnp.exp2(s_loc - m)
            p_ctx = jnp.exp2(s_ctx - m)
            l = jnp.sum(p_loc, axis=-1, keepdims=True) + jnp.sum(p_ctx, axis=-1, keepdims=True)
            probs.append((p_loc.astype(BF16), p_ctx.astype(BF16), l))
        for (c0, nc, kc0), (p_loc, p_ctx, l) in zip(COL_BLOCKS, probs):
            vi = window(v_refs, h, kc0)
            o = (_dot(p_loc, vi) + _dot(p_ctx, vc)) / l
            for qr in range(Q_ROWS):
                o_scr[h, pl.ds(qr * GRID_W + c0, nc), :] = o[qr * nc:(qr + 1) * nc]
        return carry

    lax.fori_loop(0, N_HEADS, head, 0)
    _write_normed_heads(o_scr, gb_ref, o_ref)


def _nbr_attention(q, kv, cache_k, cache_v, layer, bias_tab, gb_a, n_req, seq_len):
    rows = seq_len // GRID_W
    n_rb = rows // Q_ROWS
    tq = Q_ROWS * GRID_W
    n_kref = 4
    tk = (K_ROWS // n_kref) * GRID_W
    kblk_per_req = seq_len // tk
    ctx_len = cache_k.shape[3]

    def kmap(part, m):
        def f(b, jb):
            start = jnp.clip(2 * jb - 1, 0, kblk_per_req - n_kref)
            return (part, b * kblk_per_req + start + m, 0)
        return f

    kspecs = [pl.BlockSpec((N_HEADS, tk, HEAD_DIM), kmap(0, m)) for m in range(n_kref)]
    vspecs = [pl.BlockSpec((N_HEADS, tk, HEAD_DIM), kmap(1, m)) for m in range(n_kref)]
    cspec = pl.BlockSpec((None, None, N_HEADS, ctx_len, HEAD_DIM), lambda b, jb: (b, layer, 0, 0, 0))
    kind = lambda b, jb: (layer, (jb > 0).astype(I32) + (jb == n_rb - 1).astype(I32), 0, 0)
    return pl.pallas_call(
        _nbr_attn_kernel,
        grid=(n_req, n_rb),
        in_specs=[pl.BlockSpec((N_HEADS, tq, HEAD_DIM), lambda b, jb: (0, b * n_rb + jb, 0))]
        + kspecs + vspecs + [
            cspec, cspec,
            pl.BlockSpec((N_HEADS, None, tq, K_ROWS * KEY_COLS), kind),
            pl.BlockSpec((None, 1, D_A), lambda b, jb: (layer, 0, 0)),
        ],
        out_specs=pl.BlockSpec((tq, D_A), lambda b, jb: (b * n_rb + jb, 0)),
        out_shape=jax.ShapeDtypeStruct((n_req * seq_len, D_A), BF16),
        scratch_shapes=[pltpu.VMEM((N_HEADS, tq, HEAD_DIM), F32)],
        compiler_params=_params(("arbitrary", "arbitrary")),
        name="nbr_attention",
    )(q, *([kv] * (2 * n_kref)), cache_k, cache_v, bias_tab, gb_a)


def _mixers_kernel(u_ref, gv_ref, p_ref, pprev_ref, pnext_ref, gsgu_ref, wsgu_ref, bsgu_ref,
                   wpool_ref, spool_ref, gb_ref, o_ref, ext_scr, *, seq_len):
    i = pl.program_id(0)
    tm = u_ref.shape[0]
    blocks_per_seq = seq_len // tm
    bi = i % blocks_per_seq
    t0 = bi * tm

    gu = jax.nn.gelu(u_ref[...])
    gg = _rms(jax.nn.gelu(gv_ref[...]), gsgu_ref[...]).astype(BF16)
    cols = []
    for g in range(N_GROUPS_B):
        sl = slice(g * CHUNK, (g + 1) * CHUNK)
        chunks = []
        for n in range(tm // CHUNK):
            rows = slice(n * CHUNK, (n + 1) * CHUNK)
            chunks.append(_dot(wsgu_ref[g], gg[rows, sl]) + bsgu_ref[:, g:g + 1])
        cols.append(jnp.concatenate(chunks, axis=0))
    o_b = gu * jnp.concatenate(cols, axis=1)
    o_ref[:, 0:D_B] = _rms(o_b, gb_ref[:, 0:D_B]).astype(o_ref.dtype)

    p = p_ref[...]
    ext_scr[pl.ds(0, POOL_HALO), :] = jnp.where(bi > 0, pprev_ref[...], 0.0)
    ext_scr[pl.ds(POOL_HALO, tm), :] = p
    ext_scr[pl.ds(POOL_HALO + tm, POOL_HALO), :] = jnp.where(
        bi < blocks_per_seq - 1, pnext_ref[...], 0.0)
    t = t0 + lax.broadcasted_iota(I32, (tm, 1), 0)
    outs = []
    for g, w in enumerate(POOL_WINDOWS):
        half = w // 2
        sl = slice(g * CHUNK, (g + 1) * CHUNK)
        acc = None
        for d in range(-half, half):
            part = ext_scr[pl.ds(POOL_HALO + d, tm), sl]
            acc = part if acc is None else acc + part
        cnt = (jnp.minimum(t + half, seq_len) - jnp.maximum(t - half, 0)).astype(F32)
        pooled = acc / cnt - p[:, sl]
        outs.append(_dot(pooled.astype(BF16), wpool_ref[g]))
    o_c = jnp.concatenate(outs, axis=1) * spool_ref[...]
    o_ref[:, D_B:D_B + D_C] = _rms(o_c, gb_ref[:, D_B:D_B + D_C]).astype(o_ref.dtype)


def _mixers(ugp, g_sgu, w_sgu, b_sgu_t, w_pool, s_pool, gb_bc, layer, seq_len):
    ntok = ugp.shape[0]
    tm = min(TM_MIX, seq_len)
    nblk = ntok // tm
    hb = tm // POOL_HALO
    n_halo = ntok // POOL_HALO
    blk = lambda part: pl.BlockSpec((tm, D_B), lambda i: (i, part))
    full = lambda shape: pl.BlockSpec((None,) + shape, lambda i: (layer,) + (0,) * len(shape))
    return pl.pallas_call(
        functools.partial(_mixers_kernel, seq_len=seq_len),
        grid=(nblk,),
        in_specs=[
            blk(0), blk(1), blk(2),
            pl.BlockSpec((POOL_HALO, D_C), lambda i: (jnp.maximum(i * hb - 1, 0), 2)),
            pl.BlockSpec((POOL_HALO, D_C), lambda i: (jnp.minimum((i + 1) * hb, n_halo - 1), 2)),
            full((1, D_B)), full((N_GROUPS_B, CHUNK, CHUNK)), full((CHUNK, N_GROUPS_B)),
            full((len(POOL_WINDOWS), CHUNK, CHUNK)), full((1, D_C)), full((1, D_B + D_C)),
        ],
        out_specs=pl.BlockSpec((tm, D_B + D_C), lambda i: (i, 0)),
        out_shape=jax.ShapeDtypeStruct((ntok, D_B + D_C), BF16),
        scratch_shapes=[pltpu.VMEM((tm + 2 * POOL_HALO, D_C), F32)],
        compiler_params=_params(("arbitrary",)),
        name="mixers",
    )(ugp, ugp, ugp, ugp, ugp, g_sgu, w_sgu, b_sgu_t, w_pool, s_pool, gb_bc)


def _out_proj_kernel(x_ref, oa_ref, obc_ref, mod_ref, g_ref, w_ref, wr_ref, br_ref,
                     x1_ref, h_ref, e_ref, gate_ref):
    chunks = [pl.ds(m * OUT_CHUNK, OUT_CHUNK) for m in range(TM_OUT // OUT_CHUNK)]
    mixes = [_dot(jnp.concatenate([oa_ref[rows, :], obc_ref[rows, :]], axis=1), w_ref[...])
             for rows in chunks]
    splits = []
    for rows, mix in zip(chunks, mixes):
        x1 = x_ref[rows, :] + mod_ref[2:3, :] * mix
        x1_ref[rows, :] = x1
        h = _rms(x1, g_ref[...]) * (1.0 + mod_ref[4:5, :]) + mod_ref[3:4, :]
        h_hi = h.astype(BF16)
        h_ref[rows] = _pack_bf16_pairs(h_hi).reshape((OUT_CHUNK,) + ROW_TILE)
        splits.append((h_hi, (h - h_hi.astype(F32)).astype(BF16)))
    logits = []
    for h_hi, h_lo in splits:
        part = _dot(h_hi, wr_ref[...])
        logits.append((part[:, :LANES] + part[:, LANES:] + _dot(h_lo, wr_ref[:, :LANES]))
                      + br_ref[...])
    for rows, lg in zip(chunks, logits):
        _route_rows(rows, lg, e_ref, gate_ref)


def _route_rows(rows, logits, e_ref, gate_ref):
    lane = lax.broadcasted_iota(I32, logits.shape, 1)
    big = jnp.int32(LANES)

    def softmax_masked(mask):
        z = jnp.where(mask, logits, NEG)
        m = jnp.max(z, axis=-1, keepdims=True)
        e = jnp.where(mask, jnp.exp(z - m), 0.0)
        return e / jnp.sum(e, axis=-1, keepdims=True)

    def top1(vals, mask):
        v = jnp.where(mask, vals, -1.0)
        best = jnp.max(v, axis=-1, keepdims=True)
        idx = jnp.min(jnp.where(mask & (v == best), lane, big), axis=-1, keepdims=True)
        return best, idx

    gmask = lane < N_GROUPS_E
    p_grp = softmax_masked(gmask)
    p_g, g_sel = top1(p_grp, gmask)
    lo = N_GROUPS_E + EXPERTS_PER_GROUP * g_sel
    emask = (lane >= lo) & (lane < lo + EXPERTS_PER_GROUP)
    p_exp = softmax_masked(emask)
    p1, i1 = top1(p_exp, emask)
    p2, i2 = top1(p_exp, emask & (lane != i1))
    denom = p1 + p2
    g1 = p_g * p1 / denom
    g2 = p_g * p2 / denom
    e_ref[rows, :] = jnp.where(lane == 0, i1 - N_GROUPS_E,
                               jnp.where(lane == 1, i2 - N_GROUPS_E, 0))
    gate_ref[rows, :] = jnp.where(lane == 0, g1, jnp.where(lane == 1, g2, 0.0))


def _out_proj(x2d, ntok, o_a, o_bc, mod, mod_row_fn, g_ffn, w_out, w_r, b_r, layer):
    nblk = ntok // TM_OUT
    full = lambda shape: pl.BlockSpec((None,) + shape, lambda i: (layer,) + (0,) * len(shape))
    return pl.pallas_call(
        _out_proj_kernel,
        grid=(nblk,),
        in_specs=[
            pl.BlockSpec((TM_OUT, D_MODEL), lambda i: (i, 0)),
            pl.BlockSpec((TM_OUT, D_A), lambda i: (i, 0)),
            pl.BlockSpec((TM_OUT, D_B + D_C), lambda i: (i, 0)),
            pl.BlockSpec((None, None, N_MOD, D_MODEL), lambda i: (layer, mod_row_fn(i), 0, 0)),
            full((1, D_MODEL)), full((D_MODEL, D_MODEL)), full((D_MODEL, 2 * LANES)), full((1, LANES)),
        ],
        out_specs=[
            pl.BlockSpec((TM_OUT, D_MODEL), lambda i: (i, 0)),
            pl.BlockSpec((TM_OUT,) + ROW_TILE, lambda i: (i, 0, 0)),
            pl.BlockSpec((TM_OUT, LANES), lambda i: (i, 0)),
            pl.BlockSpec((TM_OUT, LANES), lambda i: (i, 0)),
        ],
        out_shape=[
            jax.ShapeDtypeStruct((ntok, D_MODEL), F32),
            jax.ShapeDtypeStruct((ntok,) + ROW_TILE, U32),
            jax.ShapeDtypeStruct((ntok, LANES), I32),
            jax.ShapeDtypeStruct((ntok, LANES), F32),
        ],
        compiler_params=_params(("arbitrary",)),
        name="out_proj_router",
    )(x2d, o_a, o_bc, mod, g_ffn, w_out, w_r, b_r)


def _route_plan(e_flat):
    n_assign = e_flat.shape[0]
    onehot = (e_flat[:, None] == jnp.arange(N_EXPERTS, dtype=I32)[None, :]).astype(I32)
    csum = jnp.cumsum(onehot, axis=0)
    counts = csum[-1]
    rank = jnp.sum(onehot * (csum - 1), axis=1)
    padded = (counts + MOE_BM - 1) // MOE_BM * MOE_BM
    pad_end = jnp.cumsum(padded)
    pad_start = pad_end - padded
    pos = jnp.sum(onehot * pad_start[None, :], axis=1) + rank
    n_blocks = n_assign // MOE_BM + N_EXPERTS
    starts = jnp.arange(n_blocks, dtype=I32) * MOE_BM
    block_e = jnp.minimum(jnp.sum((starts[:, None] >= pad_end[None, :]).astype(I32), axis=1),
                          N_EXPERTS - 1).astype(I32)
    n_used = (pad_end[-1] // MOE_BM).astype(I32).reshape(1)
    fill_start = jnp.where(padded > 0, pad_end - MOE_BM, -1).astype(I32)
    has_rows = padded > 0
    block_first = jnp.concatenate(
        [jnp.ones((1,), I32), (block_e[1:] != block_e[:-1]).astype(I32)])
    ids = jnp.arange(N_EXPERTS, dtype=I32)
    later = jnp.where(has_rows[None, :] & (ids[None, :] > ids[:, None]), ids[None, :], N_EXPERTS)
    next_e = jnp.min(later, axis=1)
    next_e = jnp.where(next_e < N_EXPERTS, next_e, -1).astype(I32)
    buf_e = ((jnp.cumsum(has_rows.astype(I32)) - 1) % 2).astype(I32)
    plan = (block_e, n_used, block_first, next_e, buf_e)
    return pos.astype(I32), plan, n_used, fill_start, n_blocks


def _dispatch_kernel(fill_ref, nu_ref, pos_ref, hp_ref, hs_ref, xs_ref, zero_scr, sem, fill_sem,
                     *, n_ctx_blocks, n_blocks):
    i = pl.program_id(0)

    @pl.when(i == 0)
    def _():
        zero_scr[...] = jnp.zeros_like(zero_scr)

        def fill(start):
            start = pl.multiple_of(start, MOE_BM)
            return pltpu.make_async_copy(zero_scr, xs_ref.at[pl.ds(start, MOE_BM)], fill_sem)

        def expert_fill(e, carry):
            pl.when(fill_ref[e] >= 0)(lambda: fill(fill_ref[e]).start())
            return carry

        def expert_wait(e, carry):
            pl.when(fill_ref[e] >= 0)(lambda: fill(0).wait())
            return carry

        def tail_fill(b, carry):
            fill(b * MOE_BM).start()
            return carry

        def tail_wait(b, carry):
            fill(0).wait()
            return carry

        lax.fori_loop(0, N_EXPERTS, expert_fill, 0)
        lax.fori_loop(nu_ref[0], n_blocks, tail_fill, 0)
        lax.fori_loop(0, N_EXPERTS, expert_wait, 0)
        lax.fori_loop(nu_ref[0], n_blocks, tail_wait, 0)

    def scatter(h_ref):
        def issue(r, carry):
            for k in range(TOP_K):
                slot = pos_ref[0, 0, TOP_K * r + k]
                pltpu.make_async_copy(h_ref.at[r], xs_ref.at[slot], sem).start(priority=k)
            return carry

        lax.fori_loop(0, TM_ROW, issue, 0, unroll=8)
        for _ in range(TOP_K):
            pltpu.make_async_copy(h_ref, h_ref, sem).wait()

    pl.when(i < n_ctx_blocks)(lambda: scatter(hp_ref))
    pl.when(i >= n_ctx_blocks)(lambda: scatter(hs_ref))


def _dispatch(fill_start, n_used, pos3, hp, hs, n_blocks):
    nbp = hp.shape[0] // TM_ROW
    nbs = hs.shape[0] // TM_ROW
    grid_spec = pltpu.PrefetchScalarGridSpec(
        num_scalar_prefetch=2,
        grid=(nbp + nbs,),
        in_specs=[
            pl.BlockSpec((1, 1, TOP_K * TM_ROW), lambda i, fs, nu: (i, 0, 0),
                         memory_space=pltpu.SMEM),
            pl.BlockSpec((TM_ROW,) + ROW_TILE, lambda i, fs, nu: (jnp.minimum(i, nbp - 1), 0, 0)),
            pl.BlockSpec((TM_ROW,) + ROW_TILE, lambda i, fs, nu: (jnp.maximum(i - nbp, 0), 0, 0)),
        ],
        out_specs=pl.BlockSpec(memory_space=pl.ANY),
        scratch_shapes=[pltpu.VMEM((MOE_BM,) + ROW_TILE, U32), pltpu.SemaphoreType.DMA(()),
                        pltpu.SemaphoreType.DMA(())],
    )
    return pl.pallas_call(
        functools.partial(_dispatch_kernel, n_ctx_blocks=nbp, n_blocks=n_blocks),
        grid_spec=grid_spec,
        out_shape=jax.ShapeDtypeStruct((n_blocks * MOE_BM,) + ROW_TILE, U32),
        compiler_params=_params(("arbitrary",)),
        name="moe_dispatch",
    )(fill_start, n_used, pos3, hp, hs)


def _experts_kernel(be_ref, nu_ref, first_ref, next_ref, buf_ref, xs_ref, wg_hbm, wu_hbm, wd_hbm,
                    ys_ref, wg_buf, wu_buf, wd_buf, sem, *, layer):
    i = pl.program_id(0)

    def weight_copies(e, b):
        return [pltpu.make_async_copy(src.at[layer, e], dst.at[b], sem.at[b])
                for src, dst in ((wg_hbm, wg_buf), (wu_hbm, wu_buf), (wd_hbm, wd_buf))]

    @pl.when(i < nu_ref[0])
    def _():
        e = be_ref[i]
        b = buf_ref[e]

        @pl.when(i == 0)
        def _():
            for cp in weight_copies(e, b):
                cp.start()

        @pl.when(first_ref[i] == 1)
        def _():
            nxt = next_ref[e]

            @pl.when(nxt >= 0)
            def _():
                for cp in weight_copies(nxt, 1 - b):
                    cp.start()

            for cp in weight_copies(e, b):
                cp.wait()

        x = _unpack_bf16_pairs(xs_ref[...].reshape(MOE_BM, D_PACK), BF16)
        g = _dot(x, wg_buf[b].astype(BF16))
        u = _dot(x, wu_buf[b].astype(BF16))
        a = (g * jax.nn.sigmoid(g) * u).astype(BF16)
        y = _dot(a, wd_buf[b].astype(BF16)).astype(BF16)
        ys_ref[...] = _pack_bf16_pairs(y).reshape(ys_ref.shape)

    @pl.when(i >= nu_ref[0])
    def _():
        ys_ref[...] = jnp.zeros_like(ys_ref)


def _experts(plan, xs, wg, wu, wd, layer, n_blocks):
    grid_spec = pltpu.PrefetchScalarGridSpec(
        num_scalar_prefetch=len(plan),
        grid=(n_blocks,),
        in_specs=[
            pl.BlockSpec((MOE_BM,) + ROW_TILE,
                         lambda i, be, nu, *_: (jnp.minimum(i, nu[0] - 1), 0, 0)),
            pl.BlockSpec(memory_space=pl.ANY),
            pl.BlockSpec(memory_space=pl.ANY),
            pl.BlockSpec(memory_space=pl.ANY),
        ],
        out_specs=pl.BlockSpec((MOE_BM,) + ROW_TILE, lambda i, *_: (i, 0, 0)),
        scratch_shapes=[
            pltpu.VMEM((2, D_MODEL, D_EXPERT), F32),
            pltpu.VMEM((2, D_MODEL, D_EXPERT), F32),
            pltpu.VMEM((2, D_EXPERT, D_MODEL), F32),
            pltpu.SemaphoreType.DMA((2,)),
        ],
    )
    return pl.pallas_call(
        functools.partial(_experts_kernel, layer=layer),
        grid_spec=grid_spec,
        out_shape=jax.ShapeDtypeStruct(xs.shape, U32),
        compiler_params=_params(("arbitrary",)),
        name="moe_experts",
    )(*plan, xs, wg, wu, wd)


def _combine_kernel(pos_ref, pos_next_ref, x1_ref, gate_ref, mod_ref, gfin_ref, ys_ref, o_ref,
                    ybuf, sem, *, final_norm):
    i = pl.program_id(0)
    buf = i % 2
    last = i == pl.num_programs(0) - 1

    def start_row(p_ref, b, r):
        for k in range(TOP_K):
            slot = p_ref[0, 0, TOP_K * r + k]
            pltpu.make_async_copy(ys_ref.at[slot], ybuf.at[b, k, r], sem.at[b]).start(priority=k)

    def wait_buffer(b):
        for k in range(TOP_K):
            pltpu.make_async_copy(ybuf.at[b, k], ybuf.at[b, k], sem.at[b]).wait()

    @pl.when(i == 0)
    def _():
        lax.fori_loop(0, TM_ROW, lambda r, c: (start_row(pos_ref, 0, r), c)[1], 0, unroll=8)

    wait_buffer(buf)
    for m in range(TM_ROW // COMBINE_CHUNK):
        for r in range(m * COMBINE_CHUNK, (m + 1) * COMBINE_CHUNK):
            start_row(pos_next_ref, 1 - buf, r)
        rows = pl.ds(m * COMBINE_CHUNK, COMBINE_CHUNK)
        y0 = _unpack_bf16_pairs(ybuf[buf, 0, rows].reshape(COMBINE_CHUNK, D_PACK), F32)
        y1 = _unpack_bf16_pairs(ybuf[buf, 1, rows].reshape(COMBINE_CHUNK, D_PACK), F32)
        f = gate_ref[rows, 0:1] * y0 + gate_ref[rows, 1:2] * y1
        x2 = x1_ref[rows, :] + mod_ref[5:6, :] * f
        if final_norm:
            x2 = _rms(x2, gfin_ref[...])
        o_ref[rows, :] = x2

    pl.when(last)(lambda: wait_buffer(1 - buf))


def _combine(pos3, x1, gate, mod, mod_row_fn, g_final, ys, layer, final_norm):
    ntok = x1.shape[0]
    nblk = ntok // TM_ROW
    return pl.pallas_call(
        functools.partial(_combine_kernel, final_norm=final_norm),
        grid=(nblk,),
        in_specs=[
            pl.BlockSpec((1, 1, TOP_K * TM_ROW), lambda i: (i, 0, 0), memory_space=pltpu.SMEM),
            pl.BlockSpec((1, 1, TOP_K * TM_ROW), lambda i: (jnp.minimum(i + 1, nblk - 1), 0, 0),
                         memory_space=pltpu.SMEM),
            pl.BlockSpec((TM_ROW, D_MODEL), lambda i: (i, 0)),
            pl.BlockSpec((TM_ROW, LANES), lambda i: (i, 0)),
            pl.BlockSpec((None, None, N_MOD, D_MODEL), lambda i: (layer, mod_row_fn(i), 0, 0)),
            pl.BlockSpec((1, D_MODEL), lambda i: (0, 0)),
            pl.BlockSpec(memory_space=pl.ANY),
        ],
        out_specs=pl.BlockSpec((TM_ROW, D_MODEL), lambda i: (i, 0)),
        out_shape=jax.ShapeDtypeStruct((ntok, D_MODEL), F32),
        scratch_shapes=[pltpu.VMEM((2, TOP_K, TM_ROW) + ROW_TILE, U32),
                        pltpu.SemaphoreType.DMA((2,))],
        compiler_params=_params(("arbitrary",)),
        name="moe_combine",
    )(pos3, pos3, x1, gate, mod, g_final, ys)


def kernel(x_prompt, x_sample, cache_k, cache_v, c, c_ctx, w_mod, b_mod, g_mix, g_ffn, w_in, rpb,
           g_sgu, w_sgu, b_sgu, w_pool, s_pool, g_branch, w_out, w_rg, b_rg, w_re, b_re,
           w_e_gate, w_e_up, w_e_down, g_final):
    n_ctx, seq, _ = x_prompt.shape
    n_lat, lat_seq, _ = x_sample.shape
    ntok_p = n_ctx * seq
    ntok_s = n_lat * lat_seq
    rows = lat_seq // GRID_W

    n_mod_rows = -(-(1 + n_lat) // SUBLANES) * SUBLANES
    c_all = jnp.concatenate(
        [c_ctx[None], c, jnp.zeros((n_mod_rows - 1 - n_lat, D_MODEL), F32)], axis=0)
    mod = _modulation(c_all, w_mod, b_mod).reshape(DEPTH, n_mod_rows, N_MOD, D_MODEL)

    xp = x_prompt.reshape(ntok_p, D_MODEL)
    xs_lat = x_sample.reshape(ntok_s, D_MODEL)

    def lat_mod_row(tm):
        return lambda i: 1 + (i * tm) // lat_seq

    ctx_mod_row = lambda i: 0

    w_in_b = w_in.astype(BF16)
    w_out_b = w_out.astype(BF16)
    w_sgu_b = w_sgu.astype(BF16)
    w_pool_b = w_pool.astype(BF16)

    gmix = g_mix[:, None, :]
    gffn = g_ffn[:, None, :]
    gb_a = g_branch[:, None, :D_A]
    gb_bc = g_branch[:, None, D_A:]
    gsgu = g_sgu[:, None, :]
    bsgu_t = jnp.swapaxes(b_sgu, 1, 2)
    spool = s_pool[:, None, :]
    n_route = N_GROUPS_E + N_EXPERTS
    w_r = jnp.concatenate(
        [w_rg, w_re, jnp.zeros((DEPTH, D_MODEL, LANES - n_route), F32)], axis=2)
    w_r_hi = lax.bitcast_convert_type(
        lax.bitcast_convert_type(w_r, U32) & jnp.uint32(0xFFFF0000), F32)
    w_r = jnp.concatenate([w_r_hi.astype(BF16), (w_r - w_r_hi).astype(BF16)], axis=2)
    b_r = jnp.concatenate(
        [b_rg, b_re, jnp.zeros((DEPTH, LANES - n_route), F32)], axis=1)[:, None, :]
    bias_tab = _local_bias_table(rpb.reshape((DEPTH * N_HEADS,) + rpb.shape[2:]), rows)

    cache_shape = (n_ctx, DEPTH, N_HEADS, seq, HEAD_DIM)
    caches = (jnp.zeros(cache_shape, F32), jnp.zeros(cache_shape, F32))
    for l in range(DEPTH):
        qkv_p, ugp_p, *caches = _in_proj(
            xp, ntok_p, mod, ctx_mod_row, gmix, w_in_b, l, seq, True, tuple(caches))
        oa_p = _ctx_attention(qkv_p, gb_a, l, seq)
        obc_p = _mixers(ugp_p, gsgu, w_sgu_b, bsgu_t, w_pool_b, spool, gb_bc, l, seq)
        x1p, hp, ep, gatep = _out_proj(
            xp, ntok_p, oa_p, obc_p, mod, ctx_mod_row, gffn, w_out_b, w_r, b_r, l)

        q_s, kv_s, ugp_s = _in_proj(
            xs_lat, ntok_s, mod, lat_mod_row(TM_IN), gmix, w_in_b, l, lat_seq, False)
        oa_s = _nbr_attention(q_s, kv_s, cache_k, cache_v, l, bias_tab, gb_a, n_lat, lat_seq)
        obc_s = _mixers(ugp_s, gsgu, w_sgu_b, bsgu_t, w_pool_b, spool, gb_bc, l, lat_seq)
        x1s, hs, es, gates = _out_proj(
            xs_lat, ntok_s, oa_s, obc_s, mod, lat_mod_row(TM_OUT), gffn, w_out_b, w_r, b_r, l)

        e_flat = jnp.concatenate([ep[:, :TOP_K].reshape(-1), es[:, :TOP_K].reshape(-1)])
        pos, plan, n_used, fill_start, n_blocks = _route_plan(e_flat)
        pos3 = pos.reshape((ntok_p + ntok_s) // TM_ROW, 1, TOP_K * TM_ROW)
        pos_p = pos3[:ntok_p // TM_ROW]
        pos_s = pos3[ntok_p // TM_ROW:]
        slots = _dispatch(fill_start, n_used, pos3, hp, hs, n_blocks)
        ys = _experts(plan, slots, w_e_gate, w_e_up, w_e_down, l, n_blocks)
        last = l == DEPTH - 1
        gfin = g_final[None]
        xp = _combine(pos_p, x1p, gatep, mod, ctx_mod_row, gfin, ys, l, last)
        xs_lat = _combine(pos_s, x1s, gates, mod, lat_mod_row(TM_ROW), gfin, ys, l, last)

    y_prompt = xp.reshape(n_ctx, seq, D_MODEL)
    y_sample = xs_lat.reshape(n_lat, lat_seq, D_MODEL)
    return (y_prompt, y_sample, caches[0], caches[1])
```

```python
import functools

import jax
import numpy as np
import jax.numpy as jnp
from jax import lax
from jax.experimental import pallas as pl
from jax.experimental.pallas import tpu as pltpu

F32 = jnp.float32
BF16 = jnp.bfloat16
I32 = jnp.int32
U32 = jnp.uint32

D_MODEL = 2048
DEPTH = 2
GRID_W = 64
HEAD_DIM = 128
D_A = 1024
N_HEADS = 8
WIN_H = 8
WIN_W = 16
D_B = 512
N_GROUPS_B = 4
CHUNK = 128
D_C = 512
POOL_WINDOWS = (2, 4, 8, 16)
IN_WIDTH = 3 * D_A + 2 * D_B + D_C
N_GROUPS_E = 4
EXPERTS_PER_GROUP = 8
N_EXPERTS = 32
TOP_K = 2
D_EXPERT = 512
N_MOD = 6
EPS = 1e-6
NEG = -1e30
LOG2E = 1.4426950408889634
SCALE_LOG2E = HEAD_DIM ** -0.5 * LOG2E

LANES = 128
SUBLANES = 8
VMEM_LIMIT_BYTES = 56 * 1024 * 1024

TM_IN = 1024
TN_IN = 512
IN_CHUNK = 256
W_SLOTS = 3
OUT_CHUNK = 256
TM_OUT = 512
TM_MIX = 512
Q_ROWS = 8
K_ROWS = 16
NBR_HEAD_GROUP = 2
MOE_BM = 512
D_PACK = D_MODEL // 2
ROW_TILE = (SUBLANES, LANES)
assert D_PACK == SUBLANES * LANES
TM_ROW = 512
COMBINE_CHUNK = 128
POOL_HALO = 8

COL_BLOCKS = ((0, 24, 0), (24, 16, 16), (40, 24, 32))
KEY_COLS = 32


def _params(sem):
    return pltpu.CompilerParams(dimension_semantics=sem, vmem_limit_bytes=VMEM_LIMIT_BYTES)


def _dot(a, b):
    return jnp.dot(a, b, preferred_element_type=F32)


def _dot_nt(a, b):
    return lax.dot_general(a, b, (((1,), (1,)), ((), ())), preferred_element_type=F32)


def _rms(x, g):
    return x * lax.rsqrt(jnp.mean(x * x, axis=-1, keepdims=True) + EPS) * g


def _pack_bf16_pairs(x):
    n = x.shape[1] // 2
    lo = lax.bitcast_convert_type(x[:, :n].astype(F32), U32) >> 16
    hi = lax.bitcast_convert_type(x[:, n:].astype(F32), U32) & jnp.uint32(0xFFFF0000)
    return hi | lo


def _unpack_bf16_pairs(w, dtype):
    lo = lax.bitcast_convert_type(w << 16, F32).astype(dtype)
    hi = lax.bitcast_convert_type(w & jnp.uint32(0xFFFF0000), F32).astype(dtype)
    return jnp.concatenate([lo, hi], axis=1)


def _mod_kernel(c_ref, w_ref, b_ref, o_ref):
    c = c_ref[...]
    s = c * jax.nn.sigmoid(c)
    n = s.shape[0]
    w = w_ref[...]
    s_hi = s.astype(BF16)
    s_lo = (s - s_hi.astype(F32)).astype(BF16)
    w_hi = w.astype(BF16)
    w_lo = (w - w_hi.astype(F32)).astype(BF16)
    r = _dot(jnp.concatenate([s_hi, s_lo], axis=0), w_hi)
    o_ref[...] = r[:n] + r[n:] + _dot(s_hi, w_lo) + b_ref[...]


def _modulation(c_all, w_mod, b_mod):
    tn = 1024
    nrow = c_all.shape[0]
    width = w_mod.shape[2]
    return pl.pallas_call(
        _mod_kernel,
        grid=(DEPTH, width // tn),
        in_specs=[
            pl.BlockSpec((nrow, D_MODEL), lambda l, j: (0, 0)),
            pl.BlockSpec((None, D_MODEL, tn), lambda l, j: (l, 0, j)),
            pl.BlockSpec((None, 1, tn), lambda l, j: (l, 0, j)),
        ],
        out_specs=pl.BlockSpec((None, nrow, tn), lambda l, j: (l, 0, j)),
        out_shape=jax.ShapeDtypeStruct((DEPTH, nrow, width), F32),
        compiler_params=_params(("arbitrary", "arbitrary")),
        name="modulation",
    )(c_all, w_mod, b_mod.reshape(DEPTH, 1, width))


def _in_proj_kernel(x_hbm, mod_ref, g_ref, w_hbm, *refs, ctx, n_aliased, layer, nblk):
    refs = refs[n_aliased:]
    if ctx:
        qkv_ref, ugp_ref, kc_ref, vc_ref, h_scr, xbuf, wbuf, xsem, wsem = refs
    else:
        q_ref, kv_ref, ugp_ref, h_scr, xbuf, wbuf, xsem, wsem = refs
    i = pl.program_id(0)
    j = pl.program_id(1)
    ncol = IN_WIDTH // TN_IN
    step = i * ncol + j
    heads = TN_IN // HEAD_DIM

    def x_copy(blk):
        slot = blk % 2
        return pltpu.make_async_copy(
            x_hbm.at[pl.ds(pl.multiple_of(blk * TM_IN, TM_IN), TM_IN)], xbuf.at[slot],
            xsem.at[slot])

    def w_copy(s):
        slot = s % W_SLOTS
        col = pl.multiple_of((s % ncol) * TN_IN, TN_IN)
        return pltpu.make_async_copy(
            w_hbm.at[layer, :, pl.ds(col, TN_IN)], wbuf.at[slot], wsem.at[slot])

    @pl.when(step == 0)
    def _():
        x_copy(0).start()
        for s in range(W_SLOTS - 1):
            w_copy(s).start()

    @pl.when((j == 0) & (i + 1 < nblk))
    def _():
        x_copy(i + 1).start()

    @pl.when(step + W_SLOTS - 1 < nblk * ncol)
    def _():
        w_copy(step + W_SLOTS - 1).start()

    @pl.when(j == 0)
    def _():
        x_copy(i).wait()

    w_copy(step).wait()
    x_ref = xbuf.at[i % 2]
    w_ref = wbuf.at[step % W_SLOTS]

    def head_store(ref, dtype):
        def store(m, rows, acc):
            for hh in range(heads):
                ref[hh, rows, :] = acc[:, hh * HEAD_DIM:(hh + 1) * HEAD_DIM].astype(dtype)
        return store

    def cache_store(ref):
        seq = ref.shape[2]
        per_chunk = IN_CHUNK // seq

        def store(m, rows, acc):
            for s in range(per_chunk):
                for hh in range(heads):
                    ref[m * per_chunk + s, hh] = acc[s * seq:(s + 1) * seq,
                                                     hh * HEAD_DIM:(hh + 1) * HEAD_DIM]
        return store

    def ugp_store(m, rows, acc):
        ugp_ref[rows, :] = acc

    def run(normalize, stores):
        for m in range(TM_IN // IN_CHUNK):
            rows = pl.ds(m * IN_CHUNK, IN_CHUNK)
            if normalize:
                h = _rms(x_ref[rows, :], g_ref[...]) * (1.0 + mod_ref[1:2, :]) + mod_ref[0:1, :]
                hb = h.astype(BF16)
                h_scr[rows, :] = hb
            else:
                hb = h_scr[rows, :]
            acc = _dot(hb, w_ref[...])
            for store in stores:
                store(m, rows, acc)

    def case(cond, normalize, stores):
        pl.when(cond)(lambda: run(normalize, stores))

    if ctx:
        case(j == 0, True, [head_store(qkv_ref, BF16)])
        case(j == 1, False, [head_store(qkv_ref, BF16)])
        case((j >= 2) & (j < 4), False, [head_store(qkv_ref, BF16), cache_store(kc_ref)])
        case((j >= 4) & (j < 6), False, [head_store(qkv_ref, BF16), cache_store(vc_ref)])
    else:
        case(j == 0, True, [head_store(q_ref, F32)])
        case(j == 1, False, [head_store(q_ref, F32)])
        case((j >= 2) & (j < 6), False, [head_store(kv_ref, BF16)])
    case(j >= 6, False, [ugp_store])


def _in_proj(x2d, ntok, mod, mod_row_fn, g_mix, w_in, layer, seq_len, ctx, caches=()):
    nblk = ntok // TM_IN
    ncol = IN_WIDTH // TN_IN
    heads = TN_IN // HEAD_DIM
    hm = lambda lo, n: (lambda i, j: (jnp.clip(j - lo, 0, n - 1), i, 0))
    hspec = lambda lo, n: pl.BlockSpec((heads, TM_IN, HEAD_DIM), hm(lo, n))
    ugp_shape = jax.ShapeDtypeStruct((ntok, 2 * D_B + D_C), F32)
    ugp_spec = pl.BlockSpec((TM_IN, TN_IN), lambda i, j: (i, jnp.clip(j - 6, 0, 2)))
    if ctx:
        nseq = ntok // seq_len
        spb = TM_IN // seq_len
        cache_shape = jax.ShapeDtypeStruct((nseq, DEPTH, N_HEADS, seq_len, HEAD_DIM), F32)
        cache_spec = lambda lo: pl.BlockSpec(
            (spb, None, heads, seq_len, HEAD_DIM),
            lambda i, j: (i, layer, jnp.clip(j - lo, 0, 1), 0, 0))
        out_shape = [jax.ShapeDtypeStruct((3 * N_HEADS, ntok, HEAD_DIM), BF16), ugp_shape,
                     cache_shape, cache_shape]
        out_specs = [hspec(0, 6), ugp_spec, cache_spec(2), cache_spec(4)]
    else:
        out_shape = [jax.ShapeDtypeStruct((N_HEADS, ntok, HEAD_DIM), F32),
                     jax.ShapeDtypeStruct((2 * N_HEADS, ntok, HEAD_DIM), BF16), ugp_shape]
        out_specs = [hspec(0, 2), hspec(2, 4), ugp_spec]
    return pl.pallas_call(
        functools.partial(_in_proj_kernel, ctx=ctx, n_aliased=len(caches), layer=layer, nblk=nblk),
        grid=(nblk, ncol),
        in_specs=[
            pl.BlockSpec(memory_space=pl.ANY),
            pl.BlockSpec((None, None, N_MOD, D_MODEL), lambda i, j: (layer, mod_row_fn(i), 0, 0)),
            pl.BlockSpec((None, 1, D_MODEL), lambda i, j: (layer, 0, 0)),
            pl.BlockSpec(memory_space=pl.ANY),
        ] + [pl.BlockSpec(memory_space=pl.ANY)] * len(caches),
        out_specs=out_specs,
        out_shape=out_shape,
        scratch_shapes=[
            pltpu.VMEM((TM_IN, D_MODEL), BF16),
            pltpu.VMEM((2, TM_IN, D_MODEL), F32),
            pltpu.VMEM((W_SLOTS, D_MODEL, TN_IN), BF16),
            pltpu.SemaphoreType.DMA((2,)),
            pltpu.SemaphoreType.DMA((W_SLOTS,)),
        ],
        input_output_aliases={4 + n: 2 + n for n in range(len(caches))},
        compiler_params=_params(("arbitrary", "arbitrary")),
        name="in_proj_ctx" if ctx else "in_proj_lat",
    )(x2d, mod, g_mix, w_in, *caches)


def _write_normed_heads(o_scr, gb_ref, o_ref):
    ss = None
    for h in range(N_HEADS):
        oh = o_scr[h]
        t = jnp.sum(oh * oh, axis=-1, keepdims=True)
        ss = t if ss is None else ss + t
    r = lax.rsqrt(ss * (1.0 / D_A) + EPS)
    for h in range(N_HEADS):
        sl = slice(h * HEAD_DIM, (h + 1) * HEAD_DIM)
        o_ref[:, sl] = (o_scr[h] * r * gb_ref[:, sl]).astype(o_ref.dtype)


def _ctx_attn_kernel(q_ref, k_ref, v_ref, gb_ref, o_ref, o_scr):
    group = 2

    def heads(g, carry):
        hs = [g * group + n for n in range(group)]
        scores = [_dot_nt(q_ref[h], k_ref[h]) * SCALE_LOG2E for h in hs]
        probs = []
        for s in scores:
            p = jnp.exp2(s - jnp.max(s, axis=-1, keepdims=True))
            probs.append((p.astype(BF16), jnp.sum(p, axis=-1, keepdims=True)))
        for h, (p, l) in zip(hs, probs):
            o_scr[h] = _dot(p, v_ref[h]) / l
        return carry

    lax.fori_loop(0, N_HEADS // group, heads, 0)
    _write_normed_heads(o_scr, gb_ref, o_ref)


def _ctx_attention(qkv, gb_a, layer, seq_len):
    ntok = qkv.shape[1]
    spec = lambda part: pl.BlockSpec((N_HEADS, seq_len, HEAD_DIM), lambda b: (part, b, 0))
    return pl.pallas_call(
        _ctx_attn_kernel,
        grid=(ntok // seq_len,),
        in_specs=[spec(0), spec(1), spec(2),
                  pl.BlockSpec((None, 1, D_A), lambda b: (layer, 0, 0))],
        out_specs=pl.BlockSpec((seq_len, D_A), lambda b: (b, 0)),
        out_shape=jax.ShapeDtypeStruct((ntok, D_A), BF16),
        scratch_shapes=[pltpu.VMEM((N_HEADS, seq_len, HEAD_DIM), F32)],
        compiler_params=_params(("arbitrary",)),
        name="ctx_attention",
    )(qkv, qkv, qkv, gb_a)


def _bias_table_kernel(sel_ref, rpb_ref, o_ref):
    kr_per_tile = LANES // KEY_COLS
    by_row = jnp.dot(sel_ref[...], rpb_ref[...], preferred_element_type=F32,
                     precision=lax.Precision.HIGHEST)
    for (c0, nc, kc0) in COL_BLOCKS:
        lane = lax.broadcasted_iota(I32, (nc, LANES), 1)
        c = c0 + lax.broadcasted_iota(I32, (nc, LANES), 0)
        cs = jnp.clip(c - WIN_W // 2, 0, GRID_W - WIN_W)
        masks = []
        for k4 in range(kr_per_tile):
            kc = kc0 + lane - k4 * KEY_COLS
            masks.append((lane >= k4 * KEY_COLS) & (lane < (k4 + 1) * KEY_COLS)
                         & (kc >= cs) & (kc < cs + WIN_W))
        for qr in range(Q_ROWS):
            for t in range(K_ROWS // kr_per_tile):
                tile = jnp.full((nc, LANES), NEG, F32)
                for k4 in range(kr_per_tile):
                    row = qr * K_ROWS + t * kr_per_tile + k4
                    x = jnp.broadcast_to(by_row[row:row + 1, :], (nc, LANES))
                    shift = (k4 * KEY_COLS - kc0 - (WIN_W - 1) + c0) % LANES
                    x = pltpu.roll(x, shift, 1, stride=1, stride_axis=0)
                    tile = jnp.where(masks[k4], x * LOG2E, tile)
                o_ref[pl.ds(qr * GRID_W + c0, nc), pl.ds(t * LANES, LANES)] = tile


def _local_bias_table(rpb, rows):
    n_rb = rows // Q_ROWS
    n_dr = 2 * WIN_H - 1
    n_dc = 2 * WIN_W - 1
    n_sel = n_dr + 1
    sel = np.zeros((3, Q_ROWS, K_ROWS, n_sel), np.float32)
    for t, jb in enumerate((0, 1, n_rb - 1)):
        r0 = Q_ROWS * jb
        ks = min(max(r0 - WIN_H // 2, 0), rows - K_ROWS)
        r = r0 + np.arange(Q_ROWS)[:, None]
        kr = ks + np.arange(K_ROWS)[None, :]
        rs = np.clip(r - WIN_H // 2, 0, rows - WIN_H)
        valid = (kr >= rs) & (kr < rs + WIN_H)
        ri = np.where(valid, kr - r + WIN_H - 1, n_dr)
        sel[t] = ri[..., None] == np.arange(n_sel)
    sel = sel.reshape(3, Q_ROWS * K_ROWS, n_sel)
    n_lh = rpb.shape[0]
    rpb_pad = jnp.concatenate(
        [jnp.pad(rpb, ((0, 0), (0, 0), (0, LANES - n_dc))), jnp.full((n_lh, 1, LANES), NEG, F32)],
        axis=1)
    nq = Q_ROWS * GRID_W
    nk = K_ROWS * KEY_COLS
    return pl.pallas_call(
        _bias_table_kernel,
        grid=(n_lh, 3),
        in_specs=[
            pl.BlockSpec((None, Q_ROWS * K_ROWS, n_sel), lambda h, t: (t, 0, 0)),
            pl.BlockSpec((None, n_sel, LANES), lambda h, t: (h, 0, 0)),
        ],
        out_specs=pl.BlockSpec((None, None, nq, nk), lambda h, t: (h, t, 0, 0)),
        out_shape=jax.ShapeDtypeStruct((n_lh, 3, nq, nk), F32),
        compiler_params=_params(("arbitrary", "arbitrary")),
        name="bias_table",
    )(jnp.asarray(sel), rpb_pad)


def _nbr_attn_kernel(q_ref, k0, k1, k2, k3, v0, v1, v2, v3, ck_ref, cv_ref, bias_ref, gb_ref,
                     o_ref, o_scr):
    k_refs = (k0, k1, k2, k3)
    v_refs = (v0, v1, v2, v3)
    rows_per_ref = K_ROWS // len(k_refs)

    def window(refs, h, kc0):
        parts = []
        for kr in range(K_ROWS):
            ref = refs[kr // rows_per_ref]
            parts.append(ref[h, pl.ds((kr % rows_per_ref) * GRID_W + kc0, KEY_COLS), :])
        return jnp.concatenate(parts, axis=0)

    def query_rows(ref, h, c0, nc):
        return jnp.concatenate(
            [ref[h, pl.ds(qr * GRID_W + c0, nc), :] for qr in range(Q_ROWS)], axis=0)

    def heads(g, carry):
        units = [(g * NBR_HEAD_GROUP + n, blk) for n in range(NBR_HEAD_GROUP) for blk in COL_BLOCKS]
        scores = []
        for h, (c0, nc, kc0) in units:
            qi = query_rows(q_ref, h, c0, nc).astype(BF16)
            ki = window(k_refs, h, kc0)
            s_loc = _dot_nt(qi, ki) * SCALE_LOG2E + query_rows(bias_ref, h, c0, nc)
            s_ctx = _dot_nt(qi, ck_ref[h].astype(BF16)) * SCALE_LOG2E
            scores.append((s_loc, s_ctx))
        probs = []
        for s_loc, s_ctx in scores:
            m = jnp.maximum(jnp.max(s_loc, axis=-1, keepdims=True),
                            jnp.max(s_ctx, axis=-1, keepdims=True))
            p_loc = jnp.exp2(s_loc - m)
            p_ctx = jnp.exp2(s_ctx - m)
            l = jnp.sum(p_loc, axis=-1, keepdims=True) + jnp.sum(p_ctx, axis=-1, keepdims=True)
            probs.append((p_loc.astype(BF16), p_ctx.astype(BF16), l))
        for (h, (c0, nc, kc0)), (p_loc, p_ctx, l) in zip(units, probs):
            vi = window(v_refs, h, kc0)
            o = (_dot(p_loc, vi) + _dot(p_ctx, cv_ref[h].astype(BF16))) / l
            for qr in range(Q_ROWS):
                o_scr[h, pl.ds(qr * GRID_W + c0, nc), :] = o[qr * nc:(qr + 1) * nc]
        return carry

    lax.fori_loop(0, N_HEADS // NBR_HEAD_GROUP, heads, 0)
    _write_normed_heads(o_scr, gb_ref, o_ref)


def _nbr_attention(q, kv, cache_k, cache_v, layer, bias_tab, gb_a, n_req, seq_len):
    rows = seq_len // GRID_W
    n_rb = rows // Q_ROWS
    tq = Q_ROWS * GRID_W
    n_kref = 4
    tk = (K_ROWS // n_kref) * GRID_W
    kblk_per_req = seq_len // tk
    ctx_len = cache_k.shape[3]

    def kmap(part, m):
        def f(b, jb):
            start = jnp.clip(2 * jb - 1, 0, kblk_per_req - n_kref)
            return (part, b * kblk_per_req + start + m, 0)
        return f

    kspecs = [pl.BlockSpec((N_HEADS, tk, HEAD_DIM), kmap(0, m)) for m in range(n_kref)]
    vspecs = [pl.BlockSpec((N_HEADS, tk, HEAD_DIM), kmap(1, m)) for m in range(n_kref)]
    cspec = pl.BlockSpec((None, None, N_HEADS, ctx_len, HEAD_DIM), lambda b, jb: (b, layer, 0, 0, 0))
    kind = lambda b, jb: (layer, (jb > 0).astype(I32) + (jb == n_rb - 1).astype(I32), 0, 0)
    return pl.pallas_call(
        _nbr_attn_kernel,
        grid=(n_req, n_rb),
        in_specs=[pl.BlockSpec((N_HEADS, tq, HEAD_DIM), lambda b, jb: (0, b * n_rb + jb, 0))]
        + kspecs + vspecs + [
            cspec, cspec,
            pl.BlockSpec((N_HEADS, None, tq, K_ROWS * KEY_COLS), kind),
            pl.BlockSpec((None, 1, D_A), lambda b, jb: (layer, 0, 0)),
        ],
        out_specs=pl.BlockSpec((tq, D_A), lambda b, jb: (b * n_rb + jb, 0)),
        out_shape=jax.ShapeDtypeStruct((n_req * seq_len, D_A), BF16),
        scratch_shapes=[pltpu.VMEM((N_HEADS, tq, HEAD_DIM), F32)],
        compiler_params=_params(("arbitrary", "arbitrary")),
        name="nbr_attention",
    )(q, *([kv] * (2 * n_kref)), cache_k, cache_v, bias_tab, gb_a)


def _mixers_kernel(u_ref, gv_ref, p_ref, pprev_ref, pnext_ref, gsgu_ref, wsgu_ref, bsgu_ref,
                   wpool_ref, spool_ref, gb_ref, o_ref, ext_scr, *, seq_len):
    i = pl.program_id(0)
    tm = u_ref.shape[0]
    blocks_per_seq = seq_len // tm
    bi = i % blocks_per_seq
    t0 = bi * tm

    gu = jax.nn.gelu(u_ref[...])
    gg = _rms(jax.nn.gelu(gv_ref[...]), gsgu_ref[...]).astype(BF16)
    cols = []
    for g in range(N_GROUPS_B):
        sl = slice(g * CHUNK, (g + 1) * CHUNK)
        chunks = []
        for n in range(tm // CHUNK):
            rows = slice(n * CHUNK, (n + 1) * CHUNK)
            chunks.append(_dot(wsgu_ref[g], gg[rows, sl]) + bsgu_ref[:, g:g + 1])
        cols.append(jnp.concatenate(chunks, axis=0))
    o_b = gu * jnp.concatenate(cols, axis=1)
    o_ref[:, 0:D_B] = _rms(o_b, gb_ref[:, 0:D_B]).astype(o_ref.dtype)

    p = p_ref[...]
    ext_scr[pl.ds(0, POOL_HALO), :] = jnp.where(bi > 0, pprev_ref[...], 0.0)
    ext_scr[pl.ds(POOL_HALO, tm), :] = p
    ext_scr[pl.ds(POOL_HALO + tm, POOL_HALO), :] = jnp.where(
        bi < blocks_per_seq - 1, pnext_ref[...], 0.0)
    t = t0 + lax.broadcasted_iota(I32, (tm, 1), 0)
    outs = []
    for g, w in enumerate(POOL_WINDOWS):
        half = w // 2
        sl = slice(g * CHUNK, (g + 1) * CHUNK)
        acc = None
        for d in range(-half, half):
            part = ext_scr[pl.ds(POOL_HALO + d, tm), sl]
            acc = part if acc is None else acc + part
        cnt = (jnp.minimum(t + half, seq_len) - jnp.maximum(t - half, 0)).astype(F32)
        pooled = acc / cnt - p[:, sl]
        outs.append(_dot(pooled.astype(BF16), wpool_ref[g]))
    o_c = jnp.concatenate(outs, axis=1) * spool_ref[...]
    o_ref[:, D_B:D_B + D_C] = _rms(o_c, gb_ref[:, D_B:D_B + D_C]).astype(o_ref.dtype)


def _mixers(ugp, g_sgu, w_sgu, b_sgu_t, w_pool, s_pool, gb_bc, layer, seq_len):
    ntok = ugp.shape[0]
    tm = min(TM_MIX, seq_len)
    nblk = ntok // tm
    hb = tm // POOL_HALO
    n_halo = ntok // POOL_HALO
    blk = lambda part: pl.BlockSpec((tm, D_B), lambda i: (i, part))
    full = lambda shape: pl.BlockSpec((None,) + shape, lambda i: (layer,) + (0,) * len(shape))
    return pl.pallas_call(
        functools.partial(_mixers_kernel, seq_len=seq_len),
        grid=(nblk,),
        in_specs=[
            blk(0), blk(1), blk(2),
            pl.BlockSpec((POOL_HALO, D_C), lambda i: (jnp.maximum(i * hb - 1, 0), 2)),
            pl.BlockSpec((POOL_HALO, D_C), lambda i: (jnp.minimum((i + 1) * hb, n_halo - 1), 2)),
            full((1, D_B)), full((N_GROUPS_B, CHUNK, CHUNK)), full((CHUNK, N_GROUPS_B)),
            full((len(POOL_WINDOWS), CHUNK, CHUNK)), full((1, D_C)), full((1, D_B + D_C)),
        ],
        out_specs=pl.BlockSpec((tm, D_B + D_C), lambda i: (i, 0)),
        out_shape=jax.ShapeDtypeStruct((ntok, D_B + D_C), BF16),
        scratch_shapes=[pltpu.VMEM((tm + 2 * POOL_HALO, D_C), F32)],
        compiler_params=_params(("arbitrary",)),
        name="mixers",
    )(ugp, ugp, ugp, ugp, ugp, g_sgu, w_sgu, b_sgu_t, w_pool, s_pool, gb_bc)


def _out_proj_kernel(x_ref, oa_ref, obc_ref, mod_ref, g_ref, w_ref, wr_ref, br_ref,
                     x1_ref, h_ref, e_ref, gate_ref):
    chunks = [pl.ds(m * OUT_CHUNK, OUT_CHUNK) for m in range(TM_OUT // OUT_CHUNK)]
    mixes = [_dot(jnp.concatenate([oa_ref[rows, :], obc_ref[rows, :]], axis=1), w_ref[...])
             for rows in chunks]
    hs = []
    for rows, mix in zip(chunks, mixes):
        x1 = x_ref[rows, :] + mod_ref[2:3, :] * mix
        x1_ref[rows, :] = x1
        h = _rms(x1, g_ref[...]) * (1.0 + mod_ref[4:5, :]) + mod_ref[3:4, :]
        hb = h.astype(BF16)
        h_ref[rows] = _pack_bf16_pairs(hb).reshape((OUT_CHUNK,) + ROW_TILE)
        hs.append(hb)
    logits = [_dot(hb, wr_ref[...]) + br_ref[...] for hb in hs]
    for rows, lg in zip(chunks, logits):
        _route_rows(rows, lg, e_ref, gate_ref)


def _route_rows(rows, logits, e_ref, gate_ref):
    lane = lax.broadcasted_iota(I32, logits.shape, 1)
    big = jnp.int32(LANES)

    def softmax_masked(mask):
        z = jnp.where(mask, logits, NEG)
        m = jnp.max(z, axis=-1, keepdims=True)
        e = jnp.where(mask, jnp.exp(z - m), 0.0)
        return e / jnp.sum(e, axis=-1, keepdims=True)

    def top1(vals, mask):
        v = jnp.where(mask, vals, -1.0)
        best = jnp.max(v, axis=-1, keepdims=True)
        idx = jnp.min(jnp.where(mask & (v == best), lane, big), axis=-1, keepdims=True)
        return best, idx

    gmask = lane < N_GROUPS_E
    p_grp = softmax_masked(gmask)
    p_g, g_sel = top1(p_grp, gmask)
    lo = N_GROUPS_E + EXPERTS_PER_GROUP * g_sel
    emask = (lane >= lo) & (lane < lo + EXPERTS_PER_GROUP)
    p_exp = softmax_masked(emask)
    p1, i1 = top1(p_exp, emask)
    p2, i2 = top1(p_exp, emask & (lane != i1))
    denom = p1 + p2
    g1 = p_g * p1 / denom
    g2 = p_g * p2 / denom
    e_ref[rows, :] = jnp.where(lane == 0, i1 - N_GROUPS_E,
                               jnp.where(lane == 1, i2 - N_GROUPS_E, 0))
    gate_ref[rows, :] = jnp.where(lane == 0, g1, jnp.where(lane == 1, g2, 0.0))


def _out_proj(x2d, ntok, o_a, o_bc, mod, mod_row_fn, g_ffn, w_out, w_r, b_r, layer):
    nblk = ntok // TM_OUT
    full = lambda shape: pl.BlockSpec((None,) + shape, lambda i: (layer,) + (0,) * len(shape))
    return pl.pallas_call(
        _out_proj_kernel,
        grid=(nblk,),
        in_specs=[
            pl.BlockSpec((TM_OUT, D_MODEL), lambda i: (i, 0)),
            pl.BlockSpec((TM_OUT, D_A), lambda i: (i, 0)),
            pl.BlockSpec((TM_OUT, D_B + D_C), lambda i: (i, 0)),
            pl.BlockSpec((None, None, N_MOD, D_MODEL), lambda i: (layer, mod_row_fn(i), 0, 0)),
            full((1, D_MODEL)), full((D_MODEL, D_MODEL)), full((D_MODEL, LANES)), full((1, LANES)),
        ],
        out_specs=[
            pl.BlockSpec((TM_OUT, D_MODEL), lambda i: (i, 0)),
            pl.BlockSpec((TM_OUT,) + ROW_TILE, lambda i: (i, 0, 0)),
            pl.BlockSpec((TM_OUT, LANES), lambda i: (i, 0)),
            pl.BlockSpec((TM_OUT, LANES), lambda i: (i, 0)),
        ],
        out_shape=[
            jax.ShapeDtypeStruct((ntok, D_MODEL), F32),
            jax.ShapeDtypeStruct((ntok,) + ROW_TILE, U32),
            jax.ShapeDtypeStruct((ntok, LANES), I32),
            jax.ShapeDtypeStruct((ntok, LANES), F32),
        ],
        compiler_params=_params(("arbitrary",)),
        name="out_proj_router",
    )(x2d, o_a, o_bc, mod, g_ffn, w_out, w_r, b_r)


def _route_plan(e_flat):
    n_assign = e_flat.shape[0]
    onehot = (e_flat[:, None] == jnp.arange(N_EXPERTS, dtype=I32)[None, :]).astype(I32)
    csum = jnp.cumsum(onehot, axis=0)
    counts = csum[-1]
    rank = jnp.sum(onehot * (csum - 1), axis=1)
    padded = (counts + MOE_BM - 1) // MOE_BM * MOE_BM
    pad_end = jnp.cumsum(padded)
    pad_start = pad_end - padded
    pos = jnp.sum(onehot * pad_start[None, :], axis=1) + rank
    n_blocks = n_assign // MOE_BM + N_EXPERTS
    starts = jnp.arange(n_blocks, dtype=I32) * MOE_BM
    block_e = jnp.minimum(jnp.sum((starts[:, None] >= pad_end[None, :]).astype(I32), axis=1),
                          N_EXPERTS - 1).astype(I32)
    n_used = (pad_end[-1] // MOE_BM).astype(I32).reshape(1)
    fill_start = jnp.where(padded > 0, pad_end - MOE_BM, -1).astype(I32)
    has_rows = padded > 0
    block_first = jnp.concatenate(
        [jnp.ones((1,), I32), (block_e[1:] != block_e[:-1]).astype(I32)])
    ids = jnp.arange(N_EXPERTS, dtype=I32)
    later = jnp.where(has_rows[None, :] & (ids[None, :] > ids[:, None]), ids[None, :], N_EXPERTS)
    next_e = jnp.min(later, axis=1)
    next_e = jnp.where(next_e < N_EXPERTS, next_e, -1).astype(I32)
    buf_e = ((jnp.cumsum(has_rows.astype(I32)) - 1) % 2).astype(I32)
    plan = (block_e, n_used, block_first, next_e, buf_e)
    return pos.astype(I32), plan, n_used, fill_start, n_blocks


def _dispatch_kernel(fill_ref, nu_ref, pos_ref, hp_ref, hs_ref, xs_ref, zero_scr, sem, fill_sem,
                     *, n_ctx_blocks, n_blocks):
    i = pl.program_id(0)

    @pl.when(i == 0)
    def _():
        zero_scr[...] = jnp.zeros_like(zero_scr)

        def fill(start):
            start = pl.multiple_of(start, MOE_BM)
            return pltpu.make_async_copy(zero_scr, xs_ref.at[pl.ds(start, MOE_BM)], fill_sem)

        def expert_fill(e, carry):
            pl.when(fill_ref[e] >= 0)(lambda: fill(fill_ref[e]).start())
            return carry

        def expert_wait(e, carry):
            pl.when(fill_ref[e] >= 0)(lambda: fill(0).wait())
            return carry

        def tail_fill(b, carry):
            fill(b * MOE_BM).start()
            return carry

        def tail_wait(b, carry):
            fill(0).wait()
            return carry

        lax.fori_loop(0, N_EXPERTS, expert_fill, 0)
        lax.fori_loop(nu_ref[0], n_blocks, tail_fill, 0)
        lax.fori_loop(0, N_EXPERTS, expert_wait, 0)
        lax.fori_loop(nu_ref[0], n_blocks, tail_wait, 0)

    def scatter(h_ref):
        def issue(r, carry):
            for k in range(TOP_K):
                slot = pos_ref[0, 0, TOP_K * r + k]
                pltpu.make_async_copy(h_ref.at[r], xs_ref.at[slot], sem).start(priority=k)
            return carry

        lax.fori_loop(0, TM_ROW, issue, 0, unroll=8)
        for _ in range(TOP_K):
            pltpu.make_async_copy(h_ref, h_ref, sem).wait()

    pl.when(i < n_ctx_blocks)(lambda: scatter(hp_ref))
    pl.when(i >= n_ctx_blocks)(lambda: scatter(hs_ref))


def _dispatch(fill_start, n_used, pos3, hp, hs, n_blocks):
    nbp = hp.shape[0] // TM_ROW
    nbs = hs.shape[0] // TM_ROW
    grid_spec = pltpu.PrefetchScalarGridSpec(
        num_scalar_prefetch=2,
        grid=(nbp + nbs,),
        in_specs=[
            pl.BlockSpec((1, 1, TOP_K * TM_ROW), lambda i, fs, nu: (i, 0, 0),
                         memory_space=pltpu.SMEM),
            pl.BlockSpec((TM_ROW,) + ROW_TILE, lambda i, fs, nu: (jnp.minimum(i, nbp - 1), 0, 0)),
            pl.BlockSpec((TM_ROW,) + ROW_TILE, lambda i, fs, nu: (jnp.maximum(i - nbp, 0), 0, 0)),
        ],
        out_specs=pl.BlockSpec(memory_space=pl.ANY),
        scratch_shapes=[pltpu.VMEM((MOE_BM,) + ROW_TILE, U32), pltpu.SemaphoreType.DMA(()),
                        pltpu.SemaphoreType.DMA(())],
    )
    return pl.pallas_call(
        functools.partial(_dispatch_kernel, n_ctx_blocks=nbp, n_blocks=n_blocks),
        grid_spec=grid_spec,
        out_shape=jax.ShapeDtypeStruct((n_blocks * MOE_BM,) + ROW_TILE, U32),
        compiler_params=_params(("arbitrary",)),
        name="moe_dispatch",
    )(fill_start, n_used, pos3, hp, hs)


def _experts_kernel(be_ref, nu_ref, first_ref, next_ref, buf_ref, xs_ref, wg_hbm, wu_hbm, wd_hbm,
                    ys_ref, wg_buf, wu_buf, wd_buf, sem, *, layer):
    i = pl.program_id(0)

    def weight_copies(e, b):
        return [pltpu.make_async_copy(src.at[layer, e], dst.at[b], sem.at[b])
                for src, dst in ((wg_hbm, wg_buf), (wu_hbm, wu_buf), (wd_hbm, wd_buf))]

    @pl.when(i < nu_ref[0])
    def _():
        e = be_ref[i]
        b = buf_ref[e]

        @pl.when(i == 0)
        def _():
            for cp in weight_copies(e, b):
                cp.start()

        @pl.when(first_ref[i] == 1)
        def _():
            nxt = next_ref[e]

            @pl.when(nxt >= 0)
            def _():
                for cp in weight_copies(nxt, 1 - b):
                    cp.start()

            for cp in weight_copies(e, b):
                cp.wait()

        x = _unpack_bf16_pairs(xs_ref[...].reshape(MOE_BM, D_PACK), BF16)
        g = _dot(x, wg_buf[b].astype(BF16))
        u = _dot(x, wu_buf[b].astype(BF16))
        a = (g * jax.nn.sigmoid(g) * u).astype(BF16)
        y = _dot(a, wd_buf[b].astype(BF16)).astype(BF16)
        ys_ref[...] = _pack_bf16_pairs(y).reshape(ys_ref.shape)

    @pl.when(i >= nu_ref[0])
    def _():
        ys_ref[...] = jnp.zeros_like(ys_ref)


def _experts(plan, xs, wg, wu, wd, layer, n_blocks):
    grid_spec = pltpu.PrefetchScalarGridSpec(
        num_scalar_prefetch=len(plan),
        grid=(n_blocks,),
        in_specs=[
            pl.BlockSpec((MOE_BM,) + ROW_TILE,
                         lambda i, be, nu, *_: (jnp.minimum(i, nu[0] - 1), 0, 0)),
            pl.BlockSpec(memory_space=pl.ANY),
            pl.BlockSpec(memory_space=pl.ANY),
            pl.BlockSpec(memory_space=pl.ANY),
        ],
        out_specs=pl.BlockSpec((MOE_BM,) + ROW_TILE, lambda i, *_: (i, 0, 0)),
        scratch_shapes=[
            pltpu.VMEM((2, D_MODEL, D_EXPERT), F32),
            pltpu.VMEM((2, D_MODEL, D_EXPERT), F32),
            pltpu.VMEM((2, D_EXPERT, D_MODEL), F32),
            pltpu.SemaphoreType.DMA((2,)),
        ],
    )
    return pl.pallas_call(
        functools.partial(_experts_kernel, layer=layer),
        grid_spec=grid_spec,
        out_shape=jax.ShapeDtypeStruct(xs.shape, U32),
        compiler_params=_params(("arbitrary",)),
        name="moe_experts",
    )(*plan, xs, wg, wu, wd)


def _combine_kernel(pos_ref, pos_next_ref, x1_ref, gate_ref, mod_ref, gfin_ref, ys_ref, o_ref,
                    ybuf, sem, *, final_norm):
    i = pl.program_id(0)
    buf = i % 2
    last = i == pl.num_programs(0) - 1

    def start_row(p_ref, b, r):
        for k in range(TOP_K):
            slot = p_ref[0, 0, TOP_K * r + k]
            pltpu.make_async_copy(ys_ref.at[slot], ybuf.at[b, k, r], sem.at[b]).start(priority=k)

    def wait_buffer(b):
        for k in range(TOP_K):
            pltpu.make_async_copy(ybuf.at[b, k], ybuf.at[b, k], sem.at[b]).wait()

    @pl.when(i == 0)
    def _():
        lax.fori_loop(0, TM_ROW, lambda r, c: (start_row(pos_ref, 0, r), c)[1], 0, unroll=8)

    wait_buffer(buf)
    for m in range(TM_ROW // COMBINE_CHUNK):
        for r in range(m * COMBINE_CHUNK, (m + 1) * COMBINE_CHUNK):
            start_row(pos_next_ref, 1 - buf, r)
        rows = pl.ds(m * COMBINE_CHUNK, COMBINE_CHUNK)
        y0 = _unpack_bf16_pairs(ybuf[buf, 0, rows].reshape(COMBINE_CHUNK, D_PACK), F32)
        y1 = _unpack_bf16_pairs(ybuf[buf, 1, rows].reshape(COMBINE_CHUNK, D_PACK), F32)
        f = gate_ref[rows, 0:1] * y0 + gate_ref[rows, 1:2] * y1
        x2 = x1_ref[rows, :] + mod_ref[5:6, :] * f
        if final_norm:
            x2 = _rms(x2, gfin_ref[...])
        o_ref[rows, :] = x2

    pl.when(last)(lambda: wait_buffer(1 - buf))


def _combine(pos3, x1, gate, mod, mod_row_fn, g_final, ys, layer, final_norm):
    ntok = x1.shape[0]
    nblk = ntok // TM_ROW
    return pl.pallas_call(
        functools.partial(_combine_kernel, final_norm=final_norm),
        grid=(nblk,),
        in_specs=[
            pl.BlockSpec((1, 1, TOP_K * TM_ROW), lambda i: (i, 0, 0), memory_space=pltpu.SMEM),
            pl.BlockSpec((1, 1, TOP_K * TM_ROW), lambda i: (jnp.minimum(i + 1, nblk - 1), 0, 0),
                         memory_space=pltpu.SMEM),
            pl.BlockSpec((TM_ROW, D_MODEL), lambda i: (i, 0)),
            pl.BlockSpec((TM_ROW, LANES), lambda i: (i, 0)),
            pl.BlockSpec((None, None, N_MOD, D_MODEL), lambda i: (layer, mod_row_fn(i), 0, 0)),
            pl.BlockSpec((1, D_MODEL), lambda i: (0, 0)),
            pl.BlockSpec(memory_space=pl.ANY),
        ],
        out_specs=pl.BlockSpec((TM_ROW, D_MODEL), lambda i: (i, 0)),
        out_shape=jax.ShapeDtypeStruct((ntok, D_MODEL), F32),
        scratch_shapes=[pltpu.VMEM((2, TOP_K, TM_ROW) + ROW_TILE, U32),
                        pltpu.SemaphoreType.DMA((2,))],
        compiler_params=_params(("arbitrary",)),
        name="moe_combine",
    )(pos3, pos3, x1, gate, mod, g_final, ys)


def kernel(x_prompt, x_sample, cache_k, cache_v, c, c_ctx, w_mod, b_mod, g_mix, g_ffn, w_in, rpb,
           g_sgu, w_sgu, b_sgu, w_pool, s_pool, g_branch, w_out, w_rg, b_rg, w_re, b_re,
           w_e_gate, w_e_up, w_e_down, g_final):
    n_ctx, seq, _ = x_prompt.shape
    n_lat, lat_seq, _ = x_sample.shape
    ntok_p = n_ctx * seq
    ntok_s = n_lat * lat_seq
    rows = lat_seq // GRID_W

    n_mod_rows = -(-(1 + n_lat) // SUBLANES) * SUBLANES
    c_all = jnp.concatenate(
        [c_ctx[None], c, jnp.zeros((n_mod_rows - 1 - n_lat, D_MODEL), F32)], axis=0)
    mod = _modulation(c_all, w_mod, b_mod).reshape(DEPTH, n_mod_rows, N_MOD, D_MODEL)

    xp = x_prompt.reshape(ntok_p, D_MODEL)
    xs_lat = x_sample.reshape(ntok_s, D_MODEL)

    def lat_mod_row(tm):
        return lambda i: 1 + (i * tm) // lat_seq

    ctx_mod_row = lambda i: 0

    w_in_b = w_in.astype(BF16)
    w_out_b = w_out.astype(BF16)
    w_sgu_b = w_sgu.astype(BF16)
    w_pool_b = w_pool.astype(BF16)

    gmix = g_mix[:, None, :]
    gffn = g_ffn[:, None, :]
    gb_a = g_branch[:, None, :D_A]
    gb_bc = g_branch[:, None, D_A:]
    gsgu = g_sgu[:, None, :]
    bsgu_t = jnp.swapaxes(b_sgu, 1, 2)
    spool = s_pool[:, None, :]
    n_route = N_GROUPS_E + N_EXPERTS
    w_r = jnp.concatenate(
        [w_rg, w_re, jnp.zeros((DEPTH, D_MODEL, LANES - n_route), F32)], axis=2)
    w_r = w_r.astype(BF16)
    b_r = jnp.concatenate(
        [b_rg, b_re, jnp.zeros((DEPTH, LANES - n_route), F32)], axis=1)[:, None, :]
    bias_tab = _local_bias_table(rpb.reshape((DEPTH * N_HEADS,) + rpb.shape[2:]), rows)

    cache_shape = (n_ctx, DEPTH, N_HEADS, seq, HEAD_DIM)
    caches = (jnp.zeros(cache_shape, F32), jnp.zeros(cache_shape, F32))
    for l in range(DEPTH):
        qkv_p, ugp_p, *caches = _in_proj(
            xp, ntok_p, mod, ctx_mod_row, gmix, w_in_b, l, seq, True, tuple(caches))
        oa_p = _ctx_attention(qkv_p, gb_a, l, seq)
        obc_p = _mixers(ugp_p, gsgu, w_sgu_b, bsgu_t, w_pool_b, spool, gb_bc, l, seq)
        x1p, hp, ep, gatep = _out_proj(
            xp, ntok_p, oa_p, obc_p, mod, ctx_mod_row, gffn, w_out_b, w_r, b_r, l)

        q_s, kv_s, ugp_s = _in_proj(
            xs_lat, ntok_s, mod, lat_mod_row(TM_IN), gmix, w_in_b, l, lat_seq, False)
        oa_s = _nbr_attention(q_s, kv_s, cache_k, cache_v, l, bias_tab, gb_a, n_lat, lat_seq)
        obc_s = _mixers(ugp_s, gsgu, w_sgu_b, bsgu_t, w_pool_b, spool, gb_bc, l, lat_seq)
        x1s, hs, es, gates = _out_proj(
            xs_lat, ntok_s, oa_s, obc_s, mod, lat_mod_row(TM_OUT), gffn, w_out_b, w_r, b_r, l)

        e_flat = jnp.concatenate([ep[:, :TOP_K].reshape(-1), es[:, :TOP_K].reshape(-1)])
        pos, plan, n_used, fill_start, n_blocks = _route_plan(e_flat)
        pos3 = pos.reshape((ntok_p + ntok_s) // TM_ROW, 1, TOP_K * TM_ROW)
        pos_p = pos3[:ntok_p // TM_ROW]
        pos_s = pos3[ntok_p // TM_ROW:]
        slots = _dispatch(fill_start, n_used, pos3, hp, hs, n_blocks)
        ys = _experts(plan, slots, w_e_gate, w_e_up, w_e_down, l, n_blocks)
        last = l == DEPTH - 1
        gfin = g_final[None]
        xp = _combine(pos_p, x1p, gatep, mod, ctx_mod_row, gfin, ys, l, last)
        xs_lat = _combine(pos_s, x1s, gates, mod, lat_mod_row(TM_ROW), gfin, ys, l, last)

    y_prompt = xp.reshape(n_ctx, seq, D_MODEL)
    y_sample = xs_lat.reshape(n_lat, lat_seq, D_MODEL)
    return (y_prompt, y_sample, caches[0], caches[1])
```

```python
import functools

import jax
import numpy as np
import jax.numpy as jnp
from jax import lax
from jax.experimental import pallas as pl
from jax.experimental.pallas import tpu as pltpu

F32 = jnp.float32
BF16 = jnp.bfloat16
I32 = jnp.int32
U32 = jnp.uint32

D_MODEL = 2048
DEPTH = 2
GRID_W = 64
HEAD_DIM = 128
D_A = 1024
N_HEADS = 8
WIN_H = 8
WIN_W = 16
D_B = 512
N_GROUPS_B = 4
CHUNK = 128
D_C = 512
POOL_WINDOWS = (2, 4, 8, 16)
IN_WIDTH = 3 * D_A + 2 * D_B + D_C
N_GROUPS_E = 4
EXPERTS_PER_GROUP = 8
N_EXPERTS = 32
TOP_K = 2
D_EXPERT = 512
N_MOD = 6
EPS = 1e-6
NEG = -1e30
LOG2E = 1.4426950408889634
SCALE_LOG2E = HEAD_DIM ** -0.5 * LOG2E

LANES = 128
SUBLANES = 8
VMEM_LIMIT_BYTES = 56 * 1024 * 1024

TM_IN = 1024
TN_IN = 512
IN_CHUNK = 256
W_SLOTS = 3
OUT_CHUNK = 256
TM_OUT = 512
TM_MIX = 512
Q_ROWS = 8
K_ROWS = 16
NBR_HEAD_GROUP = 2
MOE_BM = 512
D_PACK = D_MODEL // 2
ROW_TILE = (SUBLANES, LANES)
assert D_PACK == SUBLANES * LANES
TM_ROW = 512
COMBINE_CHUNK = 128
POOL_HALO = 8

COL_BLOCKS = ((0, 24, 0), (24, 16, 16), (40, 24, 32))
KEY_COLS = 32


def _params(sem):
    return pltpu.CompilerParams(dimension_semantics=sem, vmem_limit_bytes=VMEM_LIMIT_BYTES)


def _dot(a, b):
    return jnp.dot(a, b, preferred_element_type=F32)


def _dot_nt(a, b):
    return lax.dot_general(a, b, (((1,), (1,)), ((), ())), preferred_element_type=F32)


def _rms(x, g):
    return x * lax.rsqrt(jnp.mean(x * x, axis=-1, keepdims=True) + EPS) * g


def _pack_bf16_pairs(x):
    n = x.shape[1] // 2
    lo = lax.bitcast_convert_type(x[:, :n].astype(F32), U32) >> 16
    hi = lax.bitcast_convert_type(x[:, n:].astype(F32), U32) & jnp.uint32(0xFFFF0000)
    return hi | lo


def _unpack_bf16_pairs(w, dtype):
    lo = lax.bitcast_convert_type(w << 16, F32).astype(dtype)
    hi = lax.bitcast_convert_type(w & jnp.uint32(0xFFFF0000), F32).astype(dtype)
    return jnp.concatenate([lo, hi], axis=1)


def _mod_kernel(c_ref, w_ref, b_ref, o_ref):
    c = c_ref[...]
    s = c * jax.nn.sigmoid(c)
    n = s.shape[0]
    w = w_ref[...]
    s_hi = s.astype(BF16)
    s_lo = (s - s_hi.astype(F32)).astype(BF16)
    w_hi = w.astype(BF16)
    w_lo = (w - w_hi.astype(F32)).astype(BF16)
    r = _dot(jnp.concatenate([s_hi, s_lo], axis=0), w_hi)
    o_ref[...] = r[:n] + r[n:] + _dot(s_hi, w_lo) + b_ref[...]


def _modulation(c_all, w_mod, b_mod):
    tn = 1024
    nrow = c_all.shape[0]
    width = w_mod.shape[2]
    return pl.pallas_call(
        _mod_kernel,
        grid=(DEPTH, width // tn),
        in_specs=[
            pl.BlockSpec((nrow, D_MODEL), lambda l, j: (0, 0)),
            pl.BlockSpec((None, D_MODEL, tn), lambda l, j: (l, 0, j)),
            pl.BlockSpec((None, 1, tn), lambda l, j: (l, 0, j)),
        ],
        out_specs=pl.BlockSpec((None, nrow, tn), lambda l, j: (l, 0, j)),
        out_shape=jax.ShapeDtypeStruct((DEPTH, nrow, width), F32),
        compiler_params=_params(("arbitrary", "arbitrary")),
        name="modulation",
    )(c_all, w_mod, b_mod.reshape(DEPTH, 1, width))


def _in_proj_kernel(x_hbm, mod_ref, g_ref, w_hbm, *refs, ctx, n_aliased, layer, nblk):
    refs = refs[n_aliased:]
    if ctx:
        qkv_ref, ugp_ref, kc_ref, vc_ref, h_scr, xbuf, wbuf, xsem, wsem = refs
    else:
        q_ref, kv_ref, ugp_ref, h_scr, xbuf, wbuf, xsem, wsem = refs
    i = pl.program_id(0)
    j = pl.program_id(1)
    ncol = IN_WIDTH // TN_IN
    step = i * ncol + j
    heads = TN_IN // HEAD_DIM

    def x_copy(blk):
        slot = blk % 2
        return pltpu.make_async_copy(
            x_hbm.at[pl.ds(pl.multiple_of(blk * TM_IN, TM_IN), TM_IN)], xbuf.at[slot],
            xsem.at[slot])

    def w_copy(s):
        slot = s % W_SLOTS
        col = pl.multiple_of((s % ncol) * TN_IN, TN_IN)
        return pltpu.make_async_copy(
            w_hbm.at[layer, :, pl.ds(col, TN_IN)], wbuf.at[slot], wsem.at[slot])

    @pl.when(step == 0)
    def _():
        x_copy(0).start()
        for s in range(W_SLOTS - 1):
            w_copy(s).start()

    @pl.when((j == 0) & (i + 1 < nblk))
    def _():
        x_copy(i + 1).start()

    @pl.when(step + W_SLOTS - 1 < nblk * ncol)
    def _():
        w_copy(step + W_SLOTS - 1).start()

    @pl.when(j == 0)
    def _():
        x_copy(i).wait()

    w_copy(step).wait()
    x_ref = xbuf.at[i % 2]
    w_ref = wbuf.at[step % W_SLOTS]

    def head_store(ref, dtype):
        def store(m, rows, acc):
            for hh in range(heads):
                ref[hh, rows, :] = acc[:, hh * HEAD_DIM:(hh + 1) * HEAD_DIM].astype(dtype)
        return store

    def cache_store(ref):
        seq = ref.shape[2]
        per_chunk = IN_CHUNK // seq

        def store(m, rows, acc):
            for s in range(per_chunk):
                for hh in range(heads):
                    ref[m * per_chunk + s, hh] = acc[s * seq:(s + 1) * seq,
                                                     hh * HEAD_DIM:(hh + 1) * HEAD_DIM]
        return store

    def ugp_store(m, rows, acc):
        ugp_ref[rows, :] = acc

    def run(normalize, stores):
        for m in range(TM_IN // IN_CHUNK):
            rows = pl.ds(m * IN_CHUNK, IN_CHUNK)
            if normalize:
                h = _rms(x_ref[rows, :], g_ref[...]) * (1.0 + mod_ref[1:2, :]) + mod_ref[0:1, :]
                hb = h.astype(BF16)
                h_scr[rows, :] = hb
            else:
                hb = h_scr[rows, :]
            acc = _dot(hb, w_ref[...])
            for store in stores:
                store(m, rows, acc)

    def case(cond, normalize, stores):
        pl.when(cond)(lambda: run(normalize, stores))

    if ctx:
        case(j == 0, True, [head_store(qkv_ref, BF16)])
        case(j == 1, False, [head_store(qkv_ref, BF16)])
        case((j >= 2) & (j < 4), False, [head_store(qkv_ref, BF16), cache_store(kc_ref)])
        case((j >= 4) & (j < 6), False, [head_store(qkv_ref, BF16), cache_store(vc_ref)])
    else:
        case(j == 0, True, [head_store(q_ref, F32)])
        case(j == 1, False, [head_store(q_ref, F32)])
        case((j >= 2) & (j < 6), False, [head_store(kv_ref, BF16)])
    case(j >= 6, False, [ugp_store])


def _in_proj(x2d, ntok, mod, mod_row_fn, g_mix, w_in, layer, seq_len, ctx, caches=()):
    nblk = ntok // TM_IN
    ncol = IN_WIDTH // TN_IN
    heads = TN_IN // HEAD_DIM
    hm = lambda lo, n: (lambda i, j: (jnp.clip(j - lo, 0, n - 1), i, 0))
    hspec = lambda lo, n: pl.BlockSpec((heads, TM_IN, HEAD_DIM), hm(lo, n))
    ugp_shape = jax.ShapeDtypeStruct((ntok, 2 * D_B + D_C), F32)
    ugp_spec = pl.BlockSpec((TM_IN, TN_IN), lambda i, j: (i, jnp.clip(j - 6, 0, 2)))
    if ctx:
        nseq = ntok // seq_len
        spb = TM_IN // seq_len
        cache_shape = jax.ShapeDtypeStruct((nseq, DEPTH, N_HEADS, seq_len, HEAD_DIM), F32)
        cache_spec = lambda lo: pl.BlockSpec(
            (spb, None, heads, seq_len, HEAD_DIM),
            lambda i, j: (i, layer, jnp.clip(j - lo, 0, 1), 0, 0))
        out_shape = [jax.ShapeDtypeStruct((3 * N_HEADS, ntok, HEAD_DIM), BF16), ugp_shape,
                     cache_shape, cache_shape]
        out_specs = [hspec(0, 6), ugp_spec, cache_spec(2), cache_spec(4)]
    else:
        out_shape = [jax.ShapeDtypeStruct((N_HEADS, ntok, HEAD_DIM), F32),
                     jax.ShapeDtypeStruct((2 * N_HEADS, ntok, HEAD_DIM), BF16), ugp_shape]
        out_specs = [hspec(0, 2), hspec(2, 4), ugp_spec]
    return pl.pallas_call(
        functools.partial(_in_proj_kernel, ctx=ctx, n_aliased=len(caches), layer=layer, nblk=nblk),
        grid=(nblk, ncol),
        in_specs=[
            pl.BlockSpec(memory_space=pl.ANY),
            pl.BlockSpec((None, None, N_MOD, D_MODEL), lambda i, j: (layer, mod_row_fn(i), 0, 0)),
            pl.BlockSpec((None, 1, D_MODEL), lambda i, j: (layer, 0, 0)),
            pl.BlockSpec(memory_space=pl.ANY),
        ] + [pl.BlockSpec(memory_space=pl.ANY)] * len(caches),
        out_specs=out_specs,
        out_shape=out_shape,
        scratch_shapes=[
            pltpu.VMEM((TM_IN, D_MODEL), BF16),
            pltpu.VMEM((2, TM_IN, D_MODEL), F32),
            pltpu.VMEM((W_SLOTS, D_MODEL, TN_IN), BF16),
            pltpu.SemaphoreType.DMA((2,)),
            pltpu.SemaphoreType.DMA((W_SLOTS,)),
        ],
        input_output_aliases={4 + n: 2 + n for n in range(len(caches))},
        compiler_params=_params(("arbitrary", "arbitrary")),
        name="in_proj_ctx" if ctx else "in_proj_lat",
    )(x2d, mod, g_mix, w_in, *caches)


def _write_normed_heads(o_scr, gb_ref, o_ref):
    sq = None
    for h in range(N_HEADS):
        oh = o_scr[h]
        sq = oh * oh if sq is None else sq + oh * oh
    ss = jnp.sum(sq, axis=-1, keepdims=True)
    r = lax.rsqrt(ss * (1.0 / D_A) + EPS)
    for h in range(N_HEADS):
        sl = slice(h * HEAD_DIM, (h + 1) * HEAD_DIM)
        o_ref[:, sl] = (o_scr[h] * r * gb_ref[:, sl]).astype(o_ref.dtype)


def _ctx_attn_kernel(q_ref, k_ref, v_ref, gb_ref, o_ref, o_scr):
    group = 2

    def heads(g, carry):
        hs = [g * group + n for n in range(group)]
        scores = [_dot_nt(q_ref[h], k_ref[h]) * SCALE_LOG2E for h in hs]
        probs = []
        for s in scores:
            p = jnp.exp2(s - jnp.max(s, axis=-1, keepdims=True))
            probs.append((p.astype(BF16), jnp.sum(p, axis=-1, keepdims=True)))
        for h, (p, l) in zip(hs, probs):
            o_scr[h] = _dot(p, v_ref[h]) / l
        return carry

    lax.fori_loop(0, N_HEADS // group, heads, 0)
    _write_normed_heads(o_scr, gb_ref, o_ref)


def _ctx_attention(qkv, gb_a, layer, seq_len):
    ntok = qkv.shape[1]
    spec = lambda part: pl.BlockSpec((N_HEADS, seq_len, HEAD_DIM), lambda b: (part, b, 0))
    return pl.pallas_call(
        _ctx_attn_kernel,
        grid=(ntok // seq_len,),
        in_specs=[spec(0), spec(1), spec(2),
                  pl.BlockSpec((None, 1, D_A), lambda b: (layer, 0, 0))],
        out_specs=pl.BlockSpec((seq_len, D_A), lambda b: (b, 0)),
        out_shape=jax.ShapeDtypeStruct((ntok, D_A), BF16),
        scratch_shapes=[pltpu.VMEM((N_HEADS, seq_len, HEAD_DIM), F32)],
        compiler_params=_params(("arbitrary",)),
        name="ctx_attention",
    )(qkv, qkv, qkv, gb_a)


def _bias_table_kernel(sel_ref, rpb_ref, o_ref):
    kr_per_tile = LANES // KEY_COLS
    by_row = jnp.dot(sel_ref[...], rpb_ref[...], preferred_element_type=F32,
                     precision=lax.Precision.HIGHEST)
    for (c0, nc, kc0) in COL_BLOCKS:
        lane = lax.broadcasted_iota(I32, (nc, LANES), 1)
        c = c0 + lax.broadcasted_iota(I32, (nc, LANES), 0)
        cs = jnp.clip(c - WIN_W // 2, 0, GRID_W - WIN_W)
        masks = []
        for k4 in range(kr_per_tile):
            kc = kc0 + lane - k4 * KEY_COLS
            masks.append((lane >= k4 * KEY_COLS) & (lane < (k4 + 1) * KEY_COLS)
                         & (kc >= cs) & (kc < cs + WIN_W))
        for qr in range(Q_ROWS):
            for t in range(K_ROWS // kr_per_tile):
                tile = jnp.full((nc, LANES), NEG, F32)
                for k4 in range(kr_per_tile):
                    row = qr * K_ROWS + t * kr_per_tile + k4
                    x = jnp.broadcast_to(by_row[row:row + 1, :], (nc, LANES))
                    shift = (k4 * KEY_COLS - kc0 - (WIN_W - 1) + c0) % LANES
                    x = pltpu.roll(x, shift, 1, stride=1, stride_axis=0)
                    tile = jnp.where(masks[k4], x * LOG2E, tile)
                o_ref[pl.ds(qr * GRID_W + c0, nc), pl.ds(t * LANES, LANES)] = tile


def _local_bias_table(rpb, rows):
    n_rb = rows // Q_ROWS
    n_dr = 2 * WIN_H - 1
    n_dc = 2 * WIN_W - 1
    n_sel = n_dr + 1
    sel = np.zeros((3, Q_ROWS, K_ROWS, n_sel), np.float32)
    for t, jb in enumerate((0, 1, n_rb - 1)):
        r0 = Q_ROWS * jb
        ks = min(max(r0 - WIN_H // 2, 0), rows - K_ROWS)
        r = r0 + np.arange(Q_ROWS)[:, None]
        kr = ks + np.arange(K_ROWS)[None, :]
        rs = np.clip(r - WIN_H // 2, 0, rows - WIN_H)
        valid = (kr >= rs) & (kr < rs + WIN_H)
        ri = np.where(valid, kr - r + WIN_H - 1, n_dr)
        sel[t] = ri[..., None] == np.arange(n_sel)
    sel = sel.reshape(3, Q_ROWS * K_ROWS, n_sel)
    n_lh = rpb.shape[0]
    rpb_pad = jnp.concatenate(
        [jnp.pad(rpb, ((0, 0), (0, 0), (0, LANES - n_dc))), jnp.full((n_lh, 1, LANES), NEG, F32)],
        axis=1)
    nq = Q_ROWS * GRID_W
    nk = K_ROWS * KEY_COLS
    return pl.pallas_call(
        _bias_table_kernel,
        grid=(n_lh, 3),
        in_specs=[
            pl.BlockSpec((None, Q_ROWS * K_ROWS, n_sel), lambda h, t: (t, 0, 0)),
            pl.BlockSpec((None, n_sel, LANES), lambda h, t: (h, 0, 0)),
        ],
        out_specs=pl.BlockSpec((None, None, nq, nk), lambda h, t: (h, t, 0, 0)),
        out_shape=jax.ShapeDtypeStruct((n_lh, 3, nq, nk), F32),
        compiler_params=_params(("arbitrary", "arbitrary")),
        name="bias_table",
    )(jnp.asarray(sel), rpb_pad)


def _nbr_attn_kernel(q_ref, k0, k1, k2, k3, v0, v1, v2, v3, ck_ref, cv_ref, bias_ref, gb_ref,
                     o_ref, o_scr):
    k_refs = (k0, k1, k2, k3)
    v_refs = (v0, v1, v2, v3)
    rows_per_ref = K_ROWS // len(k_refs)

    def window(refs, h, kc0):
        parts = []
        for kr in range(K_ROWS):
            ref = refs[kr // rows_per_ref]
            parts.append(ref[h, pl.ds((kr % rows_per_ref) * GRID_W + kc0, KEY_COLS), :])
        return jnp.concatenate(parts, axis=0)

    def query_rows(ref, h, c0, nc):
        return jnp.concatenate(
            [ref[h, pl.ds(qr * GRID_W + c0, nc), :] for qr in range(Q_ROWS)], axis=0)

    def heads(g, carry):
        units = [(g * NBR_HEAD_GROUP + n, blk) for n in range(NBR_HEAD_GROUP) for blk in COL_BLOCKS]
        scores = []
        for h, (c0, nc, kc0) in units:
            qi = query_rows(q_ref, h, c0, nc).astype(BF16)
            ki = window(k_refs, h, kc0)
            s_loc = _dot_nt(qi, ki) * SCALE_LOG2E + query_rows(bias_ref, h, c0, nc)
            s_ctx = _dot_nt(qi, ck_ref[h].astype(BF16)) * SCALE_LOG2E
            scores.append((s_loc, s_ctx))
        probs = []
        for s_loc, s_ctx in scores:
            m = jnp.maximum(jnp.max(s_loc, axis=-1, keepdims=True),
                            jnp.max(s_ctx, axis=-1, keepdims=True))
            p_loc = jnp.exp2(s_loc - m)
            p_ctx = jnp.exp2(s_ctx - m)
            l = jnp.sum(p_loc, axis=-1, keepdims=True) + jnp.sum(p_ctx, axis=-1, keepdims=True)
            probs.append((p_loc.astype(BF16), p_ctx.astype(BF16), l))
        for (h, (c0, nc, kc0)), (p_loc, p_ctx, l) in zip(units, probs):
            vi = window(v_refs, h, kc0)
            o = (_dot(p_loc, vi) + _dot(p_ctx, cv_ref[h].astype(BF16))) / l
            for qr in range(Q_ROWS):
                o_scr[h, pl.ds(qr * GRID_W + c0, nc), :] = o[qr * nc:(qr + 1) * nc]
        return carry

    lax.fori_loop(0, N_HEADS // NBR_HEAD_GROUP, heads, 0)
    _write_normed_heads(o_scr, gb_ref, o_ref)


def _nbr_attention(q, kv, cache_k, cache_v, layer, bias_tab, gb_a, n_req, seq_len):
    rows = seq_len // GRID_W
    n_rb = rows // Q_ROWS
    tq = Q_ROWS * GRID_W
    n_kref = 4
    tk = (K_ROWS // n_kref) * GRID_W
    kblk_per_req = seq_len // tk
    ctx_len = cache_k.shape[3]

    def kmap(part, m):
        def f(b, jb):
            start = jnp.clip(2 * jb - 1, 0, kblk_per_req - n_kref)
            return (part, b * kblk_per_req + start + m, 0)
        return f

    kspecs = [pl.BlockSpec((N_HEADS, tk, HEAD_DIM), kmap(0, m)) for m in range(n_kref)]
    vspecs = [pl.BlockSpec((N_HEADS, tk, HEAD_DIM), kmap(1, m)) for m in range(n_kref)]
    cspec = pl.BlockSpec((None, None, N_HEADS, ctx_len, HEAD_DIM), lambda b, jb: (b, layer, 0, 0, 0))
    kind = lambda b, jb: (layer, (jb > 0).astype(I32) + (jb == n_rb - 1).astype(I32), 0, 0)
    return pl.pallas_call(
        _nbr_attn_kernel,
        grid=(n_req, n_rb),
        in_specs=[pl.BlockSpec((N_HEADS, tq, HEAD_DIM), lambda b, jb: (0, b * n_rb + jb, 0))]
        + kspecs + vspecs + [
            cspec, cspec,
            pl.BlockSpec((N_HEADS, None, tq, K_ROWS * KEY_COLS), kind),
            pl.BlockSpec((None, 1, D_A), lambda b, jb: (layer, 0, 0)),
        ],
        out_specs=pl.BlockSpec((tq, D_A), lambda b, jb: (b * n_rb + jb, 0)),
        out_shape=jax.ShapeDtypeStruct((n_req * seq_len, D_A), BF16),
        scratch_shapes=[pltpu.VMEM((N_HEADS, tq, HEAD_DIM), F32)],
        compiler_params=_params(("arbitrary", "arbitrary")),
        name="nbr_attention",
    )(q, *([kv] * (2 * n_kref)), cache_k, cache_v, bias_tab, gb_a)


def _mixers_kernel(u_ref, gv_ref, p_ref, pprev_ref, pnext_ref, gsgu_ref, wsgu_ref, bsgu_ref,
                   wpool_ref, spool_ref, gb_ref, o_ref, ext_scr, *, seq_len):
    i = pl.program_id(0)
    tm = u_ref.shape[0]
    blocks_per_seq = seq_len // tm
    bi = i % blocks_per_seq
    t0 = bi * tm

    gu = jax.nn.gelu(u_ref[...])
    gg = _rms(jax.nn.gelu(gv_ref[...]), gsgu_ref[...]).astype(BF16)
    cols = []
    for g in range(N_GROUPS_B):
        sl = slice(g * CHUNK, (g + 1) * CHUNK)
        chunks = []
        for n in range(tm // CHUNK):
            rows = slice(n * CHUNK, (n + 1) * CHUNK)
            chunks.append(_dot(wsgu_ref[g], gg[rows, sl]) + bsgu_ref[:, g:g + 1])
        cols.append(jnp.concatenate(chunks, axis=0))
    o_b = gu * jnp.concatenate(cols, axis=1)
    o_ref[:, 0:D_B] = _rms(o_b, gb_ref[:, 0:D_B]).astype(o_ref.dtype)

    p = p_ref[...]
    ext_scr[pl.ds(0, POOL_HALO), :] = jnp.where(bi > 0, pprev_ref[...], 0.0)
    ext_scr[pl.ds(POOL_HALO, tm), :] = p
    ext_scr[pl.ds(POOL_HALO + tm, POOL_HALO), :] = jnp.where(
        bi < blocks_per_seq - 1, pnext_ref[...], 0.0)
    t = t0 + lax.broadcasted_iota(I32, (tm, 1), 0)
    outs = []
    for g, w in enumerate(POOL_WINDOWS):
        half = w // 2
        sl = slice(g * CHUNK, (g + 1) * CHUNK)
        acc = None
        for d in range(-half, half):
            part = ext_scr[pl.ds(POOL_HALO + d, tm), sl]
            acc = part if acc is None else acc + part
        cnt = (jnp.minimum(t + half, seq_len) - jnp.maximum(t - half, 0)).astype(F32)
        pooled = acc / cnt - p[:, sl]
        outs.append(_dot(pooled.astype(BF16), wpool_ref[g]))
    o_c = jnp.concatenate(outs, axis=1) * spool_ref[...]
    o_ref[:, D_B:D_B + D_C] = _rms(o_c, gb_ref[:, D_B:D_B + D_C]).astype(o_ref.dtype)


def _mixers(ugp, g_sgu, w_sgu, b_sgu_t, w_pool, s_pool, gb_bc, layer, seq_len):
    ntok = ugp.shape[0]
    tm = min(TM_MIX, seq_len)
    nblk = ntok // tm
    hb = tm // POOL_HALO
    n_halo = ntok // POOL_HALO
    blk = lambda part: pl.BlockSpec((tm, D_B), lambda i: (i, part))
    full = lambda shape: pl.BlockSpec((None,) + shape, lambda i: (layer,) + (0,) * len(shape))
    return pl.pallas_call(
        functools.partial(_mixers_kernel, seq_len=seq_len),
        grid=(nblk,),
        in_specs=[
            blk(0), blk(1), blk(2),
            pl.BlockSpec((POOL_HALO, D_C), lambda i: (jnp.maximum(i * hb - 1, 0), 2)),
            pl.BlockSpec((POOL_HALO, D_C), lambda i: (jnp.minimum((i + 1) * hb, n_halo - 1), 2)),
            full((1, D_B)), full((N_GROUPS_B, CHUNK, CHUNK)), full((CHUNK, N_GROUPS_B)),
            full((len(POOL_WINDOWS), CHUNK, CHUNK)), full((1, D_C)), full((1, D_B + D_C)),
        ],
        out_specs=pl.BlockSpec((tm, D_B + D_C), lambda i: (i, 0)),
        out_shape=jax.ShapeDtypeStruct((ntok, D_B + D_C), BF16),
        scratch_shapes=[pltpu.VMEM((tm + 2 * POOL_HALO, D_C), F32)],
        compiler_params=_params(("arbitrary",)),
        name="mixers",
    )(ugp, ugp, ugp, ugp, ugp, g_sgu, w_sgu, b_sgu_t, w_pool, s_pool, gb_bc)


def _out_proj_kernel(x_ref, oa_ref, obc_ref, mod_ref, g_ref, w_ref, wr_ref, br_ref,
                     x1_ref, h_ref, e_ref, gate_ref):
    chunks = [pl.ds(m * OUT_CHUNK, OUT_CHUNK) for m in range(TM_OUT // OUT_CHUNK)]
    mixes = [_dot(jnp.concatenate([oa_ref[rows, :], obc_ref[rows, :]], axis=1), w_ref[...])
             for rows in chunks]
    hs = []
    for rows, mix in zip(chunks, mixes):
        x1 = x_ref[rows, :] + mod_ref[2:3, :] * mix
        x1_ref[rows, :] = x1
        h = _rms(x1, g_ref[...]) * (1.0 + mod_ref[4:5, :]) + mod_ref[3:4, :]
        hb = h.astype(BF16)
        h_ref[rows] = _pack_bf16_pairs(hb).reshape((OUT_CHUNK,) + ROW_TILE)
        hs.append(hb)
    logits = [_dot(hb, wr_ref[...]) + br_ref[...] for hb in hs]
    for rows, lg in zip(chunks, logits):
        _route_rows(rows, lg, e_ref, gate_ref)


def _route_rows(rows, logits, e_ref, gate_ref):
    lane = lax.broadcasted_iota(I32, logits.shape, 1)
    big = jnp.int32(LANES)

    def softmax_masked(mask):
        z = jnp.where(mask, logits, NEG)
        m = jnp.max(z, axis=-1, keepdims=True)
        e = jnp.where(mask, jnp.exp(z - m), 0.0)
        return e / jnp.sum(e, axis=-1, keepdims=True)

    def top1(vals, mask):
        v = jnp.where(mask, vals, -1.0)
        best = jnp.max(v, axis=-1, keepdims=True)
        idx = jnp.min(jnp.where(mask & (v == best), lane, big), axis=-1, keepdims=True)
        return best, idx

    gmask = lane < N_GROUPS_E
    p_grp = softmax_masked(gmask)
    p_g, g_sel = top1(p_grp, gmask)
    lo = N_GROUPS_E + EXPERTS_PER_GROUP * g_sel
    emask = (lane >= lo) & (lane < lo + EXPERTS_PER_GROUP)
    p_exp = softmax_masked(emask)
    p1, i1 = top1(p_exp, emask)
    p2, i2 = top1(p_exp, emask & (lane != i1))
    denom = p1 + p2
    g1 = p_g * p1 / denom
    g2 = p_g * p2 / denom
    e_ref[rows, :] = jnp.where(lane == 0, i1 - N_GROUPS_E,
                               jnp.where(lane == 1, i2 - N_GROUPS_E, 0))
    gate_ref[rows, :] = jnp.where(lane == 0, g1, jnp.where(lane == 1, g2, 0.0))


def _out_proj(x2d, ntok, o_a, o_bc, mod, mod_row_fn, g_ffn, w_out, w_r, b_r, layer):
    nblk = ntok // TM_OUT
    full = lambda shape: pl.BlockSpec((None,) + shape, lambda i: (layer,) + (0,) * len(shape))
    return pl.pallas_call(
        _out_proj_kernel,
        grid=(nblk,),
        in_specs=[
            pl.BlockSpec((TM_OUT, D_MODEL), lambda i: (i, 0)),
            pl.BlockSpec((TM_OUT, D_A), lambda i: (i, 0)),
            pl.BlockSpec((TM_OUT, D_B + D_C), lambda i: (i, 0)),
            pl.BlockSpec((None, None, N_MOD, D_MODEL), lambda i: (layer, mod_row_fn(i), 0, 0)),
            full((1, D_MODEL)), full((D_MODEL, D_MODEL)), full((D_MODEL, LANES)), full((1, LANES)),
        ],
        out_specs=[
            pl.BlockSpec((TM_OUT, D_MODEL), lambda i: (i, 0)),
            pl.BlockSpec((TM_OUT,) + ROW_TILE, lambda i: (i, 0, 0)),
            pl.BlockSpec((TM_OUT, LANES), lambda i: (i, 0)),
            pl.BlockSpec((TM_OUT, LANES), lambda i: (i, 0)),
        ],
        out_shape=[
            jax.ShapeDtypeStruct((ntok, D_MODEL), F32),
            jax.ShapeDtypeStruct((ntok,) + ROW_TILE, U32),
            jax.ShapeDtypeStruct((ntok, LANES), I32),
            jax.ShapeDtypeStruct((ntok, LANES), F32),
        ],
        compiler_params=_params(("arbitrary",)),
        name="out_proj_router",
    )(x2d, o_a, o_bc, mod, g_ffn, w_out, w_r, b_r)


def _route_plan(e_flat):
    n_assign = e_flat.shape[0]
    onehot = (e_flat[:, None] == jnp.arange(N_EXPERTS, dtype=I32)[None, :]).astype(I32)
    csum = jnp.cumsum(onehot, axis=0)
    counts = csum[-1]
    rank = jnp.sum(onehot * (csum - 1), axis=1)
    padded = (counts + MOE_BM - 1) // MOE_BM * MOE_BM
    pad_end = jnp.cumsum(padded)
    pad_start = pad_end - padded
    pos = jnp.sum(onehot * pad_start[None, :], axis=1) + rank
    n_blocks = n_assign // MOE_BM + N_EXPERTS
    starts = jnp.arange(n_blocks, dtype=I32) * MOE_BM
    block_e = jnp.minimum(jnp.sum((starts[:, None] >= pad_end[None, :]).astype(I32), axis=1),
                          N_EXPERTS - 1).astype(I32)
    n_used = (pad_end[-1] // MOE_BM).astype(I32).reshape(1)
    fill_start = jnp.where(padded > 0, pad_end - MOE_BM, -1).astype(I32)
    has_rows = padded > 0
    block_first = jnp.concatenate(
        [jnp.ones((1,), I32), (block_e[1:] != block_e[:-1]).astype(I32)])
    ids = jnp.arange(N_EXPERTS, dtype=I32)
    later = jnp.where(has_rows[None, :] & (ids[None, :] > ids[:, None]), ids[None, :], N_EXPERTS)
    next_e = jnp.min(later, axis=1)
    next_e = jnp.where(next_e < N_EXPERTS, next_e, -1).astype(I32)
    buf_e = ((jnp.cumsum(has_rows.astype(I32)) - 1) % 2).astype(I32)
    plan = (block_e, n_used, block_first, next_e, buf_e)
    return pos.astype(I32), plan, n_used, fill_start, n_blocks


def _dispatch_kernel(fill_ref, nu_ref, pos_ref, hp_ref, hs_ref, xs_ref, zero_scr, sem, fill_sem,
                     *, n_ctx_blocks, n_blocks):
    i = pl.program_id(0)

    @pl.when(i == 0)
    def _():
        zero_scr[...] = jnp.zeros_like(zero_scr)

        def fill(start):
            start = pl.multiple_of(start, MOE_BM)
            return pltpu.make_async_copy(zero_scr, xs_ref.at[pl.ds(start, MOE_BM)], fill_sem)

        def expert_fill(e, carry):
            pl.when(fill_ref[e] >= 0)(lambda: fill(fill_ref[e]).start())
            return carry

        def expert_wait(e, carry):
            pl.when(fill_ref[e] >= 0)(lambda: fill(0).wait())
            return carry

        def tail_fill(b, carry):
            fill(b * MOE_BM).start()
            return carry

        def tail_wait(b, carry):
            fill(0).wait()
            return carry

        lax.fori_loop(0, N_EXPERTS, expert_fill, 0)
        lax.fori_loop(nu_ref[0], n_blocks, tail_fill, 0)
        lax.fori_loop(0, N_EXPERTS, expert_wait, 0)
        lax.fori_loop(nu_ref[0], n_blocks, tail_wait, 0)

    def scatter(h_ref):
        def issue(r, carry):
            for k in range(TOP_K):
                slot = pos_ref[0, 0, TOP_K * r + k]
                pltpu.make_async_copy(h_ref.at[r], xs_ref.at[slot], sem).start(priority=k)
            return carry

        lax.fori_loop(0, TM_ROW, issue, 0, unroll=8)
        for _ in range(TOP_K):
            pltpu.make_async_copy(h_ref, h_ref, sem).wait()

    pl.when(i < n_ctx_blocks)(lambda: scatter(hp_ref))
    pl.when(i >= n_ctx_blocks)(lambda: scatter(hs_ref))


def _dispatch(fill_start, n_used, pos3, hp, hs, n_blocks):
    nbp = hp.shape[0] // TM_ROW
    nbs = hs.shape[0] // TM_ROW
    grid_spec = pltpu.PrefetchScalarGridSpec(
        num_scalar_prefetch=2,
        grid=(nbp + nbs,),
        in_specs=[
            pl.BlockSpec((1, 1, TOP_K * TM_ROW), lambda i, fs, nu: (i, 0, 0),
                         memory_space=pltpu.SMEM),
            pl.BlockSpec((TM_ROW,) + ROW_TILE, lambda i, fs, nu: (jnp.minimum(i, nbp - 1), 0, 0)),
            pl.BlockSpec((TM_ROW,) + ROW_TILE, lambda i, fs, nu: (jnp.maximum(i - nbp, 0), 0, 0)),
        ],
        out_specs=pl.BlockSpec(memory_space=pl.ANY),
        scratch_shapes=[pltpu.VMEM((MOE_BM,) + ROW_TILE, U32), pltpu.SemaphoreType.DMA(()),
                        pltpu.SemaphoreType.DMA(())],
    )
    return pl.pallas_call(
        functools.partial(_dispatch_kernel, n_ctx_blocks=nbp, n_blocks=n_blocks),
        grid_spec=grid_spec,
        out_shape=jax.ShapeDtypeStruct((n_blocks * MOE_BM,) + ROW_TILE, U32),
        compiler_params=_params(("arbitrary",)),
        name="moe_dispatch",
    )(fill_start, n_used, pos3, hp, hs)


def _experts_kernel(be_ref, nu_ref, first_ref, next_ref, buf_ref, xs_ref, wg_hbm, wu_hbm, wd_hbm,
                    ys_ref, wg_buf, wu_buf, wd_buf, sem, *, layer):
    i = pl.program_id(0)

    def weight_copies(e, b):
        return [pltpu.make_async_copy(src.at[layer, e], dst.at[b], sem.at[b])
                for src, dst in ((wg_hbm, wg_buf), (wu_hbm, wu_buf), (wd_hbm, wd_buf))]

    @pl.when(i < nu_ref[0])
    def _():
        e = be_ref[i]
        b = buf_ref[e]

        @pl.when(i == 0)
        def _():
            for cp in weight_copies(e, b):
                cp.start()

        @pl.when(first_ref[i] == 1)
        def _():
            nxt = next_ref[e]

            @pl.when(nxt >= 0)
            def _():
                for cp in weight_copies(nxt, 1 - b):
                    cp.start()

            for cp in weight_copies(e, b):
                cp.wait()

        x = _unpack_bf16_pairs(xs_ref[...].reshape(MOE_BM, D_PACK), BF16)
        g = _dot(x, wg_buf[b].astype(BF16))
        u = _dot(x, wu_buf[b].astype(BF16))
        a = (g * jax.nn.sigmoid(g) * u).astype(BF16)
        y = _dot(a, wd_buf[b].astype(BF16)).astype(BF16)
        ys_ref[...] = _pack_bf16_pairs(y).reshape(ys_ref.shape)

    @pl.when(i >= nu_ref[0])
    def _():
        ys_ref[...] = jnp.zeros_like(ys_ref)


def _experts(plan, xs, wg, wu, wd, layer, n_blocks):
    grid_spec = pltpu.PrefetchScalarGridSpec(
        num_scalar_prefetch=len(plan),
        grid=(n_blocks,),
        in_specs=[
            pl.BlockSpec((MOE_BM,) + ROW_TILE,
                         lambda i, be, nu, *_: (jnp.minimum(i, nu[0] - 1), 0, 0)),
            pl.BlockSpec(memory_space=pl.ANY),
            pl.BlockSpec(memory_space=pl.ANY),
            pl.BlockSpec(memory_space=pl.ANY),
        ],
        out_specs=pl.BlockSpec((MOE_BM,) + ROW_TILE, lambda i, *_: (i, 0, 0)),
        scratch_shapes=[
            pltpu.VMEM((2, D_MODEL, D_EXPERT), F32),
            pltpu.VMEM((2, D_MODEL, D_EXPERT), F32),
            pltpu.VMEM((2, D_EXPERT, D_MODEL), F32),
            pltpu.SemaphoreType.DMA((2,)),
        ],
    )
    return pl.pallas_call(
        functools.partial(_experts_kernel, layer=layer),
        grid_spec=grid_spec,
        out_shape=jax.ShapeDtypeStruct(xs.shape, U32),
        compiler_params=_params(("arbitrary",)),
        name="moe_experts",
    )(*plan, xs, wg, wu, wd)


def _combine_kernel(pos_ref, pos_next_ref, x1_ref, gate_ref, mod_ref, gfin_ref, ys_ref, o_ref,
                    ybuf, sem, *, final_norm):
    i = pl.program_id(0)
    buf = i % 2
    last = i == pl.num_programs(0) - 1

    def start_row(p_ref, b, r):
        for k in range(TOP_K):
            slot = p_ref[0, 0, TOP_K * r + k]
            pltpu.make_async_copy(ys_ref.at[slot], ybuf.at[b, k, r], sem.at[b]).start(priority=k)

    def wait_buffer(b):
        for k in range(TOP_K):
            pltpu.make_async_copy(ybuf.at[b, k], ybuf.at[b, k], sem.at[b]).wait()

    @pl.when(i == 0)
    def _():
        lax.fori_loop(0, TM_ROW, lambda r, c: (start_row(pos_ref, 0, r), c)[1], 0, unroll=8)

    wait_buffer(buf)
    for m in range(TM_ROW // COMBINE_CHUNK):
        for r in range(m * COMBINE_CHUNK, (m + 1) * COMBINE_CHUNK):
            start_row(pos_next_ref, 1 - buf, r)
        rows = pl.ds(m * COMBINE_CHUNK, COMBINE_CHUNK)
        y0 = _unpack_bf16_pairs(ybuf[buf, 0, rows].reshape(COMBINE_CHUNK, D_PACK), F32)
        y1 = _unpack_bf16_pairs(ybuf[buf, 1, rows].reshape(COMBINE_CHUNK, D_PACK), F32)
        f = gate_ref[rows, 0:1] * y0 + gate_ref[rows, 1:2] * y1
        x2 = x1_ref[rows, :] + mod_ref[5:6, :] * f
        if final_norm:
            x2 = _rms(x2, gfin_ref[...])
        o_ref[rows, :] = x2

    pl.when(last)(lambda: wait_buffer(1 - buf))


def _combine(pos3, x1, gate, mod, mod_row_fn, g_final, ys, layer, final_norm):
    ntok = x1.shape[0]
    nblk = ntok // TM_ROW
    return pl.pallas_call(
        functools.partial(_combine_kernel, final_norm=final_norm),
        grid=(nblk,),
        in_specs=[
            pl.BlockSpec((1, 1, TOP_K * TM_ROW), lambda i: (i, 0, 0), memory_space=pltpu.SMEM),
            pl.BlockSpec((1, 1, TOP_K * TM_ROW), lambda i: (jnp.minimum(i + 1, nblk - 1), 0, 0),
                         memory_space=pltpu.SMEM),
            pl.BlockSpec((TM_ROW, D_MODEL), lambda i: (i, 0)),
            pl.BlockSpec((TM_ROW, LANES), lambda i: (i, 0)),
            pl.BlockSpec((None, None, N_MOD, D_MODEL), lambda i: (layer, mod_row_fn(i), 0, 0)),
            pl.BlockSpec((1, D_MODEL), lambda i: (0, 0)),
            pl.BlockSpec(memory_space=pl.ANY),
        ],
        out_specs=pl.BlockSpec((TM_ROW, D_MODEL), lambda i: (i, 0)),
        out_shape=jax.ShapeDtypeStruct((ntok, D_MODEL), F32),
        scratch_shapes=[pltpu.VMEM((2, TOP_K, TM_ROW) + ROW_TILE, U32),
                        pltpu.SemaphoreType.DMA((2,))],
        compiler_params=_params(("arbitrary",)),
        name="moe_combine",
    )(pos3, pos3, x1, gate, mod, g_final, ys)


def kernel(x_prompt, x_sample, cache_k, cache_v, c, c_ctx, w_mod, b_mod, g_mix, g_ffn, w_in, rpb,
           g_sgu, w_sgu, b_sgu, w_pool, s_pool, g_branch, w_out, w_rg, b_rg, w_re, b_re,
           w_e_gate, w_e_up, w_e_down, g_final):
    n_ctx, seq, _ = x_prompt.shape
    n_lat, lat_seq, _ = x_sample.shape
    ntok_p = n_ctx * seq
    ntok_s = n_lat * lat_seq
    rows = lat_seq // GRID_W

    n_mod_rows = -(-(1 + n_lat) // SUBLANES) * SUBLANES
    c_all = jnp.concatenate(
        [c_ctx[None], c, jnp.zeros((n_mod_rows - 1 - n_lat, D_MODEL), F32)], axis=0)
    mod = _modulation(c_all, w_mod, b_mod).reshape(DEPTH, n_mod_rows, N_MOD, D_MODEL)

    xp = x_prompt.reshape(ntok_p, D_MODEL)
    xs_lat = x_sample.reshape(ntok_s, D_MODEL)

    def lat_mod_row(tm):
        return lambda i: 1 + (i * tm) // lat_seq

    ctx_mod_row = lambda i: 0

    w_in_b = w_in.astype(BF16)
    w_out_b = w_out.astype(BF16)
    w_sgu_b = w_sgu.astype(BF16)
    w_pool_b = w_pool.astype(BF16)

    gmix = g_mix[:, None, :]
    gffn = g_ffn[:, None, :]
    gb_a = g_branch[:, None, :D_A]
    gb_bc = g_branch[:, None, D_A:]
    gsgu = g_sgu[:, None, :]
    bsgu_t = jnp.swapaxes(b_sgu, 1, 2)
    spool = s_pool[:, None, :]
    n_route = N_GROUPS_E + N_EXPERTS
    w_r = jnp.concatenate(
        [w_rg, w_re, jnp.zeros((DEPTH, D_MODEL, LANES - n_route), F32)], axis=2)
    w_r = w_r.astype(BF16)
    b_r = jnp.concatenate(
        [b_rg, b_re, jnp.zeros((DEPTH, LANES - n_route), F32)], axis=1)[:, None, :]
    bias_tab = _local_bias_table(rpb.reshape((DEPTH * N_HEADS,) + rpb.shape[2:]), rows)

    cache_shape = (n_ctx, DEPTH, N_HEADS, seq, HEAD_DIM)
    caches = (jnp.zeros(cache_shape, F32), jnp.zeros(cache_shape, F32))
    for l in range(DEPTH):
        qkv_p, ugp_p, *caches = _in_proj(
            xp, ntok_p, mod, ctx_mod_row, gmix, w_in_b, l, seq, True, tuple(caches))
        oa_p = _ctx_attention(qkv_p, gb_a, l, seq)
        obc_p = _mixers(ugp_p, gsgu, w_sgu_b, bsgu_t, w_pool_b, spool, gb_bc, l, seq)
        x1p, hp, ep, gatep = _out_proj(
            xp, ntok_p, oa_p, obc_p, mod, ctx_mod_row, gffn, w_out_b, w_r, b_r, l)

        q_s, kv_s, ugp_s = _in_proj(
            xs_lat, ntok_s, mod, lat_mod_row(TM_IN), gmix, w_in_b, l, lat_seq, False)
        oa_s = _nbr_attention(q_s, kv_s, cache_k, cache_v, l, bias_tab, gb_a, n_lat, lat_seq)
        obc_s = _mixers(ugp_s, gsgu, w_sgu_b, bsgu_t, w_pool_b, spool, gb_bc, l, lat_seq)
        x1s, hs, es, gates = _out_proj(
            xs_lat, ntok_s, oa_s, obc_s, mod, lat_mod_row(TM_OUT), gffn, w_out_b, w_r, b_r, l)

        e_flat = jnp.concatenate([ep[:, :TOP_K].reshape(-1), es[:, :TOP_K].reshape(-1)])
        pos, plan, n_used, fill_start, n_blocks = _route_plan(e_flat)
        pos3 = pos.reshape((ntok_p + ntok_s) // TM_ROW, 1, TOP_K * TM_ROW)
        pos_p = pos3[:ntok_p // TM_ROW]
        pos_s = pos3[ntok_p // TM_ROW:]
        slots = _dispatch(fill_start, n_used, pos3, hp, hs, n_blocks)
        ys = _experts(plan, slots, w_e_gate, w_e_up, w_e_down, l, n_blocks)
        last = l == DEPTH - 1
        gfin = g_final[None]
        xp = _combine(pos_p, x1p, gatep, mod, ctx_mod_row, gfin, ys, l, last)
        xs_lat = _combine(pos_s, x1s, gates, mod, lat_mod_row(TM_ROW), gfin, ys, l, last)

    y_prompt = xp.reshape(n_ctx, seq, D_MODEL)
    y_sample = xs_lat.reshape(n_lat, lat_seq, D_MODEL)
    return (y_prompt, y_sample, caches[0], caches[1])
```

```python
import functools

import jax
import numpy as np
import jax.numpy as jnp
from jax import lax
from jax.experimental import pallas as pl
from jax.experimental.pallas import tpu as pltpu

F32 = jnp.float32
BF16 = jnp.bfloat16
I32 = jnp.int32
U32 = jnp.uint32

D_MODEL = 2048
DEPTH = 2
GRID_W = 64
HEAD_DIM = 128
D_A = 1024
N_HEADS = 8
WIN_H = 8
WIN_W = 16
D_B = 512
N_GROUPS_B = 4
CHUNK = 128
D_C = 512
POOL_WINDOWS = (2, 4, 8, 16)
IN_WIDTH = 3 * D_A + 2 * D_B + D_C
N_GROUPS_E = 4
EXPERTS_PER_GROUP = 8
N_EXPERTS = 32
TOP_K = 2
D_EXPERT = 512
N_MOD = 6
EPS = 1e-6
NEG = -1e30
LOG2E = 1.4426950408889634
SCALE_LOG2E = HEAD_DIM ** -0.5 * LOG2E

LANES = 128
SUBLANES = 8
VMEM_LIMIT_BYTES = 56 * 1024 * 1024

TM_IN = 1024
TN_IN = 512
IN_CHUNK = 256
W_SLOTS = 3
OUT_CHUNK = 256
TM_OUT = 512
TM_MIX = 512
Q_ROWS = 8
K_ROWS = 16
NBR_HEAD_GROUP = 2
MOE_BM = 512
D_PACK = D_MODEL // 2
ROW_TILE = (SUBLANES, LANES)
assert D_PACK == SUBLANES * LANES
TM_ROW = 512
COMBINE_CHUNK = 128
POOL_HALO = 8

COL_BLOCKS = ((0, 24, 0), (24, 16, 16), (40, 24, 32))
KEY_COLS = 32


def _params(sem):
    return pltpu.CompilerParams(dimension_semantics=sem, vmem_limit_bytes=VMEM_LIMIT_BYTES)


def _dot(a, b):
    return jnp.dot(a, b, preferred_element_type=F32)


def _dot_nt(a, b):
    return lax.dot_general(a, b, (((1,), (1,)), ((), ())), preferred_element_type=F32)


def _rms(x, g):
    return x * lax.rsqrt(jnp.mean(x * x, axis=-1, keepdims=True) + EPS) * g


def _pack_bf16_pairs(x):
    n = x.shape[1] // 2
    lo = lax.bitcast_convert_type(x[:, :n].astype(F32), U32) >> 16
    hi = lax.bitcast_convert_type(x[:, n:].astype(F32), U32) & jnp.uint32(0xFFFF0000)
    return hi | lo


def _unpack_bf16_pairs(w, dtype):
    lo = lax.bitcast_convert_type(w << 16, F32).astype(dtype)
    hi = lax.bitcast_convert_type(w & jnp.uint32(0xFFFF0000), F32).astype(dtype)
    return jnp.concatenate([lo, hi], axis=1)


def _mod_kernel(c_ref, w_ref, b_ref, o_ref):
    c = c_ref[...]
    s = c * jax.nn.sigmoid(c)
    n = s.shape[0]
    w = w_ref[...]
    s_hi = s.astype(BF16)
    s_lo = (s - s_hi.astype(F32)).astype(BF16)
    w_hi = w.astype(BF16)
    w_lo = (w - w_hi.astype(F32)).astype(BF16)
    r = _dot(jnp.concatenate([s_hi, s_lo], axis=0), w_hi)
    o_ref[...] = r[:n] + r[n:] + _dot(s_hi, w_lo) + b_ref[...]


def _modulation(c_all, w_mod, b_mod):
    tn = 1024
    nrow = c_all.shape[0]
    width = w_mod.shape[2]
    return pl.pallas_call(
        _mod_kernel,
        grid=(DEPTH, width // tn),
        in_specs=[
            pl.BlockSpec((nrow, D_MODEL), lambda l, j: (0, 0)),
            pl.BlockSpec((None, D_MODEL, tn), lambda l, j: (l, 0, j)),
            pl.BlockSpec((None, 1, tn), lambda l, j: (l, 0, j)),
        ],
        out_specs=pl.BlockSpec((None, nrow, tn), lambda l, j: (l, 0, j)),
        out_shape=jax.ShapeDtypeStruct((DEPTH, nrow, width), F32),
        compiler_params=_params(("arbitrary", "arbitrary")),
        name="modulation",
    )(c_all, w_mod, b_mod.reshape(DEPTH, 1, width))


def _in_proj_kernel(x_hbm, mod_ref, g_ref, w_hbm, *refs, ctx, n_aliased, layer, nblk):
    refs = refs[n_aliased:]
    if ctx:
        qkv_ref, ugp_ref, kc_ref, vc_ref, h_scr, xbuf, wbuf, xsem, wsem = refs
    else:
        q_ref, kv_ref, ugp_ref, h_scr, xbuf, wbuf, xsem, wsem = refs
    i = pl.program_id(0)
    j = pl.program_id(1)
    ncol = IN_WIDTH // TN_IN
    step = i * ncol + j
    heads = TN_IN // HEAD_DIM

    def x_copy(blk):
        slot = blk % 2
        return pltpu.make_async_copy(
            x_hbm.at[pl.ds(pl.multiple_of(blk * TM_IN, TM_IN), TM_IN)], xbuf.at[slot],
            xsem.at[slot])

    def w_copy(s):
        slot = s % W_SLOTS
        col = pl.multiple_of((s % ncol) * TN_IN, TN_IN)
        return pltpu.make_async_copy(
            w_hbm.at[layer, :, pl.ds(col, TN_IN)], wbuf.at[slot], wsem.at[slot])

    @pl.when(step == 0)
    def _():
        x_copy(0).start()
        for s in range(W_SLOTS - 1):
            w_copy(s).start()

    @pl.when((j == 0) & (i + 1 < nblk))
    def _():
        x_copy(i + 1).start()

    @pl.when(step + W_SLOTS - 1 < nblk * ncol)
    def _():
        w_copy(step + W_SLOTS - 1).start()

    @pl.when(j == 0)
    def _():
        x_copy(i).wait()

    w_copy(step).wait()
    x_ref = xbuf.at[i % 2]
    w_ref = wbuf.at[step % W_SLOTS]

    def head_store(ref, dtype):
        def store(m, rows, acc):
            for hh in range(heads):
                ref[hh, rows, :] = acc[:, hh * HEAD_DIM:(hh + 1) * HEAD_DIM].astype(dtype)
        return store

    def cache_store(ref):
        seq = ref.shape[2]
        per_chunk = IN_CHUNK // seq

        def store(m, rows, acc):
            for s in range(per_chunk):
                for hh in range(heads):
                    ref[m * per_chunk + s, hh] = acc[s * seq:(s + 1) * seq,
                                                     hh * HEAD_DIM:(hh + 1) * HEAD_DIM]
        return store

    def ugp_store(m, rows, acc):
        ugp_ref[rows, :] = acc

    def run(normalize, stores):
        for m in range(TM_IN // IN_CHUNK):
            rows = pl.ds(m * IN_CHUNK, IN_CHUNK)
            if normalize:
                h = _rms(x_ref[rows, :], g_ref[...]) * (1.0 + mod_ref[1:2, :]) + mod_ref[0:1, :]
                hb = h.astype(BF16)
                h_scr[rows, :] = hb
            else:
                hb = h_scr[rows, :]
            acc = _dot(hb, w_ref[...])
            for store in stores:
                store(m, rows, acc)

    def case(cond, normalize, stores):
        pl.when(cond)(lambda: run(normalize, stores))

    if ctx:
        case(j == 0, True, [head_store(qkv_ref, BF16)])
        case(j == 1, False, [head_store(qkv_ref, BF16)])
        case((j >= 2) & (j < 4), False, [head_store(qkv_ref, BF16), cache_store(kc_ref)])
        case((j >= 4) & (j < 6), False, [head_store(qkv_ref, BF16), cache_store(vc_ref)])
    else:
        case(j == 0, True, [head_store(q_ref, F32)])
        case(j == 1, False, [head_store(q_ref, F32)])
        case((j >= 2) & (j < 6), False, [head_store(kv_ref, BF16)])
    case(j >= 6, False, [ugp_store])


def _in_proj(x2d, ntok, mod, mod_row_fn, g_mix, w_in, layer, seq_len, ctx, caches=()):
    nblk = ntok // TM_IN
    ncol = IN_WIDTH // TN_IN
    heads = TN_IN // HEAD_DIM
    hm = lambda lo, n: (lambda i, j: (jnp.clip(j - lo, 0, n - 1), i, 0))
    hspec = lambda lo, n: pl.BlockSpec((heads, TM_IN, HEAD_DIM), hm(lo, n))
    ugp_shape = jax.ShapeDtypeStruct((ntok, 2 * D_B + D_C), F32)
    ugp_spec = pl.BlockSpec((TM_IN, TN_IN), lambda i, j: (i, jnp.clip(j - 6, 0, 2)))
    if ctx:
        nseq = ntok // seq_len
        spb = TM_IN // seq_len
        cache_shape = jax.ShapeDtypeStruct((nseq, DEPTH, N_HEADS, seq_len, HEAD_DIM), F32)
        cache_spec = lambda lo: pl.BlockSpec(
            (spb, None, heads, seq_len, HEAD_DIM),
            lambda i, j: (i, layer, jnp.clip(j - lo, 0, 1), 0, 0))
        out_shape = [jax.ShapeDtypeStruct((3 * N_HEADS, ntok, HEAD_DIM), BF16), ugp_shape,
                     cache_shape, cache_shape]
        out_specs = [hspec(0, 6), ugp_spec, cache_spec(2), cache_spec(4)]
    else:
        out_shape = [jax.ShapeDtypeStruct((N_HEADS, ntok, HEAD_DIM), F32),
                     jax.ShapeDtypeStruct((2 * N_HEADS, ntok, HEAD_DIM), BF16), ugp_shape]
        out_specs = [hspec(0, 2), hspec(2, 4), ugp_spec]
    return pl.pallas_call(
        functools.partial(_in_proj_kernel, ctx=ctx, n_aliased=len(caches), layer=layer, nblk=nblk),
        grid=(nblk, ncol),
        in_specs=[
            pl.BlockSpec(memory_space=pl.ANY),
            pl.BlockSpec((None, None, N_MOD, D_MODEL), lambda i, j: (layer, mod_row_fn(i), 0, 0)),
            pl.BlockSpec((None, 1, D_MODEL), lambda i, j: (layer, 0, 0)),
            pl.BlockSpec(memory_space=pl.ANY),
        ] + [pl.BlockSpec(memory_space=pl.ANY)] * len(caches),
        out_specs=out_specs,
        out_shape=out_shape,
        scratch_shapes=[
            pltpu.VMEM((TM_IN, D_MODEL), BF16),
            pltpu.VMEM((2, TM_IN, D_MODEL), F32),
            pltpu.VMEM((W_SLOTS, D_MODEL, TN_IN), BF16),
            pltpu.SemaphoreType.DMA((2,)),
            pltpu.SemaphoreType.DMA((W_SLOTS,)),
        ],
        input_output_aliases={4 + n: 2 + n for n in range(len(caches))},
        compiler_params=_params(("arbitrary", "arbitrary")),
        name="in_proj_ctx" if ctx else "in_proj_lat",
    )(x2d, mod, g_mix, w_in, *caches)


def _write_normed_heads(o_scr, gb_ref, o_ref):
    sq = None
    for h in range(N_HEADS):
        oh = o_scr[h]
        sq = oh * oh if sq is None else sq + oh * oh
    ss = jnp.sum(sq, axis=-1, keepdims=True)
    r = lax.rsqrt(ss * (1.0 / D_A) + EPS)
    for h in range(N_HEADS):
        sl = slice(h * HEAD_DIM, (h + 1) * HEAD_DIM)
        o_ref[:, sl] = (o_scr[h] * r * gb_ref[:, sl]).astype(o_ref.dtype)


def _ctx_attn_kernel(q_ref, k_ref, v_ref, gb_ref, o_ref, o_scr):
    group = 4

    def heads(g, carry):
        hs = [g * group + n for n in range(group)]
        scores = [_dot_nt(q_ref[h], k_ref[h]) * SCALE_LOG2E for h in hs]
        probs = []
        for s in scores:
            p = jnp.exp2(s - jnp.max(s, axis=-1, keepdims=True))
            probs.append((p.astype(BF16), jnp.sum(p, axis=-1, keepdims=True)))
        for h, (p, l) in zip(hs, probs):
            o_scr[h] = _dot(p, v_ref[h]) / l
        return carry

    lax.fori_loop(0, N_HEADS // group, heads, 0)
    _write_normed_heads(o_scr, gb_ref, o_ref)


def _ctx_attention(qkv, gb_a, layer, seq_len):
    ntok = qkv.shape[1]
    spec = lambda part: pl.BlockSpec((N_HEADS, seq_len, HEAD_DIM), lambda b: (part, b, 0))
    return pl.pallas_call(
        _ctx_attn_kernel,
        grid=(ntok // seq_len,),
        in_specs=[spec(0), spec(1), spec(2),
                  pl.BlockSpec((None, 1, D_A), lambda b: (layer, 0, 0))],
        out_specs=pl.BlockSpec((seq_len, D_A), lambda b: (b, 0)),
        out_shape=jax.ShapeDtypeStruct((ntok, D_A), BF16),
        scratch_shapes=[pltpu.VMEM((N_HEADS, seq_len, HEAD_DIM), F32)],
        compiler_params=_params(("arbitrary",)),
        name="ctx_attention",
    )(qkv, qkv, qkv, gb_a)


def _bias_table_kernel(sel_ref, rpb_ref, o_ref):
    kr_per_tile = LANES // KEY_COLS
    by_row = jnp.dot(sel_ref[...], rpb_ref[...], preferred_element_type=F32,
                     precision=lax.Precision.HIGHEST)
    for (c0, nc, kc0) in COL_BLOCKS:
        lane = lax.broadcasted_iota(I32, (nc, LANES), 1)
        c = c0 + lax.broadcasted_iota(I32, (nc, LANES), 0)
        cs = jnp.clip(c - WIN_W // 2, 0, GRID_W - WIN_W)
        masks = []
        for k4 in range(kr_per_tile):
            kc = kc0 + lane - k4 * KEY_COLS
            masks.append((lane >= k4 * KEY_COLS) & (lane < (k4 + 1) * KEY_COLS)
                         & (kc >= cs) & (kc < cs + WIN_W))
        for qr in range(Q_ROWS):
            for t in range(K_ROWS // kr_per_tile):
                tile = jnp.full((nc, LANES), NEG, F32)
                for k4 in range(kr_per_tile):
                    row = qr * K_ROWS + t * kr_per_tile + k4
                    x = jnp.broadcast_to(by_row[row:row + 1, :], (nc, LANES))
                    shift = (k4 * KEY_COLS - kc0 - (WIN_W - 1) + c0) % LANES
                    x = pltpu.roll(x, shift, 1, stride=1, stride_axis=0)
                    tile = jnp.where(masks[k4], x * LOG2E, tile)
                o_ref[pl.ds(qr * GRID_W + c0, nc), pl.ds(t * LANES, LANES)] = tile


def _local_bias_table(rpb, rows):
    n_rb = rows // Q_ROWS
    n_dr = 2 * WIN_H - 1
    n_dc = 2 * WIN_W - 1
    n_sel = n_dr + 1
    sel = np.zeros((3, Q_ROWS, K_ROWS, n_sel), np.float32)
    for t, jb in enumerate((0, 1, n_rb - 1)):
        r0 = Q_ROWS * jb
        ks = min(max(r0 - WIN_H // 2, 0), rows - K_ROWS)
        r = r0 + np.arange(Q_ROWS)[:, None]
        kr = ks + np.arange(K_ROWS)[None, :]
        rs = np.clip(r - WIN_H // 2, 0, rows - WIN_H)
        valid = (kr >= rs) & (kr < rs + WIN_H)
        ri = np.where(valid, kr - r + WIN_H - 1, n_dr)
        sel[t] = ri[..., None] == np.arange(n_sel)
    sel = sel.reshape(3, Q_ROWS * K_ROWS, n_sel)
    n_lh = rpb.shape[0]
    rpb_pad = jnp.concatenate(
        [jnp.pad(rpb, ((0, 0), (0, 0), (0, LANES - n_dc))), jnp.full((n_lh, 1, LANES), NEG, F32)],
        axis=1)
    nq = Q_ROWS * GRID_W
    nk = K_ROWS * KEY_COLS
    return pl.pallas_call(
        _bias_table_kernel,
        grid=(n_lh, 3),
        in_specs=[
            pl.BlockSpec((None, Q_ROWS * K_ROWS, n_sel), lambda h, t: (t, 0, 0)),
            pl.BlockSpec((None, n_sel, LANES), lambda h, t: (h, 0, 0)),
        ],
        out_specs=pl.BlockSpec((None, None, nq, nk), lambda h, t: (h, t, 0, 0)),
        out_shape=jax.ShapeDtypeStruct((n_lh, 3, nq, nk), F32),
        compiler_params=_params(("arbitrary", "arbitrary")),
        name="bias_table",
    )(jnp.asarray(sel), rpb_pad)


def _nbr_attn_kernel(q_ref, k0, k1, k2, k3, v0, v1, v2, v3, ck_ref, cv_ref, bias_ref, gb_ref,
                     o_ref, o_scr):
    k_refs = (k0, k1, k2, k3)
    v_refs = (v0, v1, v2, v3)
    rows_per_ref = K_ROWS // len(k_refs)

    def window(refs, h, kc0):
        parts = []
        for kr in range(K_ROWS):
            ref = refs[kr // rows_per_ref]
            parts.append(ref[h, pl.ds((kr % rows_per_ref) * GRID_W + kc0, KEY_COLS), :])
        return jnp.concatenate(parts, axis=0)

    def query_rows(ref, h, c0, nc):
        return jnp.concatenate(
            [ref[h, pl.ds(qr * GRID_W + c0, nc), :] for qr in range(Q_ROWS)], axis=0)

    def heads(g, carry):
        units = [(g * NBR_HEAD_GROUP + n, blk) for n in range(NBR_HEAD_GROUP) for blk in COL_BLOCKS]
        scores = []
        for h, (c0, nc, kc0) in units:
            qi = query_rows(q_ref, h, c0, nc).astype(BF16)
            ki = window(k_refs, h, kc0)
            s_loc = _dot_nt(qi, ki) * SCALE_LOG2E + query_rows(bias_ref, h, c0, nc)
            s_ctx = _dot_nt(qi, ck_ref[h].astype(BF16)) * SCALE_LOG2E
            scores.append((s_loc, s_ctx))
        probs = []
        for s_loc, s_ctx in scores:
            m = jnp.maximum(jnp.max(s_loc, axis=-1, keepdims=True),
                            jnp.max(s_ctx, axis=-1, keepdims=True))
            p_loc = jnp.exp2(s_loc - m)
            p_ctx = jnp.exp2(s_ctx - m)
            l = jnp.sum(p_loc, axis=-1, keepdims=True) + jnp.sum(p_ctx, axis=-1, keepdims=True)
            probs.append((p_loc.astype(BF16), p_ctx.astype(BF16), l))
        for (h, (c0, nc, kc0)), (p_loc, p_ctx, l) in zip(units, probs):
            vi = window(v_refs, h, kc0)
            o = (_dot(p_loc, vi) + _dot(p_ctx, cv_ref[h].astype(BF16))) / l
            for qr in range(Q_ROWS):
                o_scr[h, pl.ds(qr * GRID_W + c0, nc), :] = o[qr * nc:(qr + 1) * nc]
        return carry

    lax.fori_loop(0, N_HEADS // NBR_HEAD_GROUP, heads, 0)
    _write_normed_heads(o_scr, gb_ref, o_ref)


def _nbr_attention(q, kv, cache_k, cache_v, layer, bias_tab, gb_a, n_req, seq_len):
    rows = seq_len // GRID_W
    n_rb = rows // Q_ROWS
    tq = Q_ROWS * GRID_W
    n_kref = 4
    tk = (K_ROWS // n_kref) * GRID_W
    kblk_per_req = seq_len // tk
    ctx_len = cache_k.shape[3]

    def kmap(part, m):
        def f(b, jb):
            start = jnp.clip(2 * jb - 1, 0, kblk_per_req - n_kref)
            return (part, b * kblk_per_req + start + m, 0)
        return f

    kspecs = [pl.BlockSpec((N_HEADS, tk, HEAD_DIM), kmap(0, m)) for m in range(n_kref)]
    vspecs = [pl.BlockSpec((N_HEADS, tk, HEAD_DIM), kmap(1, m)) for m in range(n_kref)]
    cspec = pl.BlockSpec((None, None, N_HEADS, ctx_len, HEAD_DIM), lambda b, jb: (b, layer, 0, 0, 0))
    kind = lambda b, jb: (layer, (jb > 0).astype(I32) + (jb == n_rb - 1).astype(I32), 0, 0)
    return pl.pallas_call(
        _nbr_attn_kernel,
        grid=(n_req, n_rb),
        in_specs=[pl.BlockSpec((N_HEADS, tq, HEAD_DIM), lambda b, jb: (0, b * n_rb + jb, 0))]
        + kspecs + vspecs + [
            cspec, cspec,
            pl.BlockSpec((N_HEADS, None, tq, K_ROWS * KEY_COLS), kind),
            pl.BlockSpec((None, 1, D_A), lambda b, jb: (layer, 0, 0)),
        ],
        out_specs=pl.BlockSpec((tq, D_A), lambda b, jb: (b * n_rb + jb, 0)),
        out_shape=jax.ShapeDtypeStruct((n_req * seq_len, D_A), BF16),
        scratch_shapes=[pltpu.VMEM((N_HEADS, tq, HEAD_DIM), F32)],
        compiler_params=_params(("arbitrary", "arbitrary")),
        name="nbr_attention",
    )(q, *([kv] * (2 * n_kref)), cache_k, cache_v, bias_tab, gb_a)


def _mixers_kernel(u_ref, gv_ref, p_ref, pprev_ref, pnext_ref, gsgu_ref, wsgu_ref, bsgu_ref,
                   wpool_ref, spool_ref, gb_ref, o_ref, ext_scr, *, seq_len):
    i = pl.program_id(0)
    tm = u_ref.shape[0]
    blocks_per_seq = seq_len // tm
    bi = i % blocks_per_seq
    t0 = bi * tm

    gu = jax.nn.gelu(u_ref[...])
    gg = _rms(jax.nn.gelu(gv_ref[...]), gsgu_ref[...]).astype(BF16)
    cols = []
    for g in range(N_GROUPS_B):
        sl = slice(g * CHUNK, (g + 1) * CHUNK)
        chunks = []
        for n in range(tm // CHUNK):
            rows = slice(n * CHUNK, (n + 1) * CHUNK)
            chunks.append(_dot(wsgu_ref[g], gg[rows, sl]) + bsgu_ref[:, g:g + 1])
        cols.append(jnp.concatenate(chunks, axis=0))
    o_b = gu * jnp.concatenate(cols, axis=1)
    o_ref[:, 0:D_B] = _rms(o_b, gb_ref[:, 0:D_B]).astype(o_ref.dtype)

    p = p_ref[...]
    ext_scr[pl.ds(0, POOL_HALO), :] = jnp.where(bi > 0, pprev_ref[...], 0.0)
    ext_scr[pl.ds(POOL_HALO, tm), :] = p
    ext_scr[pl.ds(POOL_HALO + tm, POOL_HALO), :] = jnp.where(
        bi < blocks_per_seq - 1, pnext_ref[...], 0.0)
    t = t0 + lax.broadcasted_iota(I32, (tm, 1), 0)
    outs = []
    for g, w in enumerate(POOL_WINDOWS):
        half = w // 2
        sl = slice(g * CHUNK, (g + 1) * CHUNK)
        acc = None
        for d in range(-half, half):
            part = ext_scr[pl.ds(POOL_HALO + d, tm), sl]
            acc = part if acc is None else acc + part
        cnt = (jnp.minimum(t + half, seq_len) - jnp.maximum(t - half, 0)).astype(F32)
        pooled = acc / cnt - p[:, sl]
        outs.append(_dot(pooled.astype(BF16), wpool_ref[g]))
    o_c = jnp.concatenate(outs, axis=1) * spool_ref[...]
    o_ref[:, D_B:D_B + D_C] = _rms(o_c, gb_ref[:, D_B:D_B + D_C]).astype(o_ref.dtype)


def _mixers(ugp, g_sgu, w_sgu, b_sgu_t, w_pool, s_pool, gb_bc, layer, seq_len):
    ntok = ugp.shape[0]
    tm = min(TM_MIX, seq_len)
    nblk = ntok // tm
    hb = tm // POOL_HALO
    n_halo = ntok // POOL_HALO
    blk = lambda part: pl.BlockSpec((tm, D_B), lambda i: (i, part))
    full = lambda shape: pl.BlockSpec((None,) + shape, lambda i: (layer,) + (0,) * len(shape))
    return pl.pallas_call(
        functools.partial(_mixers_kernel, seq_len=seq_len),
        grid=(nblk,),
        in_specs=[
            blk(0), blk(1), blk(2),
            pl.BlockSpec((POOL_HALO, D_C), lambda i: (jnp.maximum(i * hb - 1, 0), 2)),
            pl.BlockSpec((POOL_HALO, D_C), lambda i: (jnp.minimum((i + 1) * hb, n_halo - 1), 2)),
            full((1, D_B)), full((N_GROUPS_B, CHUNK, CHUNK)), full((CHUNK, N_GROUPS_B)),
            full((len(POOL_WINDOWS), CHUNK, CHUNK)), full((1, D_C)), full((1, D_B + D_C)),
        ],
        out_specs=pl.BlockSpec((tm, D_B + D_C), lambda i: (i, 0)),
        out_shape=jax.ShapeDtypeStruct((ntok, D_B + D_C), BF16),
        scratch_shapes=[pltpu.VMEM((tm + 2 * POOL_HALO, D_C), F32)],
        compiler_params=_params(("arbitrary",)),
        name="mixers",
    )(ugp, ugp, ugp, ugp, ugp, g_sgu, w_sgu, b_sgu_t, w_pool, s_pool, gb_bc)


def _out_proj_kernel(x_ref, oa_ref, obc_ref, mod_ref, g_ref, w_ref, wr_ref, br_ref,
                     x1_ref, h_ref, e_ref, gate_ref):
    chunks = [pl.ds(m * OUT_CHUNK, OUT_CHUNK) for m in range(TM_OUT // OUT_CHUNK)]
    mixes = [_dot(jnp.concatenate([oa_ref[rows, :], obc_ref[rows, :]], axis=1), w_ref[...])
             for rows in chunks]
    hs = []
    for rows, mix in zip(chunks, mixes):
        x1 = x_ref[rows, :] + mod_ref[2:3, :] * mix
        x1_ref[rows, :] = x1
        h = _rms(x1, g_ref[...]) * (1.0 + mod_ref[4:5, :]) + mod_ref[3:4, :]
        hb = h.astype(BF16)
        h_ref[rows] = _pack_bf16_pairs(hb).reshape((OUT_CHUNK,) + ROW_TILE)
        hs.append(hb)
    logits = [_dot(hb, wr_ref[...]) + br_ref[...] for hb in hs]
    for rows, lg in zip(chunks, logits):
        _route_rows(rows, lg, e_ref, gate_ref)


def _route_rows(rows, logits, e_ref, gate_ref):
    lane = lax.broadcasted_iota(I32, logits.shape, 1)
    big = jnp.int32(LANES)

    def softmax_masked(mask):
        z = jnp.where(mask, logits, NEG)
        m = jnp.max(z, axis=-1, keepdims=True)
        e = jnp.where(mask, jnp.exp(z - m), 0.0)
        return e / jnp.sum(e, axis=-1, keepdims=True)

    def top1(vals, mask):
        v = jnp.where(mask, vals, -1.0)
        best = jnp.max(v, axis=-1, keepdims=True)
        idx = jnp.min(jnp.where(mask & (v == best), lane, big), axis=-1, keepdims=True)
        return best, idx

    gmask = lane < N_GROUPS_E
    p_grp = softmax_masked(gmask)
    p_g, g_sel = top1(p_grp, gmask)
    lo = N_GROUPS_E + EXPERTS_PER_GROUP * g_sel
    emask = (lane >= lo) & (lane < lo + EXPERTS_PER_GROUP)
    p_exp = softmax_masked(emask)
    p1, i1 = top1(p_exp, emask)
    p2, i2 = top1(p_exp, emask & (lane != i1))
    denom = p1 + p2
    g1 = p_g * p1 / denom
    g2 = p_g * p2 / denom
    e_ref[rows, :] = jnp.where(lane == 0, i1 - N_GROUPS_E,
                               jnp.where(lane == 1, i2 - N_GROUPS_E, 0))
    gate_ref[rows, :] = jnp.where(lane == 0, g1, jnp.where(lane == 1, g2, 0.0))


def _out_proj(x2d, ntok, o_a, o_bc, mod, mod_row_fn, g_ffn, w_out, w_r, b_r, layer):
    nblk = ntok // TM_OUT
    full = lambda shape: pl.BlockSpec((None,) + shape, lambda i: (layer,) + (0,) * len(shape))
    return pl.pallas_call(
        _out_proj_kernel,
        grid=(nblk,),
        in_specs=[
            pl.BlockSpec((TM_OUT, D_MODEL), lambda i: (i, 0)),
            pl.BlockSpec((TM_OUT, D_A), lambda i: (i, 0)),
            pl.BlockSpec((TM_OUT, D_B + D_C), lambda i: (i, 0)),
            pl.BlockSpec((None, None, N_MOD, D_MODEL), lambda i: (layer, mod_row_fn(i), 0, 0)),
            full((1, D_MODEL)), full((D_MODEL, D_MODEL)), full((D_MODEL, LANES)), full((1, LANES)),
        ],
        out_specs=[
            pl.BlockSpec((TM_OUT, D_MODEL), lambda i: (i, 0)),
            pl.BlockSpec((TM_OUT,) + ROW_TILE, lambda i: (i, 0, 0)),
            pl.BlockSpec((TM_OUT, LANES), lambda i: (i, 0)),
            pl.BlockSpec((TM_OUT, LANES), lambda i: (i, 0)),
        ],
        out_shape=[
            jax.ShapeDtypeStruct((ntok, D_MODEL), F32),
            jax.ShapeDtypeStruct((ntok,) + ROW_TILE, U32),
            jax.ShapeDtypeStruct((ntok, LANES), I32),
            jax.ShapeDtypeStruct((ntok, LANES), F32),
        ],
        compiler_params=_params(("arbitrary",)),
        name="out_proj_router",
    )(x2d, o_a, o_bc, mod, g_ffn, w_out, w_r, b_r)


def _route_plan(e_flat):
    n_assign = e_flat.shape[0]
    onehot = (e_flat[:, None] == jnp.arange(N_EXPERTS, dtype=I32)[None, :]).astype(I32)
    csum = jnp.cumsum(onehot, axis=0)
    counts = csum[-1]
    rank = jnp.sum(onehot * (csum - 1), axis=1)
    padded = (counts + MOE_BM - 1) // MOE_BM * MOE_BM
    pad_end = jnp.cumsum(padded)
    pad_start = pad_end - padded
    pos = jnp.sum(onehot * pad_start[None, :], axis=1) + rank
    n_blocks = n_assign // MOE_BM + N_EXPERTS
    starts = jnp.arange(n_blocks, dtype=I32) * MOE_BM
    block_e = jnp.minimum(jnp.sum((starts[:, None] >= pad_end[None, :]).astype(I32), axis=1),
                          N_EXPERTS - 1).astype(I32)
    n_used = (pad_end[-1] // MOE_BM).astype(I32).reshape(1)
    fill_start = jnp.where(padded > 0, pad_end - MOE_BM, -1).astype(I32)
    has_rows = padded > 0
    block_first = jnp.concatenate(
        [jnp.ones((1,), I32), (block_e[1:] != block_e[:-1]).astype(I32)])
    ids = jnp.arange(N_EXPERTS, dtype=I32)
    later = jnp.where(has_rows[None, :] & (ids[None, :] > ids[:, None]), ids[None, :], N_EXPERTS)
    next_e = jnp.min(later, axis=1)
    next_e = jnp.where(next_e < N_EXPERTS, next_e, -1).astype(I32)
    buf_e = ((jnp.cumsum(has_rows.astype(I32)) - 1) % 2).astype(I32)
    plan = (block_e, n_used, block_first, next_e, buf_e)
    return pos.astype(I32), plan, n_used, fill_start, n_blocks


def _dispatch_kernel(fill_ref, nu_ref, pos_ref, hp_ref, hs_ref, xs_ref, zero_scr, sem, fill_sem,
                     *, n_ctx_blocks, n_blocks):
    i = pl.program_id(0)

    @pl.when(i == 0)
    def _():
        zero_scr[...] = jnp.zeros_like(zero_scr)

        def fill(start):
            start = pl.multiple_of(start, MOE_BM)
            return pltpu.make_async_copy(zero_scr, xs_ref.at[pl.ds(start, MOE_BM)], fill_sem)

        def expert_fill(e, carry):
            pl.when(fill_ref[e] >= 0)(lambda: fill(fill_ref[e]).start())
            return carry

        def expert_wait(e, carry):
            pl.when(fill_ref[e] >= 0)(lambda: fill(0).wait())
            return carry

        def tail_fill(b, carry):
            fill(b * MOE_BM).start()
            return carry

        def tail_wait(b, carry):
            fill(0).wait()
            return carry

        lax.fori_loop(0, N_EXPERTS, expert_fill, 0)
        lax.fori_loop(nu_ref[0], n_blocks, tail_fill, 0)
        lax.fori_loop(0, N_EXPERTS, expert_wait, 0)
        lax.fori_loop(nu_ref[0], n_blocks, tail_wait, 0)

    def scatter(h_ref):
        def issue(r, carry):
            for k in range(TOP_K):
                slot = pos_ref[0, 0, TOP_K * r + k]
                pltpu.make_async_copy(h_ref.at[r], xs_ref.at[slot], sem).start(priority=k)
            return carry

        lax.fori_loop(0, TM_ROW, issue, 0, unroll=8)
        for _ in range(TOP_K):
            pltpu.make_async_copy(h_ref, h_ref, sem).wait()

    pl.when(i < n_ctx_blocks)(lambda: scatter(hp_ref))
    pl.when(i >= n_ctx_blocks)(lambda: scatter(hs_ref))


def _dispatch(fill_start, n_used, pos3, hp, hs, n_blocks):
    nbp = hp.shape[0] // TM_ROW
    nbs = hs.shape[0] // TM_ROW
    grid_spec = pltpu.PrefetchScalarGridSpec(
        num_scalar_prefetch=2,
        grid=(nbp + nbs,),
        in_specs=[
            pl.BlockSpec((1, 1, TOP_K * TM_ROW), lambda i, fs, nu: (i, 0, 0),
                         memory_space=pltpu.SMEM),
            pl.BlockSpec((TM_ROW,) + ROW_TILE, lambda i, fs, nu: (jnp.minimum(i, nbp - 1), 0, 0)),
            pl.BlockSpec((TM_ROW,) + ROW_TILE, lambda i, fs, nu: (jnp.maximum(i - nbp, 0), 0, 0)),
        ],
        out_specs=pl.BlockSpec(memory_space=pl.ANY),
        scratch_shapes=[pltpu.VMEM((MOE_BM,) + ROW_TILE, U32), pltpu.SemaphoreType.DMA(()),
                        pltpu.SemaphoreType.DMA(())],
    )
    return pl.pallas_call(
        functools.partial(_dispatch_kernel, n_ctx_blocks=nbp, n_blocks=n_blocks),
        grid_spec=grid_spec,
        out_shape=jax.ShapeDtypeStruct((n_blocks * MOE_BM,) + ROW_TILE, U32),
        compiler_params=_params(("arbitrary",)),
        name="moe_dispatch",
    )(fill_start, n_used, pos3, hp, hs)


def _experts_kernel(be_ref, nu_ref, first_ref, next_ref, buf_ref, xs_ref, wg_hbm, wu_hbm, wd_hbm,
                    ys_ref, wg_buf, wu_buf, wd_buf, sem, *, layer):
    i = pl.program_id(0)

    def weight_copies(e, b):
        return [pltpu.make_async_copy(src.at[layer, e], dst.at[b], sem.at[b])
                for src, dst in ((wg_hbm, wg_buf), (wu_hbm, wu_buf), (wd_hbm, wd_buf))]

    @pl.when(i < nu_ref[0])
    def _():
        e = be_ref[i]
        b = buf_ref[e]

        @pl.when(i == 0)
        def _():
            for cp in weight_copies(e, b):
                cp.start()

        @pl.when(first_ref[i] == 1)
        def _():
            nxt = next_ref[e]

            @pl.when(nxt >= 0)
            def _():
                for cp in weight_copies(nxt, 1 - b):
                    cp.start()

            for cp in weight_copies(e, b):
                cp.wait()

        x = _unpack_bf16_pairs(xs_ref[...].reshape(MOE_BM, D_PACK), BF16)
        g = _dot(x, wg_buf[b].astype(BF16))
        u = _dot(x, wu_buf[b].astype(BF16))
        a = (g * jax.nn.sigmoid(g) * u).astype(BF16)
        y = _dot(a, wd_buf[b].astype(BF16)).astype(BF16)
        ys_ref[...] = _pack_bf16_pairs(y).reshape(ys_ref.shape)

    @pl.when(i >= nu_ref[0])
    def _():
        ys_ref[...] = jnp.zeros_like(ys_ref)


def _experts(plan, xs, wg, wu, wd, layer, n_blocks):
    grid_spec = pltpu.PrefetchScalarGridSpec(
        num_scalar_prefetch=len(plan),
        grid=(n_blocks,),
        in_specs=[
            pl.BlockSpec((MOE_BM,) + ROW_TILE,
                         lambda i, be, nu, *_: (jnp.minimum(i, nu[0] - 1), 0, 0)),
            pl.BlockSpec(memory_space=pl.ANY),
            pl.BlockSpec(memory_space=pl.ANY),
            pl.BlockSpec(memory_space=pl.ANY),
        ],
        out_specs=pl.BlockSpec((MOE_BM,) + ROW_TILE, lambda i, *_: (i, 0, 0)),
        scratch_shapes=[
            pltpu.VMEM((2, D_MODEL, D_EXPERT), F32),
            pltpu.VMEM((2, D_MODEL, D_EXPERT), F32),
            pltpu.VMEM((2, D_EXPERT, D_MODEL), F32),
            pltpu.SemaphoreType.DMA((2,)),
        ],
    )
    return pl.pallas_call(
        functools.partial(_experts_kernel, layer=layer),
        grid_spec=grid_spec,
        out_shape=jax.ShapeDtypeStruct(xs.shape, U32),
        compiler_params=_params(("arbitrary",)),
        name="moe_experts",
    )(*plan, xs, wg, wu, wd)


def _combine_kernel(pos_ref, pos_next_ref, x1_ref, gate_ref, mod_ref, gfin_ref, ys_ref, o_ref,
                    ybuf, sem, *, final_norm):
    i = pl.program_id(0)
    buf = i % 2
    last = i == pl.num_programs(0) - 1

    def start_row(p_ref, b, r):
        for k in range(TOP_K):
            slot = p_ref[0, 0, TOP_K * r + k]
            pltpu.make_async_copy(ys_ref.at[slot], ybuf.at[b, k, r], sem.at[b]).start(priority=k)

    def wait_buffer(b):
        for k in range(TOP_K):
            pltpu.make_async_copy(ybuf.at[b, k], ybuf.at[b, k], sem.at[b]).wait()

    @pl.when(i == 0)
    def _():
        lax.fori_loop(0, TM_ROW, lambda r, c: (start_row(pos_ref, 0, r), c)[1], 0, unroll=8)

    wait_buffer(buf)
    for m in range(TM_ROW // COMBINE_CHUNK):
        for r in range(m * COMBINE_CHUNK, (m + 1) * COMBINE_CHUNK):
            start_row(pos_next_ref, 1 - buf, r)
        rows = pl.ds(m * COMBINE_CHUNK, COMBINE_CHUNK)
        y0 = _unpack_bf16_pairs(ybuf[buf, 0, rows].reshape(COMBINE_CHUNK, D_PACK), F32)
        y1 = _unpack_bf16_pairs(ybuf[buf, 1, rows].reshape(COMBINE_CHUNK, D_PACK), F32)
        f = gate_ref[rows, 0:1] * y0 + gate_ref[rows, 1:2] * y1
        x2 = x1_ref[rows, :] + mod_ref[5:6, :] * f
        if final_norm:
            x2 = _rms(x2, gfin_ref[...])
        o_ref[rows, :] = x2

    pl.when(last)(lambda: wait_buffer(1 - buf))


def _combine(pos3, x1, gate, mod, mod_row_fn, g_final, ys, layer, final_norm):
    ntok = x1.shape[0]
    nblk = ntok // TM_ROW
    return pl.pallas_call(
        functools.partial(_combine_kernel, final_norm=final_norm),
        grid=(nblk,),
        in_specs=[
            pl.BlockSpec((1, 1, TOP_K * TM_ROW), lambda i: (i, 0, 0), memory_space=pltpu.SMEM),
            pl.BlockSpec((1, 1, TOP_K * TM_ROW), lambda i: (jnp.minimum(i + 1, nblk - 1), 0, 0),
                         memory_space=pltpu.SMEM),
            pl.BlockSpec((TM_ROW, D_MODEL), lambda i: (i, 0)),
            pl.BlockSpec((TM_ROW, LANES), lambda i: (i, 0)),
            pl.BlockSpec((None, None, N_MOD, D_MODEL), lambda i: (layer, mod_row_fn(i), 0, 0)),
            pl.BlockSpec((1, D_MODEL), lambda i: (0, 0)),
            pl.BlockSpec(memory_space=pl.ANY),
        ],
        out_specs=pl.BlockSpec((TM_ROW, D_MODEL), lambda i: (i, 0)),
        out_shape=jax.ShapeDtypeStruct((ntok, D_MODEL), F32),
        scratch_shapes=[pltpu.VMEM((2, TOP_K, TM_ROW) + ROW_TILE, U32),
                        pltpu.SemaphoreType.DMA((2,))],
        compiler_params=_params(("arbitrary",)),
        name="moe_combine",
    )(pos3, pos3, x1, gate, mod, g_final, ys)


def kernel(x_prompt, x_sample, cache_k, cache_v, c, c_ctx, w_mod, b_mod, g_mix, g_ffn, w_in, rpb,
           g_sgu, w_sgu, b_sgu, w_pool, s_pool, g_branch, w_out, w_rg, b_rg, w_re, b_re,
           w_e_gate, w_e_up, w_e_down, g_final):
    n_ctx, seq, _ = x_prompt.shape
    n_lat, lat_seq, _ = x_sample.shape
    ntok_p = n_ctx * seq
    ntok_s = n_lat * lat_seq
    rows = lat_seq // GRID_W

    n_mod_rows = -(-(1 + n_lat) // SUBLANES) * SUBLANES
    c_all = jnp.concatenate(
        [c_ctx[None], c, jnp.zeros((n_mod_rows - 1 - n_lat, D_MODEL), F32)], axis=0)
    mod = _modulation(c_all, w_mod, b_mod).reshape(DEPTH, n_mod_rows, N_MOD, D_MODEL)

    xp = x_prompt.reshape(ntok_p, D_MODEL)
    xs_lat = x_sample.reshape(ntok_s, D_MODEL)

    def lat_mod_row(tm):
        return lambda i: 1 + (i * tm) // lat_seq

    ctx_mod_row = lambda i: 0

    w_in_b = w_in.astype(BF16)
    w_out_b = w_out.astype(BF16)
    w_sgu_b = w_sgu.astype(BF16)
    w_pool_b = w_pool.astype(BF16)

    gmix = g_mix[:, None, :]
    gffn = g_ffn[:, None, :]
    gb_a = g_branch[:, None, :D_A]
    gb_bc = g_branch[:, None, D_A:]
    gsgu = g_sgu[:, None, :]
    bsgu_t = jnp.swapaxes(b_sgu, 1, 2)
    spool = s_pool[:, None, :]
    n_route = N_GROUPS_E + N_EXPERTS
    w_r = jnp.concatenate(
        [w_rg, w_re, jnp.zeros((DEPTH, D_MODEL, LANES - n_route), F32)], axis=2)
    w_r = w_r.astype(BF16)
    b_r = jnp.concatenate(
        [b_rg, b_re, jnp.zeros((DEPTH, LANES - n_route), F32)], axis=1)[:, None, :]
    bias_tab = _local_bias_table(rpb.reshape((DEPTH * N_HEADS,) + rpb.shape[2:]), rows)

    cache_shape = (n_ctx, DEPTH, N_HEADS, seq, HEAD_DIM)
    caches = (jnp.zeros(cache_shape, F32), jnp.zeros(cache_shape, F32))
    for l in range(DEPTH):
        qkv_p, ugp_p, *caches = _in_proj(
            xp, ntok_p, mod, ctx_mod_row, gmix, w_in_b, l, seq, True, tuple(caches))
        oa_p = _ctx_attention(qkv_p, gb_a, l, seq)
        obc_p = _mixers(ugp_p, gsgu, w_sgu_b, bsgu_t, w_pool_b, spool, gb_bc, l, seq)
        x1p, hp, ep, gatep = _out_proj(
            xp, ntok_p, oa_p, obc_p, mod, ctx_mod_row, gffn, w_out_b, w_r, b_r, l)

        q_s, kv_s, ugp_s = _in_proj(
            xs_lat, ntok_s, mod, lat_mod_row(TM_IN), gmix, w_in_b, l, lat_seq, False)
        oa_s = _nbr_attention(q_s, kv_s, cache_k, cache_v, l, bias_tab, gb_a, n_lat, lat_seq)
        obc_s = _mixers(ugp_s, gsgu, w_sgu_b, bsgu_t, w_pool_b, spool, gb_bc, l, lat_seq)
        x1s, hs, es, gates = _out_proj(
            xs_lat, ntok_s, oa_s, obc_s, mod, lat_mod_row(TM_OUT), gffn, w_out_b, w_r, b_r, l)

        e_flat = jnp.concatenate([ep[:, :TOP_K].reshape(-1), es[:, :TOP_K].reshape(-1)])
        pos, plan, n_used, fill_start, n_blocks = _route_plan(e_flat)
        pos3 = pos.reshape((ntok_p + ntok_s) // TM_ROW, 1, TOP_K * TM_ROW)
        pos_p = pos3[:ntok_p // TM_ROW]
        pos_s = pos3[ntok_p // TM_ROW:]
        slots = _dispatch(fill_start, n_used, pos3, hp, hs, n_blocks)
        ys = _experts(plan, slots, w_e_gate, w_e_up, w_e_down, l, n_blocks)
        last = l == DEPTH - 1
        gfin = g_final[None]
        xp = _combine(pos_p, x1p, gatep, mod, ctx_mod_row, gfin, ys, l, last)
        xs_lat = _combine(pos_s, x1s, gates, mod, lat_mod_row(TM_ROW), gfin, ys, l, last)

    y_prompt = xp.reshape(n_ctx, seq, D_MODEL)
    y_sample = xs_lat.reshape(n_lat, lat_seq, D_MODEL)
    return (y_prompt, y_sample, caches[0], caches[1])
```
